```python
import math
import jax, jax.numpy as jnp
from jax import lax
import numpy as np

D_MODEL = 1024
BATCH = 8
SEQ = 2048
DEPTH = 1
DEC_BATCH = 128
DEC_SEQ = 4
PAST_LEN = 16384
PAGE_SIZE = 128

RWKV_HEADS = 8
RWKV_HEAD_DIM = 64
RWKV_WIDTH = RWKV_HEADS * RWKV_HEAD_DIM
D_DECAY_LORA = 64
D_AAA_LORA = 64
D_GATE_LORA = 128
LNX_EPS = 64e-5
HGRN_HEADS = 4
HGRN_EXPAND = 128
HGRN_HEAD_DIM = 128
HGRN_FDIM = HGRN_HEADS * HGRN_EXPAND
HGRN_WIDTH = HGRN_HEADS * HGRN_HEAD_DIM
HGRN_CHUNK = 16
MIX_WIDTH = RWKV_WIDTH + HGRN_WIDTH
SHIFT_WIDTH = 3 * RWKV_WIDTH + D_DECAY_LORA + D_AAA_LORA + D_GATE_LORA
IN_WIDTH = SHIFT_WIDTH + 2 * HGRN_FDIM + 2 * HGRN_WIDTH
D_FF = 4 * D_MODEL
NORM_EPS = 1e-6

kernel_name = 'rwkv7_hgrn2_parallel_hybrid_step'


def _rmsnorm(x, gain):
    x32 = x.astype(jnp.float32)
    y = x32 * lax.rsqrt(jnp.mean(x32 * x32, axis=-1, keepdims=True) + NORM_EPS)
    return (y * gain.astype(jnp.float32)).astype(x.dtype)


def _wkv7_scan(r, decay, k, v, kk, a, s0):
    def step(s, inp):
        r_t, w_t, k_t, v_t, kk_t, a_t = inp
        sa = jnp.einsum('bhvk,bhk->bhv', s, -kk_t)
        s = (s * w_t[:, :, None, :] + sa[..., None] * (kk_t * a_t)[:, :, None, :]
             + v_t[..., None] * k_t[:, :, None, :])
        o = jnp.einsum('bhvk,bhk->bhv', s, r_t)
        return s, o
    xs = tuple(jnp.swapaxes(t, 0, 1) for t in (r, decay, k, v, kk, a))
    s_final, o = lax.scan(step, s0, xs)
    return jnp.swapaxes(o, 0, 1), s_final


def _hgrn2_chunked(q, k, log_f, v, s0):
    B, T, H, F = q.shape
    I = v.shape[-1]
    C = math.gcd(T, HGRN_CHUNK)
    n = T // C

    def to_chunks(t):
        return t.reshape(B, n, C, H, t.shape[-1]).transpose(1, 0, 3, 2, 4)

    causal = jnp.tril(jnp.ones((C, C), dtype=bool))

    def step(s, inp):
        qc, kc, gc, vc = inp
        b = jnp.cumsum(gc, axis=2)
        o_inter = jnp.einsum('bhtf,bhfi->bhti', qc * jnp.exp(b), s)
        diff = b[:, :, :, None, :] - b[:, :, None, :, :]
        dec = jnp.where(causal[:, :, None], jnp.exp(jnp.minimum(diff, 0.0)), 0.0)
        att = jnp.einsum('bhtf,bhtsf,bhsf->bhts', qc, dec, kc)
        o_intra = jnp.einsum('bhts,bhsi->bhti', att, vc)
        b_last = b[:, :, -1:, :]
        s = (jnp.exp(b_last[:, :, 0, :])[..., None] * s
             + jnp.einsum('bhsf,bhsi->bhfi', kc * jnp.exp(b_last - b), vc))
        return s, o_inter + o_intra

    s_final, o = lax.scan(step, s0, tuple(to_chunks(t) for t in (q, k, log_f, v)))
    o = o.transpose(1, 0, 3, 2, 4).reshape(B, T, H, I)
    return o, s_final


def _token_mixers(h, shift_prev, wkv_prev, hgrn_prev, l, w):
    B, T, _ = h.shape
    dt = h.dtype
    f32 = jnp.float32
    p = h @ w['w_in'][l]
    p_rw, p_hg = p[..., :SHIFT_WIDTH], p[..., SHIFT_WIDTH:]
    p_prev = jnp.concatenate([shift_prev[:, None, :].astype(dt), p_rw[:, :-1]], axis=1)
    x_rw = p_rw + (p_prev - p_rw) * w['mu_shift'][l]
    new_shift = p_rw[:, -1]
    o1 = RWKV_WIDTH
    o2 = 2 * RWKV_WIDTH
    o3 = 3 * RWKV_WIDTH
    o4 = o3 + D_DECAY_LORA
    o5 = o4 + D_AAA_LORA
    r, k, v, wd, ad, gd = jnp.split(x_rw, [o1, o2, o3, o4, o5], axis=-1)
    r, k, v = r.astype(f32), k.astype(f32), v.astype(f32)
    w_log = -jax.nn.softplus(-(w['w0'][l] + jnp.tanh(wd) @ w['w_decay_up'][l]).astype(f32)) - 0.5
    decay = jnp.exp(-jnp.exp(w_log))
    a = jax.nn.sigmoid((w['a0'][l] + ad @ w['w_aaa_up'][l]).astype(f32))
    g = (jax.nn.sigmoid(gd) @ w['w_gate_up'][l]).astype(f32)

    def heads(t):
        return t.reshape(B, T, RWKV_HEADS, RWKV_HEAD_DIM)

    kk = heads(k * w['k_k'][l])
    kk = kk / jnp.maximum(jnp.sqrt(jnp.sum(kk * kk, axis=-1, keepdims=True)), 1e-12)
    k = k * (1.0 + (a - 1.0) * w['k_a'][l])
    rh, kh, vh, ah, dh = heads(r), heads(k), heads(v), heads(a), heads(decay)
    o, wkv_new = _wkv7_scan(rh, dh, kh, vh, kk, ah, wkv_prev.astype(f32))
    mean = jnp.mean(o, axis=-1, keepdims=True)
    var = jnp.mean(jnp.square(o - mean), axis=-1, keepdims=True)
    o = ((o - mean) * lax.rsqrt(var + LNX_EPS)).reshape(B, T, RWKV_WIDTH) * w['lnx_w'][l] + w['lnx_b'][l]
    bonus = jnp.sum(rh * kh * w['r_k'][l], axis=-1, keepdims=True) * vh
    y_rw = (o + bonus.reshape(B, T, RWKV_WIDTH)) * g

    q, f_raw, i_in, og = jnp.split(p_hg, [HGRN_FDIM, 2 * HGRN_FDIM, 2 * HGRN_FDIM + HGRN_WIDTH], axis=-1)
    lb = jnp.cumsum(jax.nn.softmax(w['hgrn_lb'].astype(f32), axis=0), axis=0)[l]
    f = lb + (1.0 - lb) * jax.nn.sigmoid(f_raw.astype(f32))
    qh = jax.nn.silu(q.astype(f32)).reshape(B, T, HGRN_HEADS, HGRN_EXPAND)
    kf = (1.0 - f).reshape(B, T, HGRN_HEADS, HGRN_EXPAND)
    gf = jnp.log(f).reshape(B, T, HGRN_HEADS, HGRN_EXPAND)
    ih = i_in.astype(f32).reshape(B, T, HGRN_HEADS, HGRN_HEAD_DIM)
    o_h, hgrn_new = _hgrn2_chunked(qh, kf, gf, ih, hgrn_prev.astype(f32))
    o_h = o_h * lax.rsqrt(jnp.mean(o_h * o_h, axis=-1, keepdims=True) + NORM_EPS) * w['hgrn_gnorm'][l]
    y_hg = o_h.reshape(B, T, HGRN_WIDTH) * jax.nn.silu(og.astype(f32))

    y = jnp.concatenate([y_rw, y_hg], axis=-1).astype(dt) @ w['w_out'][l]
    return y, new_shift.astype(shift_prev.dtype), wkv_new.astype(wkv_prev.dtype), hgrn_new.astype(hgrn_prev.dtype)


def _trunk(x, c, shift0, wkv0, hgrn0, w):
    shifts, wkvs, hgrns = [], [], []
    for l in range(DEPTH):
        mod = jax.nn.silu(c) @ w['w_ada'][l] + w['b_ada'][l]
        sh1, sc1, gt1, sh2, sc2, gt2 = jnp.split(mod, 6, axis=-1)
        h = _rmsnorm(x, w['norm1'][l]) * (1.0 + sc1[:, None]) + sh1[:, None]
        y, s_sh, s_wkv, s_hg = _token_mixers(h, shift0[l], wkv0[l], hgrn0[l], l, w)
        x = x + gt1[:, None] * y
        h = _rmsnorm(x, w['norm2'][l]) * (1.0 + sc2[:, None]) + sh2[:, None]
        x = x + gt2[:, None] * (jnp.square(jax.nn.relu(h @ w['w_up'][l])) @ w['w_down'][l])
        shifts.append(s_sh)
        wkvs.append(s_wkv)
        hgrns.append(s_hg)
    return _rmsnorm(x, w['norm_f']), jnp.stack(shifts), jnp.stack(wkvs), jnp.stack(hgrns)


def setup_inputs(seed: int = 0) -> dict:
    key = jax.random.key(seed)
    ks = iter(jax.random.split(key, 32))

    def nrm(shape, scale):
        return jax.random.normal(next(ks), shape, jnp.float32) * scale

    L = DEPTH
    return {
        'x_prompt': nrm((BATCH, SEQ, D_MODEL), 1.0),
        'x_sample': nrm((DEC_BATCH, DEC_SEQ, D_MODEL), 1.0),
        'c_prompt': nrm((BATCH, D_MODEL), 1.0),
        'c_sample': nrm((DEC_BATCH, D_MODEL), 1.0),
        'state_shift': nrm((L, DEC_BATCH, SHIFT_WIDTH), 1.0),
        'state_wkv': nrm((L, DEC_BATCH, RWKV_HEADS, RWKV_HEAD_DIM, RWKV_HEAD_DIM), 0.3),
        'state_hgrn': nrm((L, DEC_BATCH, HGRN_HEADS, HGRN_EXPAND, HGRN_HEAD_DIM), 0.3),
        'norm1': 1.0 + nrm((L, D_MODEL), 0.05),
        'norm2': 1.0 + nrm((L, D_MODEL), 0.05),
        'norm_f': 1.0 + nrm((D_MODEL,), 0.05),
        'w_ada': nrm((L, D_MODEL, 6 * D_MODEL), D_MODEL ** -0.5),
        'b_ada': nrm((L, 6 * D_MODEL), 0.01),
        'w_in': nrm((L, D_MODEL, IN_WIDTH), D_MODEL ** -0.5),
        'mu_shift': jax.random.uniform(next(ks), (L, SHIFT_WIDTH), jnp.float32),
        'w0': jnp.linspace(-6.5, -0.5, RWKV_WIDTH, dtype=jnp.float32)[None, :] + nrm((L, RWKV_WIDTH), 0.1),
        'w_decay_up': nrm((L, D_DECAY_LORA, RWKV_WIDTH), 0.5 * D_DECAY_LORA ** -0.5),
        'a0': nrm((L, RWKV_WIDTH), 0.1),
        'w_aaa_up': nrm((L, D_AAA_LORA, RWKV_WIDTH), 0.5 * D_AAA_LORA ** -0.5),
        'w_gate_up': nrm((L, D_GATE_LORA, RWKV_WIDTH), D_GATE_LORA ** -0.5),
        'k_k': 0.85 + nrm((L, RWKV_WIDTH), 0.05),
        'k_a': 1.0 + nrm((L, RWKV_WIDTH), 0.05),
        'r_k': nrm((L, RWKV_HEADS, RWKV_HEAD_DIM), 0.1),
        'lnx_w': 1.0 + nrm((L, RWKV_WIDTH), 0.05),
        'lnx_b': nrm((L, RWKV_WIDTH), 0.01),
        'hgrn_lb': nrm((L + 1, HGRN_FDIM), 0.1),
        'hgrn_gnorm': 1.0 + nrm((L, HGRN_HEAD_DIM), 0.05),
        'w_out': nrm((L, MIX_WIDTH, D_MODEL), MIX_WIDTH ** -0.5),
        'w_up': nrm((L, D_MODEL, D_FF), D_MODEL ** -0.5),
        'w_down': nrm((L, D_FF, D_MODEL), D_FF ** -0.5),
    }


def reference(x_prompt, x_sample, c_prompt, c_sample, state_shift, state_wkv, state_hgrn,
              norm1, norm2, norm_f, w_ada, b_ada, w_in, mu_shift, w0, w_decay_up, a0, w_aaa_up,
              w_gate_up, k_k, k_a, r_k, lnx_w, lnx_b, hgrn_lb, hgrn_gnorm, w_out, w_up, w_down):
    w = {'norm1': norm1, 'norm2': norm2, 'norm_f': norm_f, 'w_ada': w_ada, 'b_ada': b_ada,
         'w_in': w_in, 'mu_shift': mu_shift, 'w0': w0, 'w_decay_up': w_decay_up, 'a0': a0,
         'w_aaa_up': w_aaa_up, 'w_gate_up': w_gate_up, 'k_k': k_k, 'k_a': k_a, 'r_k': r_k,
         'lnx_w': lnx_w, 'lnx_b': lnx_b, 'hgrn_lb': hgrn_lb, 'hgrn_gnorm': hgrn_gnorm,
         'w_out': w_out, 'w_up': w_up, 'w_down': w_down}
    bp = x_prompt.shape[0]
    dt = x_prompt.dtype
    shift0 = jnp.zeros((DEPTH, bp, SHIFT_WIDTH), dt)
    wkv0 = jnp.zeros((DEPTH, bp, RWKV_HEADS, RWKV_HEAD_DIM, RWKV_HEAD_DIM), dt)
    hgrn0 = jnp.zeros((DEPTH, bp, HGRN_HEADS, HGRN_EXPAND, HGRN_HEAD_DIM), dt)
    y_prompt, shift_p, wkv_p, hgrn_p = _trunk(x_prompt, c_prompt, shift0, wkv0, hgrn0, w)
    y_sample, shift_s, wkv_s, hgrn_s = _trunk(x_sample, c_sample, state_shift, state_wkv, state_hgrn, w)
    return (y_prompt, y_sample, shift_p, wkv_p, hgrn_p, shift_s, wkv_s, hgrn_s)
```

```python
import functools

import jax
import jax.numpy as jnp
from jax import lax
from jax.experimental import pallas as pl
from jax.experimental.pallas import tpu as pltpu

F32 = jnp.float32
BF16 = jnp.bfloat16

D_MODEL = 1024
RWKV_HEADS = 8
RWKV_WIDTH = 512
HGRN_HEADS = 4
HGRN_WIDTH = 512
SHIFT_WIDTH = 1792
IN_WIDTH = 3840
D_FF = 4096
NORM_EPS = 1e-6
LNX_EPS = 64e-5

LANES = 128
HEAD_PAIR = 128
CHUNK = 64
SUB = 16
VMEM_LIMIT = 58 * 1024 * 1024


def _dot(a, b):
    return jnp.dot(a.astype(BF16), b.astype(BF16), preferred_element_type=F32)


def _dot_nt(a, b):
    return lax.dot_general(a.astype(BF16), b.astype(BF16), (((1,), (1,)), ((), ())),
                           preferred_element_type=F32)


def _split3(x):
    hi = x.astype(BF16)
    r1 = x - hi.astype(F32)
    mid = r1.astype(BF16)
    lo = (r1 - mid.astype(F32)).astype(BF16)
    return hi, mid, lo


def _dot_exact_l(m, x):
    mb = m.astype(BF16)
    hi, mid, lo = _split3(x)
    d = lambda part: jnp.dot(mb, part, preferred_element_type=F32)
    return d(hi) + d(mid) + d(lo)


def _dot_exact_r(x, m):
    mb = m.astype(BF16)
    hi, mid, lo = _split3(x)
    d = lambda part: jnp.dot(part, mb, preferred_element_type=F32)
    return d(hi) + d(mid) + d(lo)


def _iota(shape, dim):
    return lax.broadcasted_iota(jnp.int32, shape, dim)


def _shr(x, n):
    return lax.shift_right_logical(x, jnp.int32(n.bit_length() - 1))


def _sigmoid(x):
    return 1.0 / (1.0 + jnp.exp(-x))


def _silu(x):
    return x * _sigmoid(x)


def _softplus(x):
    return jnp.maximum(x, 0.0) + jnp.log(1.0 + jnp.exp(-jnp.abs(x)))


def _rms(x, gain):
    return x * lax.rsqrt(jnp.mean(x * x, axis=-1, keepdims=True) + NORM_EPS) * gain


def _make_masks(C, Ls):
    row = _iota((C, C), 0)
    col = _iota((C, C), 1)
    seq = lambda t: _shr(t, Ls)
    blk = lambda t: _shr(t, SUB)
    same = seq(row) == seq(col)
    mrow = blk(row) * SUB + (SUB // 2 - 1)
    erow = blk(row) * SUB + (SUB - 1)
    R = _iota((2 * C, 2 * C), 0)
    Q = _iota((2 * C, 2 * C), 1)
    tr = R & (C - 1)
    tq = Q & (C - 1)
    same2 = seq(tr) == seq(tq)
    lvl = []
    s = 1
    while s < Ls:
        lvl.append((_shr(R, 2 * s) == _shr(Q, 2 * s)) & ((R & (2 * s - 1)) >= s) & ((Q & (2 * s - 1)) < s))
        s *= 2
    return dict(
        row=row, col=col, same=same,
        mi=same & (col <= row),
        mmid=(seq(mrow) == seq(col)) & (col <= mrow),
        mend=(seq(erow) == seq(col)) & (col <= erow),
        sameblk=blk(row) == blk(col),
        eye2=(R == Q).astype(F32),
        lvl=lvl,
        ms_hi2=same2 & (tq < tr) & (Q >= C),
        mi22=same2 & (tq <= tr),
    )


def _tri_inverse(G, mk):
    lvl = mk['lvl']
    D = mk['eye2'] + jnp.where(lvl[0], G, 0.0)
    for m in lvl[1:]:
        D = D + _dot(D, _dot(jnp.where(m, G, 0.0), D))
    return D


def _select_seq(full, n_seq, sid_rows):
    if n_seq == 1:
        return full
    acc = jnp.where(sid_rows == 0, full[:, 0:LANES], 0.0)
    for s in range(1, n_seq):
        acc = acc + jnp.where(sid_rows == s, full[:, s * LANES:(s + 1) * LANES], 0.0)
    return acc


def _expand_seq(x, n_seq, sid_rows):
    if n_seq == 1:
        return x
    return jnp.concatenate([jnp.where(sid_rows == s, x, 0.0) for s in range(n_seq)], axis=1)


def _wkv_pair_chunk(r, lw, k, v, kk, a, S, mk, Ls):
    C = r.shape[0]
    n_seq = C // Ls
    lo_half = _iota((C, HEAD_PAIR), 1) < 64
    own = (_iota((2 * C, HEAD_PAIR), 1) < 64) == (_iota((2 * C, HEAD_PAIR), 0) < C)
    stack = lambda t: jnp.where(own, jnp.concatenate([t, t], axis=0), 0.0)
    cw = _dot_exact_l(mk['mi'].astype(F32), lw)
    e_neg = jnp.exp(-cw)
    at = -kk * jnp.exp(cw - lw)
    bt = kk * a * e_neg
    kt = k * e_neg
    rt = r * jnp.exp(cw)
    Y = jnp.concatenate([bt, kt], axis=0)
    at2 = stack(at)
    D = _tri_inverse(_dot_nt(at2, jnp.concatenate([bt, bt], axis=0)), mk)
    ga = _dot_nt(at2, Y)
    gr = _dot_nt(stack(rt), Y)
    sid2 = _shr(_iota((2 * C, LANES), 0) & (C - 1), Ls)
    XO = _select_seq(_dot_nt(jnp.concatenate([at, rt], axis=0), S), n_seq, sid2)
    Vz = jnp.concatenate([jnp.zeros_like(v), v], axis=0)
    xk = _dot(jnp.where(mk['ms_hi2'], ga, 0.0), Vz)
    X = XO[:C] + jnp.where(lo_half, xk[:C], xk[C:])
    DX = _dot(D, jnp.concatenate([X, X], axis=0))
    U = jnp.where(lo_half, DX[:C], DX[C:])
    Z = jnp.concatenate([U, v], axis=0)
    oz = _dot(jnp.where(mk['mi22'], gr, 0.0), Z)
    o = XO[C:] + jnp.where(lo_half, oz[:C], oz[C:])
    upd = _dot(Z.T, _expand_seq(Y, n_seq, sid2))
    wl = jnp.exp(_dot_exact_l(mk['same'].astype(F32), lw))
    bd = _shr(_iota((HEAD_PAIR, HEAD_PAIR), 0), 64) == _shr(_iota((HEAD_PAIR, HEAD_PAIR), 1), 64)
    new = []
    for s in range(n_seq):
        Ss = S[s * LANES:(s + 1) * LANES, :]
        new.append(jnp.where(bd, (Ss + upd[:, s * LANES:(s + 1) * LANES]) * wl[s * Ls:s * Ls + 1, :], 0.0))
    return o, new


def _hgrn_head_chunk(q, f, iv, ST, mk, Ls):
    C = q.shape[0]
    n_seq = C // Ls
    sid = _shr(_iota((C, LANES), 0), Ls)
    gf = jnp.log(f)
    kf = 1.0 - f
    b = _dot_exact_l(mk['mi'].astype(F32), gf)
    o = _select_seq(_dot_nt(q * jnp.exp(b), ST), n_seq, sid)
    bmid = _dot_exact_l(mk['mmid'].astype(F32), gf)
    att = jnp.where(mk['mi'] & mk['sameblk'],
                    _dot_nt(q * jnp.exp(b - bmid), kf * jnp.exp(bmid - b)), 0.0)
    if Ls > SUB:
        bend = _dot_exact_l(mk['mend'].astype(F32), gf)
        ko = kf * jnp.exp(jnp.minimum(bend - b, 0.0))
        rb = _shr(mk['row'], SUB)
        cb = _shr(mk['col'], SUB)
        for j in range(C // SUB - 1):
            ref = b[SUB * j + SUB - 1:SUB * j + SUB, :]
            qo = q * jnp.exp(jnp.minimum(b - ref, 0.0))
            att = att + jnp.where(mk['same'] & (cb == j) & (rb > j), _dot_nt(qo, ko), 0.0)
    o = o + _dot(att, iv)
    blast = _dot_exact_l(mk['same'].astype(F32), gf)
    ke = kf * jnp.exp(blast - b)
    upd = _dot(iv.T, _expand_seq(ke, n_seq, sid))
    dec = jnp.exp(blast)
    new = []
    for s in range(n_seq):
        new.append(ST[s * LANES:(s + 1) * LANES, :] * dec[s * Ls:s * Ls + 1, :]
                   + upd[:, s * LANES:(s + 1) * LANES])
    return o, new


def _ada_body(c_ref, w_ref, b_ref, o_ref):
    o_ref[...] = _dot(_silu(c_ref[...]), w_ref[...]) + b_ref[...]


def _inproj_body(x_ref, sh_ref, sc_ref, n1_ref, w_ref, p_ref):
    h = _rms(x_ref[...], n1_ref[...]) * (1.0 + sc_ref[...]) + sh_ref[...]
    p_ref[...] = jnp.dot(h.astype(BF16), w_ref[...], preferred_element_type=F32)


def _rec_body(Ls, single_seq,
              p_ref, shinit_ref, wkvin_ref, hgin_ref, mu_ref, w0_ref, a0_ref, kk_ref, ka_ref, rk_ref,
              lnw_ref, lnb_ref, lb_ref, gn_ref, wcomb_ref, wgate_ref,
              y_ref, shout_ref, wkvout_ref, hgout_ref):
    C = CHUNK
    n_seq = C // Ls
    c = pl.program_id(1)

    @pl.when(c == 0)
    def _():
        wkvout_ref[...] = wkvin_ref[...]
        hgout_ref[...] = hgin_ref[...]
        shout_ref[...] = jnp.zeros_like(shout_ref)

    mk = _make_masks(C, Ls)

    p_rw = p_ref[:, :SHIFT_WIDTH]
    prev_last = shout_ref[C - 1:C, :]
    rowv = _iota((C, SHIFT_WIDTH), 0)
    p_prev = jnp.where(rowv == 0, prev_last, pltpu.roll(p_rw, 1, 0))
    if single_seq:
        start = rowv == jnp.where(c == 0, 0, -1)
    else:
        start = (rowv & (Ls - 1)) == 0
    p_prev = jnp.where(start, shinit_ref[...], p_prev)
    shout_ref[...] = p_rw
    x = p_rw + (p_prev - p_rw) * mu_ref[...]

    r = x[:, 0:512]
    k = x[:, 512:1024]
    v = x[:, 1024:1536]
    wa = x[:, 1536:1664]
    gd = x[:, 1664:1792]
    lane = _iota((C, LANES), 1)
    da = _dot(jnp.where(lane < 64, jnp.tanh(wa), wa), wcomb_ref[...])
    w_log = -_softplus(-(w0_ref[...] + da[:, :512])) - 0.5
    lw = -jnp.exp(w_log)
    a = _sigmoid(a0_ref[...] + da[:, 512:])
    g = _dot(_sigmoid(gd), wgate_ref[...])
    kkr = k * kk_ref[...]
    kmod = k * (1.0 + (a - 1.0) * ka_ref[...])
    rkr = r * kmod * rk_ref[...]
    lnw = lnw_ref[...]
    lnb = lnb_ref[...]

    gsum = (_shr(_iota((HEAD_PAIR, HEAD_PAIR), 0), 64) == _shr(_iota((HEAD_PAIR, HEAD_PAIR), 1), 64)).astype(F32)

    for hp in range(RWKV_HEADS // 2):
        sl = slice(hp * HEAD_PAIR, (hp + 1) * HEAD_PAIR)
        kk_s = kkr[:, sl]
        nrm = jnp.sqrt(_dot_exact_r(kk_s * kk_s, gsum))
        kk_s = kk_s / jnp.maximum(nrm, 1e-12)
        o, new = _wkv_pair_chunk(r[:, sl], lw[:, sl], kmod[:, sl], v[:, sl], kk_s, a[:, sl],
                                 wkvout_ref[hp], mk, Ls)
        for s in range(n_seq):
            wkvout_ref[hp, s * LANES:(s + 1) * LANES, :] = new[s]
        mean = _dot_exact_r(o, gsum) * (1.0 / 64.0)
        d = o - mean
        var = _dot_exact_r(d * d, gsum) * (1.0 / 64.0)
        on = d * lax.rsqrt(var + LNX_EPS) * lnw[:, sl] + lnb[:, sl]
        bonus = _dot_exact_r(rkr[:, sl], gsum) * v[:, sl]
        y_ref[:, sl] = ((on + bonus) * g[:, sl]).astype(y_ref.dtype)

    lbp = lb_ref[...]
    m = jnp.maximum(lbp[0:1, :], lbp[1:2, :])
    e0 = jnp.exp(lbp[0:1, :] - m)
    e1 = jnp.exp(lbp[1:2, :] - m)
    lb = e0 / (e0 + e1)
    q = _silu(p_ref[:, 1792:2304])
    f = lb + (1.0 - lb) * _sigmoid(p_ref[:, 2304:2816])
    iv = p_ref[:, 2816:3328]
    og = p_ref[:, 3328:3840]
    for h in range(HGRN_HEADS):
        sl = slice(h * LANES, (h + 1) * LANES)
        o, new = _hgrn_head_chunk(q[:, sl], f[:, sl], iv[:, sl], hgout_ref[h], mk, Ls)
        for s in range(n_seq):
            hgout_ref[h, s * LANES:(s + 1) * LANES, :] = new[s]
        on = _rms(o, gn_ref[...])
        y_ref[:, RWKV_WIDTH + h * LANES:RWKV_WIDTH + (h + 1) * LANES] = (on * _silu(og[:, sl])).astype(y_ref.dtype)


def _out_body(x_ref, ym_ref, gt1_ref, sh2_ref, sc2_ref, gt2_ref, n2_ref, nf_ref,
              wo_ref, wu_ref, wd_ref, o_ref):
    y = jnp.dot(ym_ref[...], wo_ref[...], preferred_element_type=F32)
    x1 = x_ref[...] + gt1_ref[...] * y
    h = (_rms(x1, n2_ref[...]) * (1.0 + sc2_ref[...]) + sh2_ref[...]).astype(BF16)
    acc = jnp.zeros_like(x1)
    FC = 1024
    for j in range(D_FF // FC):
        u = jnp.dot(h, wu_ref[:, j * FC:(j + 1) * FC], preferred_element_type=F32)
        u = jnp.square(jnp.maximum(u, 0.0)).astype(BF16)
        acc = acc + jnp.dot(u, wd_ref[j * FC:(j + 1) * FC, :], preferred_element_type=F32)
    x2 = x1 + gt2_ref[...] * acc
    o_ref[...] = _rms(x2, nf_ref[...])


def _params(sem):
    return pltpu.CompilerParams(dimension_semantics=sem, vmem_limit_bytes=VMEM_LIMIT)


def _full(shape):
    return pl.BlockSpec(shape, lambda *_: (0,) * len(shape))


def _ada(c_all, w_ada, b_ada):
    n = c_all.shape[0]
    TN = 1024
    return pl.pallas_call(
        _ada_body,
        grid=(w_ada.shape[1] // TN,),
        in_specs=[pl.BlockSpec((n, D_MODEL), lambda j: (0, 0)),
                  pl.BlockSpec((D_MODEL, TN), lambda j: (0, j)),
                  pl.BlockSpec((1, TN), lambda j: (0, j))],
        out_specs=pl.BlockSpec((n, TN), lambda j: (0, j)),
        out_shape=jax.ShapeDtypeStruct((n, w_ada.shape[1]), F32),
        compiler_params=_params(("arbitrary",)),
        name="ada",
    )(c_all, w_ada, b_ada)


def _mod_specs(mod, cols, TM, rows_per_mod):
    if mod.ndim == 3:
        return [pl.BlockSpec((None, 1, D_MODEL), lambda i, c=c: (i * TM // rows_per_mod, 0, c)) for c in cols]
    return [pl.BlockSpec((TM, D_MODEL), lambda i, c=c: (i, c)) for c in cols]


def _inproj(x, mod, norm1, w_in, TM, rows_per_mod):
    M = x.shape[0]
    return pl.pallas_call(
        _inproj_body,
        grid=(M // TM,),
        in_specs=[pl.BlockSpec((TM, D_MODEL), lambda i: (i, 0))]
                 + _mod_specs(mod, (0, 1), TM, rows_per_mod)
                 + [_full((1, D_MODEL)), _full((D_MODEL, IN_WIDTH))],
        out_specs=pl.BlockSpec((TM, IN_WIDTH), lambda i: (i, 0)),
        out_shape=jax.ShapeDtypeStruct((M, IN_WIDTH), F32),
        compiler_params=_params(("arbitrary",)),
        name="inproj",
    )(x, mod, mod, norm1, w_in)


def _rec(p, shinit, wkv_in, hg_in, small, wcomb, wgate, Ls, single_seq):
    G, C = shinit.shape[0], CHUNK
    NC = p.shape[0] // (G * C)
    n_seq = C // Ls
    st_spec = pl.BlockSpec((None, 4, n_seq * LANES, LANES), lambda g, c: (g, 0, 0, 0))
    sh_spec = pl.BlockSpec((None, C, SHIFT_WIDTH), lambda g, c: (g, 0, 0))
    return pl.pallas_call(
        functools.partial(_rec_body, Ls, single_seq),
        grid=(G, NC),
        in_specs=[pl.BlockSpec((C, IN_WIDTH), lambda g, c: (g * NC + c, 0)), sh_spec, st_spec, st_spec]
                 + [_full(s.shape) for s in small] + [_full(wcomb.shape), _full(wgate.shape)],
        out_specs=[pl.BlockSpec((C, D_MODEL), lambda g, c: (g * NC + c, 0)), sh_spec, st_spec, st_spec],
        out_shape=[jax.ShapeDtypeStruct((p.shape[0], D_MODEL), BF16),
                   jax.ShapeDtypeStruct(shinit.shape, F32),
                   jax.ShapeDtypeStruct(wkv_in.shape, F32),
                   jax.ShapeDtypeStruct(hg_in.shape, F32)],
        compiler_params=_params(("arbitrary", "arbitrary")),
        name="rec",
    )(p, shinit, wkv_in, hg_in, *small, wcomb, wgate)


def _out(x, ym, mod, norm2, norm_f, w_out, w_up, w_down, TM, rows_per_mod):
    M = x.shape[0]
    row = lambda i: (i, 0)
    return pl.pallas_call(
        _out_body,
        grid=(M // TM,),
        in_specs=[pl.BlockSpec((TM, D_MODEL), row), pl.BlockSpec((TM, D_MODEL), row)]
                 + _mod_specs(mod, (2, 3, 4, 5), TM, rows_per_mod)
                 + [_full((1, D_MODEL)), _full((1, D_MODEL)),
                    _full(w_out.shape), _full(w_up.shape), _full(w_down.shape)],
        out_specs=pl.BlockSpec((TM, D_MODEL), row),
        out_shape=jax.ShapeDtypeStruct((M, D_MODEL), F32),
        compiler_params=_params(("arbitrary",)),
        name="outmlp",
    )(x, ym, mod, mod, mod, mod, norm2, norm_f, w_out, w_up, w_down)


def _wkv_to_pairs(s, G):
    B = s.shape[0]
    s = s.reshape(B, 4, 2, 64, 64)
    z = jnp.zeros((B, 4, 64, 64), s.dtype)
    top = jnp.concatenate([s[:, :, 0], z], axis=-1)
    bot = jnp.concatenate([z, s[:, :, 1]], axis=-1)
    bd = jnp.concatenate([top, bot], axis=-2)
    bd = bd.reshape(G, B // G, 4, 128, 128).transpose(0, 2, 1, 3, 4)
    return bd.reshape(G, 4, (B // G) * 128, 128)


def _wkv_from_pairs(s, B):
    G = s.shape[0]
    s = s.reshape(G, 4, B // G, 2, 64, 2, 64).transpose(0, 2, 1, 3, 4, 5, 6).reshape(B, 4, 2, 64, 2, 64)
    return jnp.stack([s[:, :, 0, :, 0, :], s[:, :, 1, :, 1, :]], axis=2).reshape(B, 8, 64, 64)


def _hgrn_to_t(s, G):
    B = s.shape[0]
    s = jnp.swapaxes(s, -1, -2).reshape(G, B // G, 4, 128, 128).transpose(0, 2, 1, 3, 4)
    return s.reshape(G, 4, (B // G) * 128, 128)


def _hgrn_from_t(s, B):
    G = s.shape[0]
    s = s.reshape(G, 4, B // G, 128, 128).transpose(0, 2, 1, 3, 4).reshape(B, 4, 128, 128)
    return jnp.swapaxes(s, -1, -2)


def kernel(x_prompt, x_sample, c_prompt, c_sample, state_shift, state_wkv, state_hgrn, norm1, norm2, norm_f, w_ada, b_ada, w_in, mu_shift, w0, w_decay_up, a0, w_aaa_up, w_gate_up, k_k, k_a, r_k, lnx_w, lnx_b, hgrn_lb, hgrn_gnorm, w_out, w_up, w_down):
    BP, TP, _ = x_prompt.shape
    BS, TS, _ = x_sample.shape
    l = 0
    row = lambda t: t.reshape(1, -1)

    mod = _ada(jnp.concatenate([c_prompt, c_sample], axis=0), w_ada[l], row(b_ada[l]))
    mod_p = mod[:BP].reshape(BP, 1, 6 * D_MODEL)
    mod_s = jnp.repeat(mod[BP:], TS, axis=0)

    w_in_b = w_in[l].astype(BF16)
    w_out_b = w_out[l].astype(BF16)
    w_up_b = w_up[l].astype(BF16)
    w_down_b = w_down[l].astype(BF16)
    zer = jnp.zeros((64, RWKV_WIDTH), F32)
    wcomb = jnp.concatenate([jnp.concatenate([w_decay_up[l], zer], axis=1),
                             jnp.concatenate([zer, w_aaa_up[l]], axis=1)], axis=0).astype(BF16)
    wgate = w_gate_up[l].astype(BF16)
    small = [row(mu_shift[l]), row(w0[l]), row(a0[l]), row(k_k[l]), row(k_a[l]), row(r_k[l]),
             row(lnx_w[l]), row(lnx_b[l]), hgrn_lb, row(hgrn_gnorm[l])]
    n1, n2, nf = row(norm1[l]), row(norm2[l]), row(norm_f)

    def trunk(x2d, modx, rows_per_mod, shinit, wkv_in, hg_in, Ls, single_seq, TM):
        p = _inproj(x2d, modx, n1, w_in_b, TM, rows_per_mod)
        ym, shout, wkv_o, hg_o = _rec(p, shinit, wkv_in, hg_in, small, wcomb, wgate, Ls, single_seq)
        y = _out(x2d, ym, modx, n2, nf, w_out_b, w_up_b, w_down_b, TM, rows_per_mod)
        return y, shout, wkv_o, hg_o

    zst = jnp.zeros((BP, 4, LANES, LANES), F32)
    yp, shp, wkvp, hgp = trunk(x_prompt.reshape(BP * TP, D_MODEL), mod_p, TP,
                               jnp.zeros((BP, CHUNK, SHIFT_WIDTH), F32), zst, zst, CHUNK, True, 256)
    y_prompt = yp.reshape(BP, TP, D_MODEL)
    shift_p = shp[:, CHUNK - 1][None]
    wkv_p = _wkv_from_pairs(wkvp, BP)[None]
    hgrn_p = _hgrn_from_t(hgp, BP)[None]

    n_seq = CHUNK // TS
    GS = BS // n_seq
    shinit_s = jnp.zeros((GS, n_seq, TS, SHIFT_WIDTH), F32).at[:, :, 0].set(
        state_shift[l].reshape(GS, n_seq, SHIFT_WIDTH)).reshape(GS, CHUNK, SHIFT_WIDTH)
    ys, shs, wkvs, hgs = trunk(x_sample.reshape(BS * TS, D_MODEL), mod_s, 1, shinit_s,
                               _wkv_to_pairs(state_wkv[l], GS), _hgrn_to_t(state_hgrn[l], GS), TS, False, 256)
    y_sample = ys.reshape(BS, TS, D_MODEL)
    shift_s = shs.reshape(GS, n_seq, TS, SHIFT_WIDTH)[:, :, TS - 1].reshape(BS, SHIFT_WIDTH)[None]
    wkv_s = _wkv_from_pairs(wkvs, BS)[None]
    hgrn_s = _hgrn_from_t(hgs, BS)[None]

    return (y_prompt, y_sample, shift_p, wkv_p, hgrn_p, shift_s, wkv_s, hgrn_s)
```

```python
import functools

import jax
import jax.numpy as jnp
from jax import lax
from jax.experimental import pallas as pl
from jax.experimental.pallas import tpu as pltpu

F32 = jnp.float32
BF16 = jnp.bfloat16

D_MODEL = 1024
RWKV_HEADS = 8
RWKV_WIDTH = 512
HGRN_HEADS = 4
HGRN_WIDTH = 512
SHIFT_WIDTH = 1792
IN_WIDTH = 3840
D_FF = 4096
NORM_EPS = 1e-6
LNX_EPS = 64e-5

LANES = 128
HEAD_PAIR = 128
CHUNK = 64
SUB = 16
VMEM_LIMIT = 58 * 1024 * 1024


def _dot(a, b):
    return jnp.dot(a.astype(BF16), b.astype(BF16), preferred_element_type=F32)


def _dot_nt(a, b):
    return lax.dot_general(a.astype(BF16), b.astype(BF16), (((1,), (1,)), ((), ())),
                           preferred_element_type=F32)


def _split3(x):
    hi = x.astype(BF16)
    r1 = x - hi.astype(F32)
    mid = r1.astype(BF16)
    lo = (r1 - mid.astype(F32)).astype(BF16)
    return hi, mid, lo


def _dot_exact_l(m, x):
    mb = m.astype(BF16)
    hi, mid, lo = _split3(x)
    d = lambda part: jnp.dot(mb, part, preferred_element_type=F32)
    return d(hi) + d(mid) + d(lo)


def _dot_exact_r(x, m):
    mb = m.astype(BF16)
    hi, mid, lo = _split3(x)
    d = lambda part: jnp.dot(part, mb, preferred_element_type=F32)
    return d(hi) + d(mid) + d(lo)


def _iota(shape, dim):
    return lax.broadcasted_iota(jnp.int32, shape, dim)


def _shr(x, n):
    return lax.shift_right_logical(x, jnp.int32(n.bit_length() - 1))


def _sigmoid(x):
    return 1.0 / (1.0 + jnp.exp(-x))


def _silu(x):
    return x * _sigmoid(x)


def _softplus(x):
    return jnp.maximum(x, 0.0) + jnp.log(1.0 + jnp.exp(-jnp.abs(x)))


def _rms(x, gain):
    return x * lax.rsqrt(jnp.mean(x * x, axis=-1, keepdims=True) + NORM_EPS) * gain


def _make_masks(C, Ls):
    row = _iota((C, C), 0)
    col = _iota((C, C), 1)
    seq = lambda t: _shr(t, Ls)
    blk = lambda t: _shr(t, SUB)
    same = seq(row) == seq(col)
    mrow = blk(row) * SUB + (SUB // 2 - 1)
    erow = blk(row) * SUB + (SUB - 1)
    R = _iota((2 * C, 2 * C), 0)
    Q = _iota((2 * C, 2 * C), 1)
    tr = R & (C - 1)
    tq = Q & (C - 1)
    same2 = seq(tr) == seq(tq)
    lvl = []
    s = 1
    while s < Ls:
        lvl.append((_shr(R, 2 * s) == _shr(Q, 2 * s)) & ((R & (2 * s - 1)) >= s) & ((Q & (2 * s - 1)) < s))
        s *= 2
    return dict(
        row=row, col=col, same=same,
        mi=same & (col <= row),
        mmid=(seq(mrow) == seq(col)) & (col <= mrow),
        mend=(seq(erow) == seq(col)) & (col <= erow),
        sameblk=blk(row) == blk(col),
        eye2=(R == Q).astype(F32),
        lvl=lvl,
        ms_hi2=same2 & (tq < tr) & (Q >= C),
        mi22=same2 & (tq <= tr),
    )


def _tri_inverse(G, mk):
    lvl = mk['lvl']
    D = mk['eye2'] + jnp.where(lvl[0], G, 0.0)
    for m in lvl[1:]:
        LD = _dot(jnp.where(m, G, 0.0), D)
        yield
        D = D + _dot(D, LD)
        yield
    return D


def _lockstep(streams):
    streams = list(streams)
    out = [None] * len(streams)
    alive = list(range(len(streams)))
    while alive:
        for i in list(alive):
            try:
                next(streams[i])
            except StopIteration as stop:
                out[i] = stop.value
                alive.remove(i)
    return out


def _select_seq(full, n_seq, sid_rows):
    if n_seq == 1:
        return full
    acc = jnp.where(sid_rows == 0, full[:, 0:LANES], 0.0)
    for s in range(1, n_seq):
        acc = acc + jnp.where(sid_rows == s, full[:, s * LANES:(s + 1) * LANES], 0.0)
    return acc


def _expand_seq(x, n_seq, sid_rows):
    if n_seq == 1:
        return x
    return jnp.concatenate([jnp.where(sid_rows == s, x, 0.0) for s in range(n_seq)], axis=1)


def _wkv_pair_chunk(r, lw, k, v, kk, a, S, mk, Ls):
    C = r.shape[0]
    n_seq = C // Ls
    lo_half = _iota((C, HEAD_PAIR), 1) < 64
    own = (_iota((2 * C, HEAD_PAIR), 1) < 64) == (_iota((2 * C, HEAD_PAIR), 0) < C)
    stack = lambda t: jnp.where(own, jnp.concatenate([t, t], axis=0), 0.0)
    cw = _dot_exact_l(mk['mi'].astype(F32), lw)
    wl = jnp.exp(_dot_exact_l(mk['same'].astype(F32), lw))
    yield
    e_neg = jnp.exp(-cw)
    at = -kk * jnp.exp(cw - lw)
    bt = kk * a * e_neg
    kt = k * e_neg
    rt = r * jnp.exp(cw)
    Y = jnp.concatenate([bt, kt], axis=0)
    at2 = stack(at)
    G = _dot_nt(at2, jnp.concatenate([bt, bt], axis=0))
    ga = _dot_nt(at2, Y)
    gr = _dot_nt(stack(rt), Y)
    sid2 = _shr(_iota((2 * C, LANES), 0) & (C - 1), Ls)
    XO = _select_seq(_dot_nt(jnp.concatenate([at, rt], axis=0), S), n_seq, sid2)
    yield
    Vz = jnp.concatenate([jnp.zeros_like(v), v], axis=0)
    xk = _dot(jnp.where(mk['ms_hi2'], ga, 0.0), Vz)
    D = yield from _tri_inverse(G, mk)
    X = XO[:C] + jnp.where(lo_half, xk[:C], xk[C:])
    DX = _dot(D, jnp.concatenate([X, X], axis=0))
    yield
    U = jnp.where(lo_half, DX[:C], DX[C:])
    Z = jnp.concatenate([U, v], axis=0)
    oz = _dot(jnp.where(mk['mi22'], gr, 0.0), Z)
    upd = _dot(Z.T, _expand_seq(Y, n_seq, sid2))
    yield
    o = XO[C:] + jnp.where(lo_half, oz[:C], oz[C:])
    bd = _shr(_iota((HEAD_PAIR, HEAD_PAIR), 0), 64) == _shr(_iota((HEAD_PAIR, HEAD_PAIR), 1), 64)
    new = []
    for s in range(n_seq):
        Ss = S[s * LANES:(s + 1) * LANES, :]
        new.append(jnp.where(bd, (Ss + upd[:, s * LANES:(s + 1) * LANES]) * wl[s * Ls:s * Ls + 1, :], 0.0))
    return o, new


def _hgrn_head_chunk(q, f, iv, ST, mk, Ls):
    C = q.shape[0]
    n_seq = C // Ls
    sid = _shr(_iota((C, LANES), 0), Ls)
    gf = jnp.log(f)
    kf = 1.0 - f
    b = _dot_exact_l(mk['mi'].astype(F32), gf)
    bmid = _dot_exact_l(mk['mmid'].astype(F32), gf)
    blast = _dot_exact_l(mk['same'].astype(F32), gf)
    if Ls > SUB:
        bend = _dot_exact_l(mk['mend'].astype(F32), gf)
    yield
    o = _select_seq(_dot_nt(q * jnp.exp(b), ST), n_seq, sid)
    att = jnp.where(mk['mi'] & mk['sameblk'],
                    _dot_nt(q * jnp.exp(b - bmid), kf * jnp.exp(bmid - b)), 0.0)
    if Ls > SUB:
        ko = kf * jnp.exp(jnp.minimum(bend - b, 0.0))
        rb = _shr(mk['row'], SUB)
        cb = _shr(mk['col'], SUB)
        for j in range(C // SUB - 1):
            ref = b[SUB * j + SUB - 1:SUB * j + SUB, :]
            qo = q * jnp.exp(jnp.minimum(b - ref, 0.0))
            att = att + jnp.where(mk['same'] & (cb == j) & (rb > j), _dot_nt(qo, ko), 0.0)
    ke = kf * jnp.exp(blast - b)
    upd = _dot(iv.T, _expand_seq(ke, n_seq, sid))
    yield
    o = o + _dot(att, iv)
    yield
    dec = jnp.exp(blast)
    new = []
    for s in range(n_seq):
        new.append(ST[s * LANES:(s + 1) * LANES, :] * dec[s * Ls:s * Ls + 1, :]
                   + upd[:, s * LANES:(s + 1) * LANES])
    return o, new


def _ada_body(c_ref, w_ref, b_ref, o_ref):
    o_ref[...] = _dot(_silu(c_ref[...]), w_ref[...]) + b_ref[...]


def _inproj_body(x_ref, sh_ref, sc_ref, n1_ref, w_ref, p_ref):
    h = _rms(x_ref[...], n1_ref[...]) * (1.0 + sc_ref[...]) + sh_ref[...]
    p_ref[...] = jnp.dot(h.astype(BF16), w_ref[...], preferred_element_type=F32)


def _rec_body(Ls, single_seq,
              p_ref, shinit_ref, wkvin_ref, hgin_ref, mu_ref, w0_ref, a0_ref, kk_ref, ka_ref, rk_ref,
              lnw_ref, lnb_ref, lb_ref, gn_ref, wcomb_ref, wgate_ref,
              y_ref, shout_ref, wkvout_ref, hgout_ref):
    C = CHUNK
    n_seq = C // Ls
    c = pl.program_id(1)

    @pl.when(c == 0)
    def _():
        wkvout_ref[...] = wkvin_ref[...]
        hgout_ref[...] = hgin_ref[...]
        shout_ref[...] = jnp.zeros_like(shout_ref)

    mk = _make_masks(C, Ls)

    p_rw = p_ref[:, :SHIFT_WIDTH]
    prev_last = shout_ref[C - 1:C, :]
    rowv = _iota((C, SHIFT_WIDTH), 0)
    p_prev = jnp.where(rowv == 0, prev_last, pltpu.roll(p_rw, 1, 0))
    if single_seq:
        start = rowv == jnp.where(c == 0, 0, -1)
    else:
        start = (rowv & (Ls - 1)) == 0
    p_prev = jnp.where(start, shinit_ref[...], p_prev)
    shout_ref[...] = p_rw
    x = p_rw + (p_prev - p_rw) * mu_ref[...]

    r = x[:, 0:512]
    k = x[:, 512:1024]
    v = x[:, 1024:1536]
    wa = x[:, 1536:1664]
    gd = x[:, 1664:1792]
    lane = _iota((C, LANES), 1)
    da = _dot(jnp.where(lane < 64, jnp.tanh(wa), wa), wcomb_ref[...])
    w_log = -_softplus(-(w0_ref[...] + da[:, :512])) - 0.5
    lw = -jnp.exp(w_log)
    a = _sigmoid(a0_ref[...] + da[:, 512:])
    g = _dot(_sigmoid(gd), wgate_ref[...])
    kkr = k * kk_ref[...]
    kmod = k * (1.0 + (a - 1.0) * ka_ref[...])
    rkr = r * kmod * rk_ref[...]
    lnw = lnw_ref[...]
    lnb = lnb_ref[...]

    gsum = (_shr(_iota((HEAD_PAIR, HEAD_PAIR), 0), 64) == _shr(_iota((HEAD_PAIR, HEAD_PAIR), 1), 64)).astype(F32)

    def wkv_stream(hp):
        sl = slice(hp * HEAD_PAIR, (hp + 1) * HEAD_PAIR)
        kk_s = kkr[:, sl]
        nrm = jnp.sqrt(_dot_exact_r(kk_s * kk_s, gsum))
        bonus = _dot_exact_r(rkr[:, sl], gsum) * v[:, sl]
        yield
        kk_s = kk_s / jnp.maximum(nrm, 1e-12)
        o, new = yield from _wkv_pair_chunk(r[:, sl], lw[:, sl], kmod[:, sl], v[:, sl], kk_s, a[:, sl],
                                            wkvout_ref[hp], mk, Ls)
        for s in range(n_seq):
            wkvout_ref[hp, s * LANES:(s + 1) * LANES, :] = new[s]
        mean = _dot_exact_r(o, gsum) * (1.0 / 64.0)
        yield
        d = o - mean
        var = _dot_exact_r(d * d, gsum) * (1.0 / 64.0)
        yield
        on = d * lax.rsqrt(var + LNX_EPS) * lnw[:, sl] + lnb[:, sl]
        y_ref[:, sl] = ((on + bonus) * g[:, sl]).astype(y_ref.dtype)

    lbp = lb_ref[...]
    m = jnp.maximum(lbp[0:1, :], lbp[1:2, :])
    e0 = jnp.exp(lbp[0:1, :] - m)
    e1 = jnp.exp(lbp[1:2, :] - m)
    lb = e0 / (e0 + e1)
    q = _silu(p_ref[:, 1792:2304])
    f = lb + (1.0 - lb) * _sigmoid(p_ref[:, 2304:2816])
    iv = p_ref[:, 2816:3328]
    og = p_ref[:, 3328:3840]

    def hgrn_stream(h):
        sl = slice(h * LANES, (h + 1) * LANES)
        o, new = yield from _hgrn_head_chunk(q[:, sl], f[:, sl], iv[:, sl], hgout_ref[h], mk, Ls)
        for s in range(n_seq):
            hgout_ref[h, s * LANES:(s + 1) * LANES, :] = new[s]
        on = _rms(o, gn_ref[...])
        y_ref[:, RWKV_WIDTH + h * LANES:RWKV_WIDTH + (h + 1) * LANES] = (on * _silu(og[:, sl])).astype(y_ref.dtype)

    _lockstep([wkv_stream(hp) for hp in range(RWKV_HEADS // 2)]
              + [hgrn_stream(h) for h in range(HGRN_HEADS)])


def _out_body(x_ref, ym_ref, gt1_ref, sh2_ref, sc2_ref, gt2_ref, n2_ref, nf_ref,
              wo_ref, wu_ref, wd_ref, o_ref):
    y = jnp.dot(ym_ref[...], wo_ref[...], preferred_element_type=F32)
    x1 = x_ref[...] + gt1_ref[...] * y
    h = (_rms(x1, n2_ref[...]) * (1.0 + sc2_ref[...]) + sh2_ref[...]).astype(BF16)
    acc = jnp.zeros_like(x1)
    FC = 1024
    for j in range(D_FF // FC):
        u = jnp.dot(h, wu_ref[:, j * FC:(j + 1) * FC], preferred_element_type=F32)
        u = jnp.square(jnp.maximum(u, 0.0)).astype(BF16)
        acc = acc + jnp.dot(u, wd_ref[j * FC:(j + 1) * FC, :], preferred_element_type=F32)
    x2 = x1 + gt2_ref[...] * acc
    o_ref[...] = _rms(x2, nf_ref[...])


def _params(sem):
    return pltpu.CompilerParams(dimension_semantics=sem, vmem_limit_bytes=VMEM_LIMIT)


def _full(shape):
    return pl.BlockSpec(shape, lambda *_: (0,) * len(shape))


def _ada(c_all, w_ada, b_ada):
    n = c_all.shape[0]
    TN = 1024
    return pl.pallas_call(
        _ada_body,
        grid=(w_ada.shape[1] // TN,),
        in_specs=[pl.BlockSpec((n, D_MODEL), lambda j: (0, 0)),
                  pl.BlockSpec((D_MODEL, TN), lambda j: (0, j)),
                  pl.BlockSpec((1, TN), lambda j: (0, j))],
        out_specs=pl.BlockSpec((n, TN), lambda j: (0, j)),
        out_shape=jax.ShapeDtypeStruct((n, w_ada.shape[1]), F32),
        compiler_params=_params(("arbitrary",)),
        name="ada",
    )(c_all, w_ada, b_ada)


def _mod_specs(mod, cols, TM, rows_per_mod):
    if mod.ndim == 3:
        return [pl.BlockSpec((None, 1, D_MODEL), lambda i, c=c: (i * TM // rows_per_mod, 0, c)) for c in cols]
    return [pl.BlockSpec((TM, D_MODEL), lambda i, c=c: (i, c)) for c in cols]


def _inproj(x, mod, norm1, w_in, TM, rows_per_mod):
    M = x.shape[0]
    return pl.pallas_call(
        _inproj_body,
        grid=(M // TM,),
        in_specs=[pl.BlockSpec((TM, D_MODEL), lambda i: (i, 0))]
                 + _mod_specs(mod, (0, 1), TM, rows_per_mod)
                 + [_full((1, D_MODEL)), _full((D_MODEL, IN_WIDTH))],
        out_specs=pl.BlockSpec((TM, IN_WIDTH), lambda i: (i, 0)),
        out_shape=jax.ShapeDtypeStruct((M, IN_WIDTH), F32),
        compiler_params=_params(("arbitrary",)),
        name="inproj",
    )(x, mod, mod, norm1, w_in)


def _rec(p, shinit, wkv_in, hg_in, small, wcomb, wgate, Ls, single_seq):
    G, C = shinit.shape[0], CHUNK
    NC = p.shape[0] // (G * C)
    n_seq = C // Ls
    st_spec = pl.BlockSpec((None, 4, n_seq * LANES, LANES), lambda g, c: (g, 0, 0, 0))
    sh_spec = pl.BlockSpec((None, C, SHIFT_WIDTH), lambda g, c: (g, 0, 0))
    return pl.pallas_call(
        functools.partial(_rec_body, Ls, single_seq),
        grid=(G, NC),
        in_specs=[pl.BlockSpec((C, IN_WIDTH), lambda g, c: (g * NC + c, 0)), sh_spec, st_spec, st_spec]
                 + [_full(s.shape) for s in small] + [_full(wcomb.shape), _full(wgate.shape)],
        out_specs=[pl.BlockSpec((C, D_MODEL), lambda g, c: (g * NC + c, 0)), sh_spec, st_spec, st_spec],
        out_shape=[jax.ShapeDtypeStruct((p.shape[0], D_MODEL), BF16),
                   jax.ShapeDtypeStruct(shinit.shape, F32),
                   jax.ShapeDtypeStruct(wkv_in.shape, F32),
                   jax.ShapeDtypeStruct(hg_in.shape, F32)],
        compiler_params=_params(("arbitrary", "arbitrary")),
        name="rec",
    )(p, shinit, wkv_in, hg_in, *small, wcomb, wgate)


def _out(x, ym, mod, norm2, norm_f, w_out, w_up, w_down, TM, rows_per_mod):
    M = x.shape[0]
    row = lambda i: (i, 0)
    return pl.pallas_call(
        _out_body,
        grid=(M // TM,),
        in_specs=[pl.BlockSpec((TM, D_MODEL), row), pl.BlockSpec((TM, D_MODEL), row)]
                 + _mod_specs(mod, (2, 3, 4, 5), TM, rows_per_mod)
                 + [_full((1, D_MODEL)), _full((1, D_MODEL)),
                    _full(w_out.shape), _full(w_up.shape), _full(w_down.shape)],
        out_specs=pl.BlockSpec((TM, D_MODEL), row),
        out_shape=jax.ShapeDtypeStruct((M, D_MODEL), F32),
        compiler_params=_params(("arbitrary",)),
        name="outmlp",
    )(x, ym, mod, mod, mod, mod, norm2, norm_f, w_out, w_up, w_down)


def _wkv_to_pairs(s, G):
    B = s.shape[0]
    s = s.reshape(B, 4, 2, 64, 64)
    z = jnp.zeros((B, 4, 64, 64), s.dtype)
    top = jnp.concatenate([s[:, :, 0], z], axis=-1)
    bot = jnp.concatenate([z, s[:, :, 1]], axis=-1)
    bd = jnp.concatenate([top, bot], axis=-2)
    bd = bd.reshape(G, B // G, 4, 128, 128).transpose(0, 2, 1, 3, 4)
    return bd.reshape(G, 4, (B // G) * 128, 128)


def _wkv_from_pairs(s, B):
    G = s.shape[0]
    s = s.reshape(G, 4, B // G, 2, 64, 2, 64).transpose(0, 2, 1, 3, 4, 5, 6).reshape(B, 4, 2, 64, 2, 64)
    return jnp.stack([s[:, :, 0, :, 0, :], s[:, :, 1, :, 1, :]], axis=2).reshape(B, 8, 64, 64)


def _hgrn_to_t(s, G):
    B = s.shape[0]
    s = jnp.swapaxes(s, -1, -2).reshape(G, B // G, 4, 128, 128).transpose(0, 2, 1, 3, 4)
    return s.reshape(G, 4, (B // G) * 128, 128)


def _hgrn_from_t(s, B):
    G = s.shape[0]
    s = s.reshape(G, 4, B // G, 128, 128).transpose(0, 2, 1, 3, 4).reshape(B, 4, 128, 128)
    return jnp.swapaxes(s, -1, -2)


def kernel(x_prompt, x_sample, c_prompt, c_sample, state_shift, state_wkv, state_hgrn, norm1, norm2, norm_f, w_ada, b_ada, w_in, mu_shift, w0, w_decay_up, a0, w_aaa_up, w_gate_up, k_k, k_a, r_k, lnx_w, lnx_b, hgrn_lb, hgrn_gnorm, w_out, w_up, w_down):
    BP, TP, _ = x_prompt.shape
    BS, TS, _ = x_sample.shape
    l = 0
    row = lambda t: t.reshape(1, -1)

    mod = _ada(jnp.concatenate([c_prompt, c_sample], axis=0), w_ada[l], row(b_ada[l]))
    mod_p = mod[:BP].reshape(BP, 1, 6 * D_MODEL)
    mod_s = jnp.repeat(mod[BP:], TS, axis=0)

    w_in_b = w_in[l].astype(BF16)
    w_out_b = w_out[l].astype(BF16)
    w_up_b = w_up[l].astype(BF16)
    w_down_b = w_down[l].astype(BF16)
    zer = jnp.zeros((64, RWKV_WIDTH), F32)
    wcomb = jnp.concatenate([jnp.concatenate([w_decay_up[l], zer], axis=1),
                             jnp.concatenate([zer, w_aaa_up[l]], axis=1)], axis=0).astype(BF16)
    wgate = w_gate_up[l].astype(BF16)
    small = [row(mu_shift[l]), row(w0[l]), row(a0[l]), row(k_k[l]), row(k_a[l]), row(r_k[l]),
             row(lnx_w[l]), row(lnx_b[l]), hgrn_lb, row(hgrn_gnorm[l])]
    n1, n2, nf = row(norm1[l]), row(norm2[l]), row(norm_f)

    def trunk(x2d, modx, rows_per_mod, shinit, wkv_in, hg_in, Ls, single_seq, TM):
        p = _inproj(x2d, modx, n1, w_in_b, TM, rows_per_mod)
        ym, shout, wkv_o, hg_o = _rec(p, shinit, wkv_in, hg_in, small, wcomb, wgate, Ls, single_seq)
        y = _out(x2d, ym, modx, n2, nf, w_out_b, w_up_b, w_down_b, TM, rows_per_mod)
        return y, shout, wkv_o, hg_o

    zst = jnp.zeros((BP, 4, LANES, LANES), F32)
    yp, shp, wkvp, hgp = trunk(x_prompt.reshape(BP * TP, D_MODEL), mod_p, TP,
                               jnp.zeros((BP, CHUNK, SHIFT_WIDTH), F32), zst, zst, CHUNK, True, 256)
    y_prompt = yp.reshape(BP, TP, D_MODEL)
    shift_p = shp[:, CHUNK - 1][None]
    wkv_p = _wkv_from_pairs(wkvp, BP)[None]
    hgrn_p = _hgrn_from_t(hgp, BP)[None]

    n_seq = CHUNK // TS
    GS = BS // n_seq
    shinit_s = jnp.zeros((GS, n_seq, TS, SHIFT_WIDTH), F32).at[:, :, 0].set(
        state_shift[l].reshape(GS, n_seq, SHIFT_WIDTH)).reshape(GS, CHUNK, SHIFT_WIDTH)
    ys, shs, wkvs, hgs = trunk(x_sample.reshape(BS * TS, D_MODEL), mod_s, 1, shinit_s,
                               _wkv_to_pairs(state_wkv[l], GS), _hgrn_to_t(state_hgrn[l], GS), TS, False, 256)
    y_sample = ys.reshape(BS, TS, D_MODEL)
    shift_s = shs.reshape(GS, n_seq, TS, SHIFT_WIDTH)[:, :, TS - 1].reshape(BS, SHIFT_WIDTH)[None]
    wkv_s = _wkv_from_pairs(wkvs, BS)[None]
    hgrn_s = _hgrn_from_t(hgs, BS)[None]

    return (y_prompt, y_sample, shift_p, wkv_p, hgrn_p, shift_s, wkv_s, hgrn_s)
```

```python
import functools

import jax
import jax.numpy as jnp
from jax import lax
from jax.experimental import pallas as pl
from jax.experimental.pallas import tpu as pltpu

F32 = jnp.float32
BF16 = jnp.bfloat16

D_MODEL = 1024
RWKV_HEADS = 8
RWKV_WIDTH = 512
HGRN_HEADS = 4
HGRN_WIDTH = 512
SHIFT_WIDTH = 1792
IN_WIDTH = 3840
D_FF = 4096
NORM_EPS = 1e-6
LNX_EPS = 64e-5

LANES = 128
HEAD_PAIR = 128
CHUNK = 64
SUB = 16
VMEM_LIMIT = 58 * 1024 * 1024


def _dot(a, b):
    return jnp.dot(a.astype(BF16), b.astype(BF16), preferred_element_type=F32)


def _dot_nt(a, b):
    return lax.dot_general(a.astype(BF16), b.astype(BF16), (((1,), (1,)), ((), ())),
                           preferred_element_type=F32)


def _split3(x):
    hi = x.astype(BF16)
    r1 = x - hi.astype(F32)
    mid = r1.astype(BF16)
    lo = (r1 - mid.astype(F32)).astype(BF16)
    return hi, mid, lo


def _dot_exact_l(m, x):
    mb = m.astype(BF16)
    hi, mid, lo = _split3(x)
    d = lambda part: jnp.dot(mb, part, preferred_element_type=F32)
    return d(hi) + d(mid) + d(lo)


def _dot_exact_r(x, m):
    mb = m.astype(BF16)
    hi, mid, lo = _split3(x)
    d = lambda part: jnp.dot(part, mb, preferred_element_type=F32)
    return d(hi) + d(mid) + d(lo)


def _iota(shape, dim):
    return lax.broadcasted_iota(jnp.int32, shape, dim)


def _shr(x, n):
    return lax.shift_right_logical(x, jnp.int32(n.bit_length() - 1))


def _sigmoid(x):
    return 1.0 / (1.0 + jnp.exp(-x))


def _silu(x):
    return x * _sigmoid(x)


def _softplus(x):
    return jnp.maximum(x, 0.0) + jnp.log(1.0 + jnp.exp(-jnp.abs(x)))


def _rms(x, gain):
    return x * lax.rsqrt(jnp.mean(x * x, axis=-1, keepdims=True) + NORM_EPS) * gain


def _make_masks(C, Ls):
    row = _iota((C, C), 0)
    col = _iota((C, C), 1)
    seq = lambda t: _shr(t, Ls)
    blk = lambda t: _shr(t, SUB)
    same = seq(row) == seq(col)
    mrow = blk(row) * SUB + (SUB // 2 - 1)
    erow = blk(row) * SUB + (SUB - 1)
    R = _iota((2 * C, 2 * C), 0)
    Q = _iota((2 * C, 2 * C), 1)
    tr = R & (C - 1)
    tq = Q & (C - 1)
    same2 = seq(tr) == seq(tq)
    lvl = []
    s = 1
    while s < Ls:
        lvl.append((_shr(R, 2 * s) == _shr(Q, 2 * s)) & ((R & (2 * s - 1)) >= s) & ((Q & (2 * s - 1)) < s))
        s *= 2
    return dict(
        row=row, col=col, same=same,
        mi=same & (col <= row),
        mmid=(seq(mrow) == seq(col)) & (col <= mrow),
        mend=(seq(erow) == seq(col)) & (col <= erow),
        sameblk=blk(row) == blk(col),
        eye2=(R == Q).astype(F32),
        lvl=lvl,
        ms_hi2=same2 & (tq < tr) & (Q >= C),
        mi22=same2 & (tq <= tr),
    )


def _tri_inverse(G, mk):
    lvl = mk['lvl']
    D = mk['eye2'] + jnp.where(lvl[0], G, 0.0)
    for m in lvl[1:]:
        LD = _dot(jnp.where(m, G, 0.0), D)
        yield
        D = D + _dot(D, LD)
        yield
    return D


def _lockstep(streams):
    streams = list(streams)
    out = [None] * len(streams)
    alive = list(range(len(streams)))
    while alive:
        for i in list(alive):
            try:
                next(streams[i])
            except StopIteration as stop:
                out[i] = stop.value
                alive.remove(i)
    return out


def _select_seq(full, n_seq, sid_rows):
    if n_seq == 1:
        return full
    acc = jnp.where(sid_rows == 0, full[:, 0:LANES], 0.0)
    for s in range(1, n_seq):
        acc = acc + jnp.where(sid_rows == s, full[:, s * LANES:(s + 1) * LANES], 0.0)
    return acc


def _expand_seq(x, n_seq, sid_rows):
    if n_seq == 1:
        return x
    return jnp.concatenate([jnp.where(sid_rows == s, x, 0.0) for s in range(n_seq)], axis=1)


def _wkv_pair_chunk(r, lw, k, v, kk, a, S, mk, Ls):
    C = r.shape[0]
    n_seq = C // Ls
    lo_half = _iota((C, HEAD_PAIR), 1) < 64
    own = (_iota((2 * C, HEAD_PAIR), 1) < 64) == (_iota((2 * C, HEAD_PAIR), 0) < C)
    stack = lambda t: jnp.where(own, jnp.concatenate([t, t], axis=0), 0.0)
    cw = _dot_exact_l(mk['mi'].astype(F32), lw)
    wl = jnp.exp(_dot_exact_l(mk['same'].astype(F32), lw))
    yield
    e_neg = jnp.exp(-cw)
    at = -kk * jnp.exp(cw - lw)
    bt = kk * a * e_neg
    kt = k * e_neg
    rt = r * jnp.exp(cw)
    Y = jnp.concatenate([bt, kt], axis=0)
    at2 = stack(at)
    G = _dot_nt(at2, jnp.concatenate([bt, bt], axis=0))
    ga = _dot_nt(at2, Y)
    gr = _dot_nt(stack(rt), Y)
    sid2 = _shr(_iota((2 * C, LANES), 0) & (C - 1), Ls)
    XO = _select_seq(_dot_nt(jnp.concatenate([at, rt], axis=0), S), n_seq, sid2)
    yield
    Vz = jnp.concatenate([jnp.zeros_like(v), v], axis=0)
    xk = _dot(jnp.where(mk['ms_hi2'], ga, 0.0), Vz)
    D = yield from _tri_inverse(G, mk)
    X = XO[:C] + jnp.where(lo_half, xk[:C], xk[C:])
    DX = _dot(D, jnp.concatenate([X, X], axis=0))
    yield
    U = jnp.where(lo_half, DX[:C], DX[C:])
    Z = jnp.concatenate([U, v], axis=0)
    oz = _dot(jnp.where(mk['mi22'], gr, 0.0), Z)
    upd = _dot(Z.T, _expand_seq(Y, n_seq, sid2))
    yield
    o = XO[C:] + jnp.where(lo_half, oz[:C], oz[C:])
    bd = _shr(_iota((HEAD_PAIR, HEAD_PAIR), 0), 64) == _shr(_iota((HEAD_PAIR, HEAD_PAIR), 1), 64)
    new = []
    for s in range(n_seq):
        Ss = S[s * LANES:(s + 1) * LANES, :]
        new.append(jnp.where(bd, (Ss + upd[:, s * LANES:(s + 1) * LANES]) * wl[s * Ls:s * Ls + 1, :], 0.0))
    return o, new


def _hgrn_head_chunk(q, f, iv, ST, mk, Ls):
    C = q.shape[0]
    n_seq = C // Ls
    sid = _shr(_iota((C, LANES), 0), Ls)
    gf = jnp.log(f)
    kf = 1.0 - f
    b = _dot_exact_l(mk['mi'].astype(F32), gf)
    bmid = _dot_exact_l(mk['mmid'].astype(F32), gf)
    blast = _dot_exact_l(mk['same'].astype(F32), gf)
    if Ls > SUB:
        bend = _dot_exact_l(mk['mend'].astype(F32), gf)
    yield
    o = _select_seq(_dot_nt(q * jnp.exp(b), ST), n_seq, sid)
    att = jnp.where(mk['mi'] & mk['sameblk'],
                    _dot_nt(q * jnp.exp(b - bmid), kf * jnp.exp(bmid - b)), 0.0)
    if Ls > SUB:
        ko = kf * jnp.exp(jnp.minimum(bend - b, 0.0))
        rb = _shr(mk['row'], SUB)
        cb = _shr(mk['col'], SUB)
        for j in range(C // SUB - 1):
            ref = b[SUB * j + SUB - 1:SUB * j + SUB, :]
            qo = q * jnp.exp(jnp.minimum(b - ref, 0.0))
            att = att + jnp.where(mk['same'] & (cb == j) & (rb > j), _dot_nt(qo, ko), 0.0)
    ke = kf * jnp.exp(blast - b)
    upd = _dot(iv.T, _expand_seq(ke, n_seq, sid))
    yield
    o = o + _dot(att, iv)
    yield
    dec = jnp.exp(blast)
    new = []
    for s in range(n_seq):
        new.append(ST[s * LANES:(s + 1) * LANES, :] * dec[s * Ls:s * Ls + 1, :]
                   + upd[:, s * LANES:(s + 1) * LANES])
    return o, new


def _ada_body(c_ref, w_ref, b_ref, o_ref):
    o_ref[...] = _dot(_silu(c_ref[...]), w_ref[...]) + b_ref[...]


def _inproj_body(x_ref, sh_ref, sc_ref, n1_ref, w_ref, p_ref):
    h = _rms(x_ref[...], n1_ref[...]) * (1.0 + sc_ref[...]) + sh_ref[...]
    p_ref[...] = jnp.dot(h.astype(BF16), w_ref[...], preferred_element_type=F32)


def _rec_body(Ls, single_seq, *refs):
    if single_seq:
        p_ref, shinit_ref = refs[:2]
        wkvin_ref = hgin_ref = None
        rest = refs[2:]
    else:
        p_ref, shinit_ref, wkvin_ref, hgin_ref = refs[:4]
        rest = refs[4:]
    (mu_ref, w0_ref, a0_ref, kk_ref, ka_ref, rk_ref, lnw_ref, lnb_ref, lb_ref, gn_ref, wcomb_ref, wgate_ref,
     y_ref, shout_ref, wkvout_ref, hgout_ref, wkv_sc, hg_sc) = rest
    C = CHUNK
    n_seq = C // Ls
    c = pl.program_id(1)

    @pl.when(c == 0)
    def _():
        shout_ref[...] = jnp.zeros_like(shout_ref)
        wkv_sc[...] = jnp.zeros_like(wkv_sc)
        if single_seq:
            hg_sc[...] = jnp.zeros_like(hg_sc)
        else:
            for s in range(n_seq):
                for hp in range(RWKV_HEADS // 2):
                    wkv_sc[hp, s * LANES:s * LANES + 64, 0:64] = wkvin_ref[s, 2 * hp]
                    wkv_sc[hp, s * LANES + 64:(s + 1) * LANES, 64:128] = wkvin_ref[s, 2 * hp + 1]
                for h in range(HGRN_HEADS):
                    hg_sc[h, s * LANES:(s + 1) * LANES, :] = hgin_ref[s, h].T

    mk = _make_masks(C, Ls)

    p_rw = p_ref[:, :SHIFT_WIDTH]
    prev_last = shout_ref[C - 1:C, :]
    rowv = _iota((C, SHIFT_WIDTH), 0)
    p_prev = jnp.where(rowv == 0, prev_last, pltpu.roll(p_rw, 1, 0))
    if single_seq:
        start = rowv == jnp.where(c == 0, 0, -1)
    else:
        start = (rowv & (Ls - 1)) == 0
    p_prev = jnp.where(start, shinit_ref[...], p_prev)
    shout_ref[...] = p_rw
    x = p_rw + (p_prev - p_rw) * mu_ref[...]

    r = x[:, 0:512]
    k = x[:, 512:1024]
    v = x[:, 1024:1536]
    wa = x[:, 1536:1664]
    gd = x[:, 1664:1792]
    lane = _iota((C, LANES), 1)
    da = _dot(jnp.where(lane < 64, jnp.tanh(wa), wa), wcomb_ref[...])
    w_log = -_softplus(-(w0_ref[...] + da[:, :512])) - 0.5
    lw = -jnp.exp(w_log)
    a = _sigmoid(a0_ref[...] + da[:, 512:])
    g = _dot(_sigmoid(gd), wgate_ref[...])
    kkr = k * kk_ref[...]
    kmod = k * (1.0 + (a - 1.0) * ka_ref[...])
    rkr = r * kmod * rk_ref[...]
    lnw = lnw_ref[...]
    lnb = lnb_ref[...]

    gsum = (_shr(_iota((HEAD_PAIR, HEAD_PAIR), 0), 64) == _shr(_iota((HEAD_PAIR, HEAD_PAIR), 1), 64)).astype(F32)

    def wkv_stream(hp):
        sl = slice(hp * HEAD_PAIR, (hp + 1) * HEAD_PAIR)
        kk_s = kkr[:, sl]
        nrm = jnp.sqrt(_dot_exact_r(kk_s * kk_s, gsum))
        bonus = _dot_exact_r(rkr[:, sl], gsum) * v[:, sl]
        yield
        kk_s = kk_s / jnp.maximum(nrm, 1e-12)
        o, new = yield from _wkv_pair_chunk(r[:, sl], lw[:, sl], kmod[:, sl], v[:, sl], kk_s, a[:, sl],
                                            wkv_sc[hp], mk, Ls)
        for s in range(n_seq):
            wkv_sc[hp, s * LANES:(s + 1) * LANES, :] = new[s]
        mean = _dot_exact_r(o, gsum) * (1.0 / 64.0)
        yield
        d = o - mean
        var = _dot_exact_r(d * d, gsum) * (1.0 / 64.0)
        yield
        on = d * lax.rsqrt(var + LNX_EPS) * lnw[:, sl] + lnb[:, sl]
        y_ref[:, sl] = ((on + bonus) * g[:, sl]).astype(y_ref.dtype)

    lbp = lb_ref[...]
    m = jnp.maximum(lbp[0:1, :], lbp[1:2, :])
    e0 = jnp.exp(lbp[0:1, :] - m)
    e1 = jnp.exp(lbp[1:2, :] - m)
    lb = e0 / (e0 + e1)
    q = _silu(p_ref[:, 1792:2304])
    f = lb + (1.0 - lb) * _sigmoid(p_ref[:, 2304:2816])
    iv = p_ref[:, 2816:3328]
    og = p_ref[:, 3328:3840]

    def hgrn_stream(h):
        sl = slice(h * LANES, (h + 1) * LANES)
        o, new = yield from _hgrn_head_chunk(q[:, sl], f[:, sl], iv[:, sl], hg_sc[h], mk, Ls)
        for s in range(n_seq):
            hg_sc[h, s * LANES:(s + 1) * LANES, :] = new[s]
        on = _rms(o, gn_ref[...])
        y_ref[:, RWKV_WIDTH + h * LANES:RWKV_WIDTH + (h + 1) * LANES] = (on * _silu(og[:, sl])).astype(y_ref.dtype)

    _lockstep([wkv_stream(hp) for hp in range(RWKV_HEADS // 2)]
              + [hgrn_stream(h) for h in range(HGRN_HEADS)])

    @pl.when(c == pl.num_programs(1) - 1)
    def _():
        for s in range(n_seq):
            for hp in range(RWKV_HEADS // 2):
                wkvout_ref[s, 2 * hp] = wkv_sc[hp, s * LANES:s * LANES + 64, 0:64]
                wkvout_ref[s, 2 * hp + 1] = wkv_sc[hp, s * LANES + 64:(s + 1) * LANES, 64:128]
            for h in range(HGRN_HEADS):
                hgout_ref[s, h] = hg_sc[h, s * LANES:(s + 1) * LANES, :].T


def _out_body(x_ref, ym_ref, gt1_ref, sh2_ref, sc2_ref, gt2_ref, n2_ref, nf_ref,
              wo_ref, wu_ref, wd_ref, o_ref):
    y = jnp.dot(ym_ref[...], wo_ref[...], preferred_element_type=F32)
    x1 = x_ref[...] + gt1_ref[...] * y
    h = (_rms(x1, n2_ref[...]) * (1.0 + sc2_ref[...]) + sh2_ref[...]).astype(BF16)
    acc = jnp.zeros_like(x1)
    FC = 1024
    for j in range(D_FF // FC):
        u = jnp.dot(h, wu_ref[:, j * FC:(j + 1) * FC], preferred_element_type=F32)
        u = jnp.square(jnp.maximum(u, 0.0)).astype(BF16)
        acc = acc + jnp.dot(u, wd_ref[j * FC:(j + 1) * FC, :], preferred_element_type=F32)
    x2 = x1 + gt2_ref[...] * acc
    o_ref[...] = _rms(x2, nf_ref[...])


def _params(sem):
    return pltpu.CompilerParams(dimension_semantics=sem, vmem_limit_bytes=VMEM_LIMIT)


def _full(shape):
    return pl.BlockSpec(shape, lambda *_: (0,) * len(shape))


def _ada(c_all, w_ada, b_ada):
    n = c_all.shape[0]
    TN = 1024
    return pl.pallas_call(
        _ada_body,
        grid=(w_ada.shape[1] // TN,),
        in_specs=[pl.BlockSpec((n, D_MODEL), lambda j: (0, 0)),
                  pl.BlockSpec((D_MODEL, TN), lambda j: (0, j)),
                  pl.BlockSpec((1, TN), lambda j: (0, j))],
        out_specs=pl.BlockSpec((n, TN), lambda j: (0, j)),
        out_shape=jax.ShapeDtypeStruct((n, w_ada.shape[1]), F32),
        compiler_params=_params(("arbitrary",)),
        name="ada",
    )(c_all, w_ada, b_ada)


def _mod_specs(mod, cols, TM, rows_per_mod):
    if mod.ndim == 3:
        return [pl.BlockSpec((None, 1, D_MODEL), lambda i, c=c: (i * TM // rows_per_mod, 0, c)) for c in cols]
    return [pl.BlockSpec((TM, D_MODEL), lambda i, c=c: (i, c)) for c in cols]


def _inproj(x, mod, norm1, w_in, TM, rows_per_mod):
    M = x.shape[0]
    return pl.pallas_call(
        _inproj_body,
        grid=(M // TM,),
        in_specs=[pl.BlockSpec((TM, D_MODEL), lambda i: (i, 0))]
                 + _mod_specs(mod, (0, 1), TM, rows_per_mod)
                 + [_full((1, D_MODEL)), _full((D_MODEL, IN_WIDTH))],
        out_specs=pl.BlockSpec((TM, IN_WIDTH), lambda i: (i, 0)),
        out_shape=jax.ShapeDtypeStruct((M, IN_WIDTH), F32),
        compiler_params=_params(("arbitrary",)),
        name="inproj",
    )(x, mod, mod, norm1, w_in)


def _rec(p, shinit, states, small, wcomb, wgate, Ls, n_batch):
    G, C = shinit.shape[0], CHUNK
    NC = p.shape[0] // (G * C)
    n_seq = C // Ls
    wkv_spec = pl.BlockSpec((n_seq, RWKV_HEADS, 64, 64), lambda g, c: (g, 0, 0, 0))
    hg_spec = pl.BlockSpec((n_seq, HGRN_HEADS, LANES, LANES), lambda g, c: (g, 0, 0, 0))
    sh_spec = pl.BlockSpec((None, C, SHIFT_WIDTH), lambda g, c: (g, 0, 0))
    st_in = [] if states is None else list(states)
    st_specs = [] if states is None else [wkv_spec, hg_spec]
    return pl.pallas_call(
        functools.partial(_rec_body, Ls, states is None),
        grid=(G, NC),
        in_specs=[pl.BlockSpec((C, IN_WIDTH), lambda g, c: (g * NC + c, 0)), sh_spec] + st_specs
                 + [_full(s.shape) for s in small] + [_full(wcomb.shape), _full(wgate.shape)],
        out_specs=[pl.BlockSpec((C, D_MODEL), lambda g, c: (g * NC + c, 0)), sh_spec, wkv_spec, hg_spec],
        out_shape=[jax.ShapeDtypeStruct((p.shape[0], D_MODEL), BF16),
                   jax.ShapeDtypeStruct(shinit.shape, F32),
                   jax.ShapeDtypeStruct((n_batch, RWKV_HEADS, 64, 64), F32),
                   jax.ShapeDtypeStruct((n_batch, HGRN_HEADS, LANES, LANES), F32)],
        scratch_shapes=[pltpu.VMEM((RWKV_HEADS // 2, n_seq * LANES, LANES), F32),
                        pltpu.VMEM((HGRN_HEADS, n_seq * LANES, LANES), F32)],
        compiler_params=_params(("arbitrary", "arbitrary")),
        name="rec",
    )(p, shinit, *st_in, *small, wcomb, wgate)


def _out(x, ym, mod, norm2, norm_f, w_out, w_up, w_down, TM, rows_per_mod):
    M = x.shape[0]
    row = lambda i: (i, 0)
    return pl.pallas_call(
        _out_body,
        grid=(M // TM,),
        in_specs=[pl.BlockSpec((TM, D_MODEL), row), pl.BlockSpec((TM, D_MODEL), row)]
                 + _mod_specs(mod, (2, 3, 4, 5), TM, rows_per_mod)
                 + [_full((1, D_MODEL)), _full((1, D_MODEL)),
                    _full(w_out.shape), _full(w_up.shape), _full(w_down.shape)],
        out_specs=pl.BlockSpec((TM, D_MODEL), row),
        out_shape=jax.ShapeDtypeStruct((M, D_MODEL), F32),
        compiler_params=_params(("arbitrary",)),
        name="outmlp",
    )(x, ym, mod, mod, mod, mod, norm2, norm_f, w_out, w_up, w_down)


def kernel(x_prompt, x_sample, c_prompt, c_sample, state_shift, state_wkv, state_hgrn, norm1, norm2, norm_f, w_ada, b_ada, w_in, mu_shift, w0, w_decay_up, a0, w_aaa_up, w_gate_up, k_k, k_a, r_k, lnx_w, lnx_b, hgrn_lb, hgrn_gnorm, w_out, w_up, w_down):
    BP, TP, _ = x_prompt.shape
    BS, TS, _ = x_sample.shape
    l = 0
    row = lambda t: t.reshape(1, -1)

    mod = _ada(jnp.concatenate([c_prompt, c_sample], axis=0), w_ada[l], row(b_ada[l]))
    mod_p = mod[:BP].reshape(BP, 1, 6 * D_MODEL)
    mod_s = jnp.repeat(mod[BP:], TS, axis=0)

    w_in_b = w_in[l].astype(BF16)
    w_out_b = w_out[l].astype(BF16)
    w_up_b = w_up[l].astype(BF16)
    w_down_b = w_down[l].astype(BF16)
    zer = jnp.zeros((64, RWKV_WIDTH), F32)
    wcomb = jnp.concatenate([jnp.concatenate([w_decay_up[l], zer], axis=1),
                             jnp.concatenate([zer, w_aaa_up[l]], axis=1)], axis=0).astype(BF16)
    wgate = w_gate_up[l].astype(BF16)
    small = [row(mu_shift[l]), row(w0[l]), row(a0[l]), row(k_k[l]), row(k_a[l]), row(r_k[l]),
             row(lnx_w[l]), row(lnx_b[l]), hgrn_lb, row(hgrn_gnorm[l])]
    n1, n2, nf = row(norm1[l]), row(norm2[l]), row(norm_f)

    def trunk(x2d, modx, rows_per_mod, shinit, states, Ls, n_batch, TM):
        p = _inproj(x2d, modx, n1, w_in_b, TM, rows_per_mod)
        ym, shout, wkv_o, hg_o = _rec(p, shinit, states, small, wcomb, wgate, Ls, n_batch)
        y = _out(x2d, ym, modx, n2, nf, w_out_b, w_up_b, w_down_b, TM, rows_per_mod)
        return y, shout, wkv_o, hg_o

    yp, shp, wkvp, hgp = trunk(x_prompt.reshape(BP * TP, D_MODEL), mod_p, TP,
                               jnp.zeros((BP, CHUNK, SHIFT_WIDTH), F32), None, CHUNK, BP, 256)
    y_prompt = yp.reshape(BP, TP, D_MODEL)
    shift_p = shp[:, CHUNK - 1][None]
    wkv_p = wkvp[None]
    hgrn_p = hgp[None]

    n_seq = CHUNK // TS
    GS = BS // n_seq
    shinit_s = jnp.zeros((GS, n_seq, TS, SHIFT_WIDTH), F32).at[:, :, 0].set(
        state_shift[l].reshape(GS, n_seq, SHIFT_WIDTH)).reshape(GS, CHUNK, SHIFT_WIDTH)
    ys, shs, wkvs, hgs = trunk(x_sample.reshape(BS * TS, D_MODEL), mod_s, 1, shinit_s,
                               (state_wkv[l], state_hgrn[l]), TS, BS, 256)
    y_sample = ys.reshape(BS, TS, D_MODEL)
    shift_s = shs.reshape(GS, n_seq, TS, SHIFT_WIDTH)[:, :, TS - 1].reshape(BS, SHIFT_WIDTH)[None]
    wkv_s = wkvs[None]
    hgrn_s = hgs[None]

    return (y_prompt, y_sample, shift_p, wkv_p, hgrn_p, shift_s, wkv_s, hgrn_s)
```

```python
import functools

import jax
import jax.numpy as jnp
from jax import lax
from jax.experimental import pallas as pl
from jax.experimental.pallas import tpu as pltpu

F32 = jnp.float32
BF16 = jnp.bfloat16

D_MODEL = 1024
RWKV_HEADS = 8
RWKV_WIDTH = 512
HGRN_HEADS = 4
HGRN_WIDTH = 512
SHIFT_WIDTH = 1792
IN_WIDTH = 3840
D_FF = 4096
NORM_EPS = 1e-6
LNX_EPS = 64e-5

LANES = 128
HEAD_PAIR = 128
CHUNK = 64
SUB = 16
PROMPT_GROUPS_PER_STEP = 4
VMEM_LIMIT = 58 * 1024 * 1024


def _dot(a, b):
    return jnp.dot(a.astype(BF16), b.astype(BF16), preferred_element_type=F32)


def _dot_nt(a, b):
    return lax.dot_general(a.astype(BF16), b.astype(BF16), (((1,), (1,)), ((), ())),
                           preferred_element_type=F32)


def _split3(x):
    hi = x.astype(BF16)
    r1 = x - hi.astype(F32)
    mid = r1.astype(BF16)
    lo = (r1 - mid.astype(F32)).astype(BF16)
    return hi, mid, lo


def _dot_exact_l(m, x):
    mb = m.astype(BF16)
    hi, mid, lo = _split3(x)
    d = lambda part: jnp.dot(mb, part, preferred_element_type=F32)
    return d(hi) + d(mid) + d(lo)


def _group_sums(xs, gmat):
    n, rows = len(xs), xs[0].shape[0]
    hi, mid, lo = _split3(jnp.concatenate(xs, axis=0))
    P = jnp.dot(jnp.concatenate([hi, mid, lo], axis=0), gmat.astype(BF16), preferred_element_type=F32)
    R = P[:n * rows] + P[n * rows:2 * n * rows] + P[2 * n * rows:]
    return [R[i * rows:(i + 1) * rows] for i in range(n)]


def _iota(shape, dim):
    return lax.broadcasted_iota(jnp.int32, shape, dim)


def _shr(x, n):
    return lax.shift_right_logical(x, jnp.int32(n.bit_length() - 1))


def _sigmoid(x):
    return 1.0 / (1.0 + jnp.exp(-x))


def _silu(x):
    return x * _sigmoid(x)


def _softplus(x):
    return jnp.maximum(x, 0.0) + jnp.log(1.0 + jnp.exp(-jnp.abs(x)))


def _rms(x, gain):
    return x * lax.rsqrt(jnp.mean(x * x, axis=-1, keepdims=True) + NORM_EPS) * gain


def _make_masks(C, Ls):
    row = _iota((C, C), 0)
    col = _iota((C, C), 1)
    seq = lambda t: _shr(t, Ls)
    blk = lambda t: _shr(t, SUB)
    same = seq(row) == seq(col)
    R = _iota((2 * C, 2 * C), 0)
    Q = _iota((2 * C, 2 * C), 1)
    tr = R & (C - 1)
    tq = Q & (C - 1)
    same2 = seq(tr) == seq(tq)
    lvl = []
    s = 1
    while s < Ls:
        lvl.append((_shr(R, 2 * s) == _shr(Q, 2 * s)) & ((R & (2 * s - 1)) >= s) & ((Q & (2 * s - 1)) < s))
        s *= 2
    return dict(
        row=row, col=col, same=same,
        mi=same & (col <= row),
        sameblk=blk(row) == blk(col),
        eye2=(R == Q).astype(F32),
        lvl=lvl,
        ms_hi2=same2 & (tq < tr) & (Q >= C),
        mi22=same2 & (tq <= tr),
    )


def _tri_inverse(G, mk):
    lvl = mk['lvl']
    D = mk['eye2'] + jnp.where(lvl[0], G, 0.0)
    for m in lvl[1:]:
        LD = _dot(jnp.where(m, G, 0.0), D)
        yield
        D = D + _dot(D, LD)
        yield
    return D


def _lockstep(streams):
    streams = list(streams)
    out = [None] * len(streams)
    alive = list(range(len(streams)))
    while alive:
        for i in list(alive):
            try:
                next(streams[i])
            except StopIteration as stop:
                out[i] = stop.value
                alive.remove(i)
    return out


def _select_seq(full, n_seq, sid_rows):
    if n_seq == 1:
        return full
    acc = jnp.where(sid_rows == 0, full[:, 0:LANES], 0.0)
    for s in range(1, n_seq):
        acc = acc + jnp.where(sid_rows == s, full[:, s * LANES:(s + 1) * LANES], 0.0)
    return acc


def _expand_seq(x, n_seq, sid_rows):
    if n_seq == 1:
        return x
    return jnp.concatenate([jnp.where(sid_rows == s, x, 0.0) for s in range(n_seq)], axis=1)


def _wkv_pair_chunk(r, lw, cw, wl, k, v, kk, a, S, mk, Ls):
    C = r.shape[0]
    n_seq = C // Ls
    lo_half = _iota((C, HEAD_PAIR), 1) < 64
    own = (_iota((2 * C, HEAD_PAIR), 1) < 64) == (_iota((2 * C, HEAD_PAIR), 0) < C)
    stack = lambda t: jnp.where(own, jnp.concatenate([t, t], axis=0), 0.0)
    e_neg = jnp.exp(-cw)
    at = -kk * jnp.exp(cw - lw)
    bt = kk * a * e_neg
    kt = k * e_neg
    rt = r * jnp.exp(cw)
    Y = jnp.concatenate([bt, kt], axis=0)
    at2 = stack(at)
    G = _dot_nt(at2, jnp.concatenate([bt, bt], axis=0))
    ga = _dot_nt(at2, Y)
    gr = _dot_nt(stack(rt), Y)
    sid2 = _shr(_iota((2 * C, LANES), 0) & (C - 1), Ls)
    XO = _select_seq(_dot_nt(jnp.concatenate([at, rt], axis=0), S), n_seq, sid2)
    yield
    Vz = jnp.concatenate([jnp.zeros_like(v), v], axis=0)
    xk = _dot(jnp.where(mk['ms_hi2'], ga, 0.0), Vz)
    D = yield from _tri_inverse(G, mk)
    X = XO[:C] + jnp.where(lo_half, xk[:C], xk[C:])
    DX = _dot(D, jnp.concatenate([X, X], axis=0))
    yield
    U = jnp.where(lo_half, DX[:C], DX[C:])
    Z = jnp.concatenate([U, v], axis=0)
    oz = _dot(jnp.where(mk['mi22'], gr, 0.0), Z)
    upd = _dot(Z.T, _expand_seq(Y, n_seq, sid2))
    yield
    o = XO[C:] + jnp.where(lo_half, oz[:C], oz[C:])
    bd = _shr(_iota((HEAD_PAIR, HEAD_PAIR), 0), 64) == _shr(_iota((HEAD_PAIR, HEAD_PAIR), 1), 64)
    new = []
    for s in range(n_seq):
        Ss = S[s * LANES:(s + 1) * LANES, :]
        new.append(jnp.where(bd, (Ss + upd[:, s * LANES:(s + 1) * LANES]) * wl[s * Ls:s * Ls + 1, :], 0.0))
    return o, new


def _hgrn_head_chunk(q, kf, b, blast, iv, ST, mk, Ls):
    C = q.shape[0]
    assert Ls == C or Ls <= SUB
    n_seq = C // Ls
    sid = _shr(_iota((C, LANES), 0), Ls)
    if Ls > SUB:
        rows_of = lambda off: jnp.concatenate(
            [jnp.broadcast_to(b[SUB * j + off:SUB * j + off + 1, :], (SUB, LANES)) for j in range(C // SUB)], axis=0)
        bmid = rows_of(SUB // 2 - 1)
        bend = rows_of(SUB - 1)
    else:
        bmid = 0.0
    o = _select_seq(_dot_nt(q * jnp.exp(b), ST), n_seq, sid)
    att = jnp.where(mk['mi'] & mk['sameblk'],
                    _dot_nt(q * jnp.exp(b - bmid), kf * jnp.exp(bmid - b)), 0.0)
    if Ls > SUB:
        ko = kf * jnp.exp(jnp.minimum(bend - b, 0.0))
        rb = _shr(mk['row'], SUB)
        cb = _shr(mk['col'], SUB)
        for j in range(C // SUB - 1):
            ref = b[SUB * j + SUB - 1:SUB * j + SUB, :]
            qo = q * jnp.exp(jnp.minimum(b - ref, 0.0))
            att = att + jnp.where(mk['same'] & (cb == j) & (rb > j), _dot_nt(qo, ko), 0.0)
    ke = kf * jnp.exp(blast - b)
    upd = _dot(iv.T, _expand_seq(ke, n_seq, sid))
    yield
    o = o + _dot(att, iv)
    yield
    dec = jnp.exp(blast)
    new = []
    for s in range(n_seq):
        new.append(ST[s * LANES:(s + 1) * LANES, :] * dec[s * Ls:s * Ls + 1, :]
                   + upd[:, s * LANES:(s + 1) * LANES])
    return o, new


def _ada_body(c_ref, w_ref, b_ref, o_ref):
    o_ref[...] = _dot(_silu(c_ref[...]), w_ref[...]) + b_ref[...]


def _inproj_body(x_ref, sh_ref, sc_ref, n1_ref, w_ref, p_ref):
    h = _rms(x_ref[...], n1_ref[...]) * (1.0 + sc_ref[...]) + sh_ref[...]
    p_ref[...] = jnp.dot(h.astype(BF16), w_ref[...], preferred_element_type=F32)


def _rec_body(Ls, single_seq, R, *refs):
    if single_seq:
        p_ref, shinit_ref = refs[:2]
        wkvin_ref = hgin_ref = None
        rest = refs[2:]
    else:
        p_ref, shinit_ref, wkvin_ref, hgin_ref = refs[:4]
        rest = refs[4:]
    (mu_ref, w0_ref, a0_ref, kk_ref, ka_ref, rk_ref, lnw_ref, lnb_ref, lb_ref, gn_ref, wcomb_ref, wgate_ref,
     y_ref, shout_ref, wkvout_ref, hgout_ref, wkv_sc, hg_sc) = rest
    C = CHUNK
    n_seq = C // Ls
    NP = RWKV_HEADS // 2
    c = pl.program_id(1)
    rows = lambda j: slice(j * C, (j + 1) * C)
    blk = lambda s: slice(s * LANES, (s + 1) * LANES)

    @pl.when(c == 0)
    def _():
        shout_ref[...] = jnp.zeros_like(shout_ref)
        wkv_sc[...] = jnp.zeros_like(wkv_sc)
        if single_seq:
            hg_sc[...] = jnp.zeros_like(hg_sc)
        else:
            for j in range(R):
                for s in range(n_seq):
                    for hp in range(NP):
                        wkv_sc[j, hp, s * LANES:s * LANES + 64, 0:64] = wkvin_ref[j * n_seq + s, 2 * hp]
                        wkv_sc[j, hp, s * LANES + 64:(s + 1) * LANES, 64:128] = wkvin_ref[j * n_seq + s, 2 * hp + 1]
                    for h in range(HGRN_HEADS):
                        hg_sc[j, h, blk(s), :] = hgin_ref[j * n_seq + s, h].T

    mk = _make_masks(C, Ls)

    rowv = _iota((C, SHIFT_WIDTH), 0)
    if single_seq:
        start = rowv == jnp.where(c == 0, 0, -1)
    else:
        start = (rowv & (Ls - 1)) == 0
    xs = []
    for j in range(R):
        p_rw = p_ref[j, :, :SHIFT_WIDTH]
        p_prev = jnp.where(rowv == 0, shout_ref[j, C - 1:C, :], pltpu.roll(p_rw, 1, 0))
        p_prev = jnp.where(start, shinit_ref[j], p_prev)
        shout_ref[j] = p_rw
        xs.append(p_rw + (p_prev - p_rw) * mu_ref[...])
    x = jnp.concatenate(xs, axis=0)
    cat = lambda lo, hi: jnp.concatenate([p_ref[j, :, lo:hi] for j in range(R)], axis=0)

    r = x[:, 0:512]
    k = x[:, 512:1024]
    v = x[:, 1024:1536]
    wa = x[:, 1536:1664]
    gd = x[:, 1664:1792]
    lane = _iota((R * C, LANES), 1)
    da = _dot(jnp.where(lane < 64, jnp.tanh(wa), wa), wcomb_ref[...])
    w_log = -_softplus(-(w0_ref[...] + da[:, :512])) - 0.5
    lw = -jnp.exp(w_log)
    a = _sigmoid(a0_ref[...] + da[:, 512:])
    g = _dot(_sigmoid(gd), wgate_ref[...])
    kkr = k * kk_ref[...]
    kmod = k * (1.0 + (a - 1.0) * ka_ref[...])
    rkr = r * kmod * rk_ref[...]
    lnw = lnw_ref[...]
    lnb = lnb_ref[...]

    lbp = lb_ref[...]
    m = jnp.maximum(lbp[0:1, :], lbp[1:2, :])
    e0 = jnp.exp(lbp[0:1, :] - m)
    e1 = jnp.exp(lbp[1:2, :] - m)
    lb = e0 / (e0 + e1)
    q = _silu(cat(1792, 2304))
    f = lb + (1.0 - lb) * _sigmoid(cat(2304, 2816))
    kf = 1.0 - f
    iv = cat(2816, 3328)
    og = cat(3328, 3840)

    logs = jnp.concatenate([lw, jnp.log(f)], axis=1)
    cums, tots = [], []
    for j in range(R):
        if single_seq:
            cum = _dot_exact_l(mk['mi'].astype(F32), logs[rows(j)])
            tot = cum[C - 1:C, :]
        else:
            both = _dot_exact_l(jnp.concatenate([mk['mi'], mk['same']], axis=0).astype(F32), logs[rows(j)])
            cum, tot = both[:C], both[C:]
        cums.append(cum)
        tots.append(tot)

    gsum = (_shr(_iota((HEAD_PAIR, HEAD_PAIR), 0), 64) == _shr(_iota((HEAD_PAIR, HEAD_PAIR), 1), 64)).astype(F32)
    pairs = [slice(hp * HEAD_PAIR, (hp + 1) * HEAD_PAIR) for hp in range(NP)]
    sums = _group_sums([kkr[:, sl] * kkr[:, sl] for sl in pairs] + [rkr[:, sl] for sl in pairs], gsum)

    def wkv_stream(j, hp):
        sl = pairs[hp]
        kk_s = kkr[rows(j), sl] / jnp.maximum(jnp.sqrt(sums[hp][rows(j)]), 1e-12)
        o, new = yield from _wkv_pair_chunk(r[rows(j), sl], lw[rows(j), sl], cums[j][:, sl],
                                            jnp.exp(tots[j][:, sl]), kmod[rows(j), sl], v[rows(j), sl],
                                            kk_s, a[rows(j), sl], wkv_sc[j, hp], mk, Ls)
        for s in range(n_seq):
            wkv_sc[j, hp, blk(s), :] = new[s]
        return o

    def hgrn_stream(j, h):
        sl = slice(RWKV_WIDTH + h * LANES, RWKV_WIDTH + (h + 1) * LANES)
        hs = blk(h)
        o, new = yield from _hgrn_head_chunk(q[rows(j), hs], kf[rows(j), hs], cums[j][:, sl], tots[j][:, sl],
                                             iv[rows(j), hs], hg_sc[j, h], mk, Ls)
        for s in range(n_seq):
            hg_sc[j, h, blk(s), :] = new[s]
        on = _rms(o, gn_ref[...])
        y_ref[j, :, sl] = (on * _silu(og[rows(j), hs])).astype(y_ref.dtype)

    outs = _lockstep([wkv_stream(j, hp) for j in range(R) for hp in range(NP)]
                     + [hgrn_stream(j, h) for j in range(R) for h in range(HGRN_HEADS)])

    os_ = [jnp.concatenate([outs[j * NP + hp] for j in range(R)], axis=0) for hp in range(NP)]
    means = _group_sums(os_, gsum)
    ds = [o - mu * (1.0 / 64.0) for o, mu in zip(os_, means)]
    vars_ = _group_sums([d * d for d in ds], gsum)
    for hp, sl in enumerate(pairs):
        on = ds[hp] * lax.rsqrt(vars_[hp] * (1.0 / 64.0) + LNX_EPS) * lnw[:, sl] + lnb[:, sl]
        yv = ((on + sums[NP + hp] * v[:, sl]) * g[:, sl]).astype(y_ref.dtype)
        for j in range(R):
            y_ref[j, :, sl] = yv[rows(j)]

    @pl.when(c == pl.num_programs(1) - 1)
    def _():
        for j in range(R):
            for s in range(n_seq):
                for hp in range(NP):
                    wkvout_ref[j * n_seq + s, 2 * hp] = wkv_sc[j, hp, s * LANES:s * LANES + 64, 0:64]
                    wkvout_ref[j * n_seq + s, 2 * hp + 1] = wkv_sc[j, hp, s * LANES + 64:(s + 1) * LANES, 64:128]
                for h in range(HGRN_HEADS):
                    hgout_ref[j * n_seq + s, h] = hg_sc[j, h, blk(s), :].T


def _out_body(x_ref, ym_ref, gt1_ref, sh2_ref, sc2_ref, gt2_ref, n2_ref, nf_ref,
              wo_ref, wu_ref, wd_ref, o_ref):
    y = jnp.dot(ym_ref[...], wo_ref[...], preferred_element_type=F32)
    x1 = x_ref[...] + gt1_ref[...] * y
    h = (_rms(x1, n2_ref[...]) * (1.0 + sc2_ref[...]) + sh2_ref[...]).astype(BF16)
    acc = jnp.zeros_like(x1)
    FC = 1024
    for j in range(D_FF // FC):
        u = jnp.dot(h, wu_ref[:, j * FC:(j + 1) * FC], preferred_element_type=F32)
        u = jnp.square(jnp.maximum(u, 0.0)).astype(BF16)
        acc = acc + jnp.dot(u, wd_ref[j * FC:(j + 1) * FC, :], preferred_element_type=F32)
    x2 = x1 + gt2_ref[...] * acc
    o_ref[...] = _rms(x2, nf_ref[...])


def _params(sem):
    return pltpu.CompilerParams(dimension_semantics=sem, vmem_limit_bytes=VMEM_LIMIT)


def _full(shape):
    return pl.BlockSpec(shape, lambda *_: (0,) * len(shape))


def _ada(c_all, w_ada, b_ada):
    n = c_all.shape[0]
    TN = 1024
    return pl.pallas_call(
        _ada_body,
        grid=(w_ada.shape[1] // TN,),
        in_specs=[pl.BlockSpec((n, D_MODEL), lambda j: (0, 0)),
                  pl.BlockSpec((D_MODEL, TN), lambda j: (0, j)),
                  pl.BlockSpec((1, TN), lambda j: (0, j))],
        out_specs=pl.BlockSpec((n, TN), lambda j: (0, j)),
        out_shape=jax.ShapeDtypeStruct((n, w_ada.shape[1]), F32),
        compiler_params=_params(("arbitrary",)),
        name="ada",
    )(c_all, w_ada, b_ada)


def _mod_specs(mod, cols, TM, rows_per_mod):
    if mod.ndim == 3:
        return [pl.BlockSpec((None, 1, D_MODEL), lambda i, c=c: (i * TM // rows_per_mod, 0, c)) for c in cols]
    return [pl.BlockSpec((TM, D_MODEL), lambda i, c=c: (i, c)) for c in cols]


def _inproj(x, mod, norm1, w_in, TM, rows_per_mod):
    M = x.shape[0]
    return pl.pallas_call(
        _inproj_body,
        grid=(M // TM,),
        in_specs=[pl.BlockSpec((TM, D_MODEL), lambda i: (i, 0))]
                 + _mod_specs(mod, (0, 1), TM, rows_per_mod)
                 + [_full((1, D_MODEL)), _full((D_MODEL, IN_WIDTH))],
        out_specs=pl.BlockSpec((TM, IN_WIDTH), lambda i: (i, 0)),
        out_shape=jax.ShapeDtypeStruct((M, IN_WIDTH), F32),
        compiler_params=_params(("arbitrary",)),
        name="inproj",
    )(x, mod, mod, norm1, w_in)


def _rec(p, shinit, states, small, wcomb, wgate, Ls, n_batch, R):
    Gt, C = shinit.shape[0], CHUNK
    NC = p.shape[1] // C
    n_seq = C // Ls
    wkv_spec = pl.BlockSpec((R * n_seq, RWKV_HEADS, 64, 64), lambda g, c: (g, 0, 0, 0))
    hg_spec = pl.BlockSpec((R * n_seq, HGRN_HEADS, LANES, LANES), lambda g, c: (g, 0, 0, 0))
    sh_spec = pl.BlockSpec((R, C, SHIFT_WIDTH), lambda g, c: (g, 0, 0))
    st_in = [] if states is None else list(states)
    st_specs = [] if states is None else [wkv_spec, hg_spec]
    return pl.pallas_call(
        functools.partial(_rec_body, Ls, states is None, R),
        grid=(Gt // R, NC),
        in_specs=[pl.BlockSpec((R, C, IN_WIDTH), lambda g, c: (g, c, 0)), sh_spec] + st_specs
                 + [_full(s.shape) for s in small] + [_full(wcomb.shape), _full(wgate.shape)],
        out_specs=[pl.BlockSpec((R, C, D_MODEL), lambda g, c: (g, c, 0)), sh_spec, wkv_spec, hg_spec],
        out_shape=[jax.ShapeDtypeStruct((Gt, NC * C, D_MODEL), BF16),
                   jax.ShapeDtypeStruct(shinit.shape, F32),
                   jax.ShapeDtypeStruct((n_batch, RWKV_HEADS, 64, 64), F32),
                   jax.ShapeDtypeStruct((n_batch, HGRN_HEADS, LANES, LANES), F32)],
        scratch_shapes=[pltpu.VMEM((R, RWKV_HEADS // 2, n_seq * LANES, LANES), F32),
                        pltpu.VMEM((R, HGRN_HEADS, n_seq * LANES, LANES), F32)],
        compiler_params=_params(("arbitrary", "arbitrary")),
        name="rec",
    )(p, shinit, *st_in, *small, wcomb, wgate)


def _out(x, ym, mod, norm2, norm_f, w_out, w_up, w_down, TM, rows_per_mod):
    M = x.shape[0]
    row = lambda i: (i, 0)
    return pl.pallas_call(
        _out_body,
        grid=(M // TM,),
        in_specs=[pl.BlockSpec((TM, D_MODEL), row), pl.BlockSpec((TM, D_MODEL), row)]
                 + _mod_specs(mod, (2, 3, 4, 5), TM, rows_per_mod)
                 + [_full((1, D_MODEL)), _full((1, D_MODEL)),
                    _full(w_out.shape), _full(w_up.shape), _full(w_down.shape)],
        out_specs=pl.BlockSpec((TM, D_MODEL), row),
        out_shape=jax.ShapeDtypeStruct((M, D_MODEL), F32),
        compiler_params=_params(("arbitrary",)),
        name="outmlp",
    )(x, ym, mod, mod, mod, mod, norm2, norm_f, w_out, w_up, w_down)


def kernel(x_prompt, x_sample, c_prompt, c_sample, state_shift, state_wkv, state_hgrn, norm1, norm2, norm_f, w_ada, b_ada, w_in, mu_shift, w0, w_decay_up, a0, w_aaa_up, w_gate_up, k_k, k_a, r_k, lnx_w, lnx_b, hgrn_lb, hgrn_gnorm, w_out, w_up, w_down):
    BP, TP, _ = x_prompt.shape
    BS, TS, _ = x_sample.shape
    l = 0
    row = lambda t: t.reshape(1, -1)

    mod = _ada(jnp.concatenate([c_prompt, c_sample], axis=0), w_ada[l], row(b_ada[l]))
    mod_p = mod[:BP].reshape(BP, 1, 6 * D_MODEL)
    mod_s = jnp.repeat(mod[BP:], TS, axis=0)

    w_in_b = w_in[l].astype(BF16)
    w_out_b = w_out[l].astype(BF16)
    w_up_b = w_up[l].astype(BF16)
    w_down_b = w_down[l].astype(BF16)
    zer = jnp.zeros((64, RWKV_WIDTH), F32)
    wcomb = jnp.concatenate([jnp.concatenate([w_decay_up[l], zer], axis=1),
                             jnp.concatenate([zer, w_aaa_up[l]], axis=1)], axis=0).astype(BF16)
    wgate = w_gate_up[l].astype(BF16)
    small = [row(mu_shift[l]), row(w0[l]), row(a0[l]), row(k_k[l]), row(k_a[l]), row(r_k[l]),
             row(lnx_w[l]), row(lnx_b[l]), hgrn_lb, row(hgrn_gnorm[l])]
    n1, n2, nf = row(norm1[l]), row(norm2[l]), row(norm_f)

    def trunk(x2d, modx, rows_per_mod, shinit, states, Ls, n_batch, TM, R):
        Gt = shinit.shape[0]
        p = _inproj(x2d, modx, n1, w_in_b, TM, rows_per_mod)
        ym, shout, wkv_o, hg_o = _rec(p.reshape(Gt, -1, IN_WIDTH), shinit, states, small, wcomb, wgate,
                                      Ls, n_batch, R)
        y = _out(x2d, ym.reshape(-1, D_MODEL), modx, n2, nf, w_out_b, w_up_b, w_down_b, TM, rows_per_mod)
        return y, shout, wkv_o, hg_o

    yp, shp, wkvp, hgp = trunk(x_prompt.reshape(BP * TP, D_MODEL), mod_p, TP,
                               jnp.zeros((BP, CHUNK, SHIFT_WIDTH), F32), None, CHUNK, BP, 256, PROMPT_GROUPS_PER_STEP)
    y_prompt = yp.reshape(BP, TP, D_MODEL)
    shift_p = shp[:, CHUNK - 1][None]
    wkv_p = wkvp[None]
    hgrn_p = hgp[None]

    n_seq = CHUNK // TS
    GS = BS // n_seq
    shinit_s = jnp.zeros((GS, n_seq, TS, SHIFT_WIDTH), F32).at[:, :, 0].set(
        state_shift[l].reshape(GS, n_seq, SHIFT_WIDTH)).reshape(GS, CHUNK, SHIFT_WIDTH)
    ys, shs, wkvs, hgs = trunk(x_sample.reshape(BS * TS, D_MODEL), mod_s, 1, shinit_s,
                               (state_wkv[l], state_hgrn[l]), TS, BS, 256, 1)
    y_sample = ys.reshape(BS, TS, D_MODEL)
    shift_s = shs.reshape(GS, n_seq, TS, SHIFT_WIDTH)[:, :, TS - 1].reshape(BS, SHIFT_WIDTH)[None]
    wkv_s = wkvs[None]
    hgrn_s = hgs[None]

    return (y_prompt, y_sample, shift_p, wkv_p, hgrn_p, shift_s, wkv_s, hgrn_s)
```

```python
import functools

import jax
import jax.numpy as jnp
from jax import lax
from jax.experimental import pallas as pl
from jax.experimental.pallas import tpu as pltpu

F32 = jnp.float32
BF16 = jnp.bfloat16

D_MODEL = 1024
RWKV_HEADS = 8
RWKV_WIDTH = 512
HGRN_HEADS = 4
HGRN_WIDTH = 512
SHIFT_WIDTH = 1792
IN_WIDTH = 3840
D_FF = 4096
NORM_EPS = 1e-6
LNX_EPS = 64e-5

LANES = 128
HEAD_PAIR = 128
CHUNK = 64
SUB = 16
PROMPT_GROUPS_PER_STEP = 4
VMEM_LIMIT = 58 * 1024 * 1024


def _dot(a, b):
    return jnp.dot(a.astype(BF16), b.astype(BF16), preferred_element_type=F32)


def _dot_nt(a, b):
    return lax.dot_general(a.astype(BF16), b.astype(BF16), (((1,), (1,)), ((), ())),
                           preferred_element_type=F32)


def _dot_2pass_l(m, x):
    mb = m.astype(BF16)
    hi = x.astype(BF16)
    lo = (x - hi.astype(F32)).astype(BF16)
    return jnp.dot(mb, hi, preferred_element_type=F32) + jnp.dot(mb, lo, preferred_element_type=F32)


def _group_sums(xs, gmat):
    n, rows = len(xs), xs[0].shape[0]
    P = _dot(jnp.concatenate(xs, axis=0), gmat)
    return [P[i * rows:(i + 1) * rows] for i in range(n)]


def _iota(shape, dim):
    return lax.broadcasted_iota(jnp.int32, shape, dim)


def _shr(x, n):
    return lax.shift_right_logical(x, jnp.int32(n.bit_length() - 1))


def _sigmoid(x):
    return 1.0 / (1.0 + jnp.exp(-x))


def _silu(x):
    return x * _sigmoid(x)


def _softplus(x):
    return jnp.maximum(x, 0.0) + jnp.log(1.0 + jnp.exp(-jnp.abs(x)))


def _rms(x, gain):
    return x * lax.rsqrt(jnp.mean(x * x, axis=-1, keepdims=True) + NORM_EPS) * gain


def _make_masks(C, Ls):
    row = _iota((C, C), 0)
    col = _iota((C, C), 1)
    seq = lambda t: _shr(t, Ls)
    blk = lambda t: _shr(t, SUB)
    same = seq(row) == seq(col)
    R = _iota((2 * C, 2 * C), 0)
    Q = _iota((2 * C, 2 * C), 1)
    tr = R & (C - 1)
    tq = Q & (C - 1)
    same2 = seq(tr) == seq(tq)
    lvl = []
    s = 1
    while s < Ls:
        lvl.append((_shr(R, 2 * s) == _shr(Q, 2 * s)) & ((R & (2 * s - 1)) >= s) & ((Q & (2 * s - 1)) < s))
        s *= 2
    return dict(
        row=row, col=col, same=same,
        mi=same & (col <= row),
        sameblk=blk(row) == blk(col),
        eye2=(R == Q).astype(F32),
        lvl=lvl,
        ms_hi2=same2 & (tq < tr) & (Q >= C),
        mi22=same2 & (tq <= tr),
    )


def _tri_inverse(G, mk):
    lvl = mk['lvl']
    D = mk['eye2'] + jnp.where(lvl[0], G, 0.0)
    for m in lvl[1:]:
        LD = _dot(jnp.where(m, G, 0.0), D)
        yield
        D = D + _dot(D, LD)
        yield
    return D


def _lockstep(streams):
    streams = list(streams)
    out = [None] * len(streams)
    alive = list(range(len(streams)))
    while alive:
        for i in list(alive):
            try:
                next(streams[i])
            except StopIteration as stop:
                out[i] = stop.value
                alive.remove(i)
    return out


def _select_seq(full, n_seq, sid_rows):
    if n_seq == 1:
        return full
    acc = jnp.where(sid_rows == 0, full[:, 0:LANES], 0.0)
    for s in range(1, n_seq):
        acc = acc + jnp.where(sid_rows == s, full[:, s * LANES:(s + 1) * LANES], 0.0)
    return acc


def _expand_seq(x, n_seq, sid_rows):
    if n_seq == 1:
        return x
    return jnp.concatenate([jnp.where(sid_rows == s, x, 0.0) for s in range(n_seq)], axis=1)


def _wkv_pair_chunk(r, lw, cw, wl, k, v, kk, a, S, mk, Ls):
    C = r.shape[0]
    n_seq = C // Ls
    lo_half = _iota((C, HEAD_PAIR), 1) < 64
    own = (_iota((2 * C, HEAD_PAIR), 1) < 64) == (_iota((2 * C, HEAD_PAIR), 0) < C)
    stack = lambda t: jnp.where(own, jnp.concatenate([t, t], axis=0), 0.0)
    e_neg = jnp.exp(-cw)
    at = -kk * jnp.exp(cw - lw)
    bt = kk * a * e_neg
    kt = k * e_neg
    rt = r * jnp.exp(cw)
    Y = jnp.concatenate([bt, kt], axis=0)
    ga = _dot_nt(stack(at), Y)
    first_cols = _iota((2 * C, 2 * C), 1) < C
    top_rows = _iota((2 * C, 2 * C), 0) < C
    G = (jnp.where(first_cols & top_rows, ga, 0.0)
         + pltpu.roll(jnp.where(first_cols & jnp.logical_not(top_rows), ga, 0.0), C, 1))
    gr = _dot_nt(stack(rt), Y)
    sid2 = _shr(_iota((2 * C, LANES), 0) & (C - 1), Ls)
    XO = _select_seq(_dot_nt(jnp.concatenate([at, rt], axis=0), S), n_seq, sid2)
    yield
    Vz = jnp.concatenate([jnp.zeros_like(v), v], axis=0)
    xk = _dot(jnp.where(mk['ms_hi2'], ga, 0.0), Vz)
    D = yield from _tri_inverse(G, mk)
    X = XO[:C] + jnp.where(lo_half, xk[:C], xk[C:])
    DX = _dot(D, jnp.concatenate([X, X], axis=0))
    yield
    U = jnp.where(lo_half, DX[:C], DX[C:])
    Z = jnp.concatenate([U, v], axis=0)
    oz = _dot(jnp.where(mk['mi22'], gr, 0.0), Z)
    upd = _dot(Z.T, _expand_seq(Y, n_seq, sid2))
    yield
    o = XO[C:] + jnp.where(lo_half, oz[:C], oz[C:])
    bd = _shr(_iota((HEAD_PAIR, HEAD_PAIR), 0), 64) == _shr(_iota((HEAD_PAIR, HEAD_PAIR), 1), 64)
    new = []
    for s in range(n_seq):
        Ss = S[s * LANES:(s + 1) * LANES, :]
        new.append(jnp.where(bd, (Ss + upd[:, s * LANES:(s + 1) * LANES]) * wl[s * Ls:s * Ls + 1, :], 0.0))
    return o, new


def _hgrn_head_chunk(q, kf, b, blast, iv, ST, mk, Ls):
    C = q.shape[0]
    assert Ls == C or Ls <= SUB
    n_seq = C // Ls
    sid = _shr(_iota((C, LANES), 0), Ls)
    if Ls > SUB:
        rows_of = lambda off: jnp.concatenate(
            [jnp.broadcast_to(b[SUB * j + off:SUB * j + off + 1, :], (SUB, LANES)) for j in range(C // SUB)], axis=0)
        bmid = rows_of(SUB // 2 - 1)
        bend = rows_of(SUB - 1)
    else:
        bmid = 0.0
    o = _select_seq(_dot_nt(q * jnp.exp(b), ST), n_seq, sid)
    att = jnp.where(mk['mi'] & mk['sameblk'],
                    _dot_nt(q * jnp.exp(b - bmid), kf * jnp.exp(bmid - b)), 0.0)
    if Ls > SUB:
        ko = kf * jnp.exp(jnp.minimum(bend - b, 0.0))
        rb = _shr(mk['row'], SUB)
        cb = _shr(mk['col'], SUB)
        for j in range(C // SUB - 1):
            ref = b[SUB * j + SUB - 1:SUB * j + SUB, :]
            qo = q * jnp.exp(jnp.minimum(b - ref, 0.0))
            att = att + jnp.where(mk['same'] & (cb == j) & (rb > j), _dot_nt(qo, ko), 0.0)
    ke = kf * jnp.exp(blast - b)
    upd = _dot(iv.T, _expand_seq(ke, n_seq, sid))
    yield
    o = o + _dot(att, iv)
    yield
    dec = jnp.exp(blast)
    new = []
    for s in range(n_seq):
        new.append(ST[s * LANES:(s + 1) * LANES, :] * dec[s * Ls:s * Ls + 1, :]
                   + upd[:, s * LANES:(s + 1) * LANES])
    return o, new


def _ada_body(c_ref, w_ref, b_ref, o_ref):
    o_ref[...] = _dot(_silu(c_ref[...]), w_ref[...]) + b_ref[...]


def _inproj_body(x_ref, sh_ref, sc_ref, n1_ref, w_ref, p_ref):
    h = _rms(x_ref[...], n1_ref[...]) * (1.0 + sc_ref[...]) + sh_ref[...]
    p_ref[...] = jnp.dot(h.astype(BF16), w_ref[...], preferred_element_type=F32)


def _rec_body(Ls, single_seq, R, *refs):
    if single_seq:
        p_ref, shinit_ref = refs[:2]
        wkvin_ref = hgin_ref = None
        rest = refs[2:]
    else:
        p_ref, shinit_ref, wkvin_ref, hgin_ref = refs[:4]
        rest = refs[4:]
    (mu_ref, w0_ref, a0_ref, kk_ref, ka_ref, rk_ref, lnw_ref, lnb_ref, lb_ref, gn_ref, wcomb_ref, wgate_ref,
     y_ref, shout_ref, wkvout_ref, hgout_ref, wkv_sc, hg_sc) = rest
    C = CHUNK
    n_seq = C // Ls
    NP = RWKV_HEADS // 2
    c = pl.program_id(1)
    rows = lambda j: slice(j * C, (j + 1) * C)
    blk = lambda s: slice(s * LANES, (s + 1) * LANES)

    @pl.when(c == 0)
    def _():
        shout_ref[...] = jnp.zeros_like(shout_ref)
        wkv_sc[...] = jnp.zeros_like(wkv_sc)
        if single_seq:
            hg_sc[...] = jnp.zeros_like(hg_sc)
        else:
            for j in range(R):
                for s in range(n_seq):
                    for hp in range(NP):
                        wkv_sc[j, hp, s * LANES:s * LANES + 64, 0:64] = wkvin_ref[j * n_seq + s, 2 * hp]
                        wkv_sc[j, hp, s * LANES + 64:(s + 1) * LANES, 64:128] = wkvin_ref[j * n_seq + s, 2 * hp + 1]
                    for h in range(HGRN_HEADS):
                        hg_sc[j, h, blk(s), :] = hgin_ref[j * n_seq + s, h].T

    mk = _make_masks(C, Ls)

    rowv = _iota((C, SHIFT_WIDTH), 0)
    if single_seq:
        start = rowv == jnp.where(c == 0, 0, -1)
    else:
        start = (rowv & (Ls - 1)) == 0
    xs = []
    for j in range(R):
        p_rw = p_ref[j, :, :SHIFT_WIDTH]
        p_prev = jnp.where(rowv == 0, shout_ref[j, C - 1:C, :], pltpu.roll(p_rw, 1, 0))
        p_prev = jnp.where(start, shinit_ref[j], p_prev)
        shout_ref[j] = p_rw
        xs.append(p_rw + (p_prev - p_rw) * mu_ref[...])
    x = jnp.concatenate(xs, axis=0)
    cat = lambda lo, hi: jnp.concatenate([p_ref[j, :, lo:hi] for j in range(R)], axis=0)

    r = x[:, 0:512]
    k = x[:, 512:1024]
    v = x[:, 1024:1536]
    wa = x[:, 1536:1664]
    gd = x[:, 1664:1792]
    lane = _iota((R * C, LANES), 1)
    da = _dot(jnp.where(lane < 64, jnp.tanh(wa), wa), wcomb_ref[...])
    w_log = -_softplus(-(w0_ref[...] + da[:, :512])) - 0.5
    lw = -jnp.exp(w_log)
    a = _sigmoid(a0_ref[...] + da[:, 512:])
    g = _dot(_sigmoid(gd), wgate_ref[...])
    kkr = k * kk_ref[...]
    kmod = k * (1.0 + (a - 1.0) * ka_ref[...])
    rkr = r * kmod * rk_ref[...]
    lnw = lnw_ref[...]
    lnb = lnb_ref[...]

    lbp = lb_ref[...]
    m = jnp.maximum(lbp[0:1, :], lbp[1:2, :])
    e0 = jnp.exp(lbp[0:1, :] - m)
    e1 = jnp.exp(lbp[1:2, :] - m)
    lb = e0 / (e0 + e1)
    q = _silu(cat(1792, 2304))
    f = lb + (1.0 - lb) * _sigmoid(cat(2304, 2816))
    kf = 1.0 - f
    iv = cat(2816, 3328)
    og = cat(3328, 3840)

    logs = jnp.concatenate([lw, jnp.log(f)], axis=1)
    cums, tots = [], []
    for j in range(R):
        if single_seq:
            cum = _dot_2pass_l(mk['mi'].astype(F32), logs[rows(j)])
            tot = cum[C - 1:C, :]
        else:
            both = _dot_2pass_l(jnp.concatenate([mk['mi'], mk['same']], axis=0).astype(F32), logs[rows(j)])
            cum, tot = both[:C], both[C:]
        cums.append(cum)
        tots.append(tot)

    gsum = (_shr(_iota((HEAD_PAIR, HEAD_PAIR), 0), 64) == _shr(_iota((HEAD_PAIR, HEAD_PAIR), 1), 64)).astype(F32)
    pairs = [slice(hp * HEAD_PAIR, (hp + 1) * HEAD_PAIR) for hp in range(NP)]
    sums = _group_sums([kkr[:, sl] * kkr[:, sl] for sl in pairs] + [rkr[:, sl] for sl in pairs], gsum)

    def wkv_stream(j, hp):
        sl = pairs[hp]
        kk_s = kkr[rows(j), sl] / jnp.maximum(jnp.sqrt(sums[hp][rows(j)]), 1e-12)
        o, new = yield from _wkv_pair_chunk(r[rows(j), sl], lw[rows(j), sl], cums[j][:, sl],
                                            jnp.exp(tots[j][:, sl]), kmod[rows(j), sl], v[rows(j), sl],
                                            kk_s, a[rows(j), sl], wkv_sc[j, hp], mk, Ls)
        for s in range(n_seq):
            wkv_sc[j, hp, blk(s), :] = new[s]
        return o

    def hgrn_stream(j, h):
        sl = slice(RWKV_WIDTH + h * LANES, RWKV_WIDTH + (h + 1) * LANES)
        hs = blk(h)
        o, new = yield from _hgrn_head_chunk(q[rows(j), hs], kf[rows(j), hs], cums[j][:, sl], tots[j][:, sl],
                                             iv[rows(j), hs], hg_sc[j, h], mk, Ls)
        for s in range(n_seq):
            hg_sc[j, h, blk(s), :] = new[s]
        on = _rms(o, gn_ref[...])
        y_ref[j, :, sl] = (on * _silu(og[rows(j), hs])).astype(y_ref.dtype)

    outs = _lockstep([wkv_stream(j, hp) for j in range(R) for hp in range(NP)]
                     + [hgrn_stream(j, h) for j in range(R) for h in range(HGRN_HEADS)])

    os_ = [jnp.concatenate([outs[j * NP + hp] for j in range(R)], axis=0) for hp in range(NP)]
    means = _group_sums(os_, gsum)
    ds = [o - mu * (1.0 / 64.0) for o, mu in zip(os_, means)]
    vars_ = _group_sums([d * d for d in ds], gsum)
    for hp, sl in enumerate(pairs):
        on = ds[hp] * lax.rsqrt(vars_[hp] * (1.0 / 64.0) + LNX_EPS) * lnw[:, sl] + lnb[:, sl]
        yv = ((on + sums[NP + hp] * v[:, sl]) * g[:, sl]).astype(y_ref.dtype)
        for j in range(R):
            y_ref[j, :, sl] = yv[rows(j)]

    @pl.when(c == pl.num_programs(1) - 1)
    def _():
        for j in range(R):
            for s in range(n_seq):
                for hp in range(NP):
                    wkvout_ref[j * n_seq + s, 2 * hp] = wkv_sc[j, hp, s * LANES:s * LANES + 64, 0:64]
                    wkvout_ref[j * n_seq + s, 2 * hp + 1] = wkv_sc[j, hp, s * LANES + 64:(s + 1) * LANES, 64:128]
                for h in range(HGRN_HEADS):
                    hgout_ref[j * n_seq + s, h] = hg_sc[j, h, blk(s), :].T


def _out_body(x_ref, ym_ref, gt1_ref, sh2_ref, sc2_ref, gt2_ref, n2_ref, nf_ref,
              wo_ref, wu_ref, wd_ref, o_ref):
    y = jnp.dot(ym_ref[...], wo_ref[...], preferred_element_type=F32)
    x1 = x_ref[...] + gt1_ref[...] * y
    h = (_rms(x1, n2_ref[...]) * (1.0 + sc2_ref[...]) + sh2_ref[...]).astype(BF16)
    acc = jnp.zeros_like(x1)
    FC = 1024
    for j in range(D_FF // FC):
        u = jnp.dot(h, wu_ref[:, j * FC:(j + 1) * FC], preferred_element_type=F32)
        u = jnp.square(jnp.maximum(u, 0.0)).astype(BF16)
        acc = acc + jnp.dot(u, wd_ref[j * FC:(j + 1) * FC, :], preferred_element_type=F32)
    x2 = x1 + gt2_ref[...] * acc
    o_ref[...] = _rms(x2, nf_ref[...])


def _params(sem):
    return pltpu.CompilerParams(dimension_semantics=sem, vmem_limit_bytes=VMEM_LIMIT)


def _full(shape):
    return pl.BlockSpec(shape, lambda *_: (0,) * len(shape))


def _ada(c_all, w_ada, b_ada):
    n = c_all.shape[0]
    TN = 1024
    return pl.pallas_call(
        _ada_body,
        grid=(w_ada.shape[1] // TN,),
        in_specs=[pl.BlockSpec((n, D_MODEL), lambda j: (0, 0)),
                  pl.BlockSpec((D_MODEL, TN), lambda j: (0, j)),
                  pl.BlockSpec((1, TN), lambda j: (0, j))],
        out_specs=pl.BlockSpec((n, TN), lambda j: (0, j)),
        out_shape=jax.ShapeDtypeStruct((n, w_ada.shape[1]), F32),
        compiler_params=_params(("arbitrary",)),
        name="ada",
    )(c_all, w_ada, b_ada)


def _mod_specs(mod, cols, TM, rows_per_mod):
    if mod.ndim == 3:
        return [pl.BlockSpec((None, 1, D_MODEL), lambda i, c=c: (i * TM // rows_per_mod, 0, c)) for c in cols]
    return [pl.BlockSpec((TM, D_MODEL), lambda i, c=c: (i, c)) for c in cols]


def _inproj(x, mod, norm1, w_in, TM, rows_per_mod):
    M = x.shape[0]
    return pl.pallas_call(
        _inproj_body,
        grid=(M // TM,),
        in_specs=[pl.BlockSpec((TM, D_MODEL), lambda i: (i, 0))]
                 + _mod_specs(mod, (0, 1), TM, rows_per_mod)
                 + [_full((1, D_MODEL)), _full((D_MODEL, IN_WIDTH))],
        out_specs=pl.BlockSpec((TM, IN_WIDTH), lambda i: (i, 0)),
        out_shape=jax.ShapeDtypeStruct((M, IN_WIDTH), F32),
        compiler_params=_params(("arbitrary",)),
        name="inproj",
    )(x, mod, mod, norm1, w_in)


def _rec(p, shinit, states, small, wcomb, wgate, Ls, n_batch, R):
    Gt, C = shinit.shape[0], CHUNK
    NC = p.shape[1] // C
    n_seq = C // Ls
    wkv_spec = pl.BlockSpec((R * n_seq, RWKV_HEADS, 64, 64), lambda g, c: (g, 0, 0, 0))
    hg_spec = pl.BlockSpec((R * n_seq, HGRN_HEADS, LANES, LANES), lambda g, c: (g, 0, 0, 0))
    sh_spec = pl.BlockSpec((R, C, SHIFT_WIDTH), lambda g, c: (g, 0, 0))
    st_in = [] if states is None else list(states)
    st_specs = [] if states is None else [wkv_spec, hg_spec]
    return pl.pallas_call(
        functools.partial(_rec_body, Ls, states is None, R),
        grid=(Gt // R, NC),
        in_specs=[pl.BlockSpec((R, C, IN_WIDTH), lambda g, c: (g, c, 0)), sh_spec] + st_specs
                 + [_full(s.shape) for s in small] + [_full(wcomb.shape), _full(wgate.shape)],
        out_specs=[pl.BlockSpec((R, C, D_MODEL), lambda g, c: (g, c, 0)), sh_spec, wkv_spec, hg_spec],
        out_shape=[jax.ShapeDtypeStruct((Gt, NC * C, D_MODEL), BF16),
                   jax.ShapeDtypeStruct(shinit.shape, F32),
                   jax.ShapeDtypeStruct((n_batch, RWKV_HEADS, 64, 64), F32),
                   jax.ShapeDtypeStruct((n_batch, HGRN_HEADS, LANES, LANES), F32)],
        scratch_shapes=[pltpu.VMEM((R, RWKV_HEADS // 2, n_seq * LANES, LANES), F32),
                        pltpu.VMEM((R, HGRN_HEADS, n_seq * LANES, LANES), F32)],
        compiler_params=_params(("arbitrary", "arbitrary")),
        name="rec",
    )(p, shinit, *st_in, *small, wcomb, wgate)


def _out(x, ym, mod, norm2, norm_f, w_out, w_up, w_down, TM, rows_per_mod):
    M = x.shape[0]
    row = lambda i: (i, 0)
    return pl.pallas_call(
        _out_body,
        grid=(M // TM,),
        in_specs=[pl.BlockSpec((TM, D_MODEL), row), pl.BlockSpec((TM, D_MODEL), row)]
                 + _mod_specs(mod, (2, 3, 4, 5), TM, rows_per_mod)
                 + [_full((1, D_MODEL)), _full((1, D_MODEL)),
                    _full(w_out.shape), _full(w_up.shape), _full(w_down.shape)],
        out_specs=pl.BlockSpec((TM, D_MODEL), row),
        out_shape=jax.ShapeDtypeStruct((M, D_MODEL), F32),
        compiler_params=_params(("arbitrary",)),
        name="outmlp",
    )(x, ym, mod, mod, mod, mod, norm2, norm_f, w_out, w_up, w_down)


def kernel(x_prompt, x_sample, c_prompt, c_sample, state_shift, state_wkv, state_hgrn, norm1, norm2, norm_f, w_ada, b_ada, w_in, mu_shift, w0, w_decay_up, a0, w_aaa_up, w_gate_up, k_k, k_a, r_k, lnx_w, lnx_b, hgrn_lb, hgrn_gnorm, w_out, w_up, w_down):
    BP, TP, _ = x_prompt.shape
    BS, TS, _ = x_sample.shape
    l = 0
    row = lambda t: t.reshape(1, -1)

    mod = _ada(jnp.concatenate([c_prompt, c_sample], axis=0), w_ada[l], row(b_ada[l]))
    mod_p = mod[:BP].reshape(BP, 1, 6 * D_MODEL)
    mod_s = jnp.repeat(mod[BP:], TS, axis=0)

    w_in_b = w_in[l].astype(BF16)
    w_out_b = w_out[l].astype(BF16)
    w_up_b = w_up[l].astype(BF16)
    w_down_b = w_down[l].astype(BF16)
    zer = jnp.zeros((64, RWKV_WIDTH), F32)
    wcomb = jnp.concatenate([jnp.concatenate([w_decay_up[l], zer], axis=1),
                             jnp.concatenate([zer, w_aaa_up[l]], axis=1)], axis=0).astype(BF16)
    wgate = w_gate_up[l].astype(BF16)
    small = [row(mu_shift[l]), row(w0[l]), row(a0[l]), row(k_k[l]), row(k_a[l]), row(r_k[l]),
             row(lnx_w[l]), row(lnx_b[l]), hgrn_lb, row(hgrn_gnorm[l])]
    n1, n2, nf = row(norm1[l]), row(norm2[l]), row(norm_f)

    def trunk(x2d, modx, rows_per_mod, shinit, states, Ls, n_batch, TM, R):
        Gt = shinit.shape[0]
        p = _inproj(x2d, modx, n1, w_in_b, TM, rows_per_mod)
        ym, shout, wkv_o, hg_o = _rec(p.reshape(Gt, -1, IN_WIDTH), shinit, states, small, wcomb, wgate,
                                      Ls, n_batch, R)
        y = _out(x2d, ym.reshape(-1, D_MODEL), modx, n2, nf, w_out_b, w_up_b, w_down_b, TM, rows_per_mod)
        return y, shout, wkv_o, hg_o

    yp, shp, wkvp, hgp = trunk(x_prompt.reshape(BP * TP, D_MODEL), mod_p, TP,
                               jnp.zeros((BP, CHUNK, SHIFT_WIDTH), F32), None, CHUNK, BP, 256, PROMPT_GROUPS_PER_STEP)
    y_prompt = yp.reshape(BP, TP, D_MODEL)
    shift_p = shp[:, CHUNK - 1][None]
    wkv_p = wkvp[None]
    hgrn_p = hgp[None]

    n_seq = CHUNK // TS
    GS = BS // n_seq
    shinit_s = jnp.zeros((GS, n_seq, TS, SHIFT_WIDTH), F32).at[:, :, 0].set(
        state_shift[l].reshape(GS, n_seq, SHIFT_WIDTH)).reshape(GS, CHUNK, SHIFT_WIDTH)
    ys, shs, wkvs, hgs = trunk(x_sample.reshape(BS * TS, D_MODEL), mod_s, 1, shinit_s,
                               (state_wkv[l], state_hgrn[l]), TS, BS, 256, 1)
    y_sample = ys.reshape(BS, TS, D_MODEL)
    shift_s = shs.reshape(GS, n_seq, TS, SHIFT_WIDTH)[:, :, TS - 1].reshape(BS, SHIFT_WIDTH)[None]
    wkv_s = wkvs[None]
    hgrn_s = hgs[None]

    return (y_prompt, y_sample, shift_p, wkv_p, hgrn_p, shift_s, wkv_s, hgrn_s)
```

```python
import functools

import jax
import jax.numpy as jnp
from jax import lax
from jax.experimental import pallas as pl
from jax.experimental.pallas import tpu as pltpu

F32 = jnp.float32
BF16 = jnp.bfloat16

D_MODEL = 1024
RWKV_HEADS = 8
RWKV_WIDTH = 512
HGRN_HEADS = 4
HGRN_WIDTH = 512
SHIFT_WIDTH = 1792
IN_WIDTH = 3840
D_FF = 4096
NORM_EPS = 1e-6
LNX_EPS = 64e-5

LANES = 128
HEAD_PAIR = 128
CHUNK = 64
SUB = 16
PROMPT_GROUPS_PER_STEP = 4
INPROJ_TILE = 256
VMEM_LIMIT = 58 * 1024 * 1024


def _dot(a, b):
    return jnp.dot(a.astype(BF16), b.astype(BF16), preferred_element_type=F32)


def _dot_nt(a, b):
    return lax.dot_general(a.astype(BF16), b.astype(BF16), (((1,), (1,)), ((), ())),
                           preferred_element_type=F32)


def _dot_2pass_l(m, x):
    mb = m.astype(BF16)
    hi = x.astype(BF16)
    lo = (x - hi.astype(F32)).astype(BF16)
    return jnp.dot(mb, hi, preferred_element_type=F32) + jnp.dot(mb, lo, preferred_element_type=F32)


def _group_sums(xs, gmat):
    n, rows = len(xs), xs[0].shape[0]
    P = _dot(jnp.concatenate(xs, axis=0), gmat)
    return [P[i * rows:(i + 1) * rows] for i in range(n)]


def _iota(shape, dim):
    return lax.broadcasted_iota(jnp.int32, shape, dim)


def _shr(x, n):
    return lax.shift_right_logical(x, jnp.int32(n.bit_length() - 1))


def _sigmoid(x):
    return 1.0 / (1.0 + jnp.exp(-x))


def _silu(x):
    return x * _sigmoid(x)


def _softplus(x):
    return jnp.maximum(x, 0.0) + jnp.log(1.0 + jnp.exp(-jnp.abs(x)))


def _rms(x, gain):
    return x * lax.rsqrt(jnp.mean(x * x, axis=-1, keepdims=True) + NORM_EPS) * gain


def _make_masks(C, Ls):
    row = _iota((C, C), 0)
    col = _iota((C, C), 1)
    seq = lambda t: _shr(t, Ls)
    blk = lambda t: _shr(t, SUB)
    same = seq(row) == seq(col)
    R = _iota((2 * C, 2 * C), 0)
    Q = _iota((2 * C, 2 * C), 1)
    tr = R & (C - 1)
    tq = Q & (C - 1)
    same2 = seq(tr) == seq(tq)
    lvl = []
    s = 1
    while s < Ls:
        lvl.append((_shr(R, 2 * s) == _shr(Q, 2 * s)) & ((R & (2 * s - 1)) >= s) & ((Q & (2 * s - 1)) < s))
        s *= 2
    return dict(
        row=row, col=col, same=same,
        mi=same & (col <= row),
        sameblk=blk(row) == blk(col),
        eye2=(R == Q).astype(F32),
        lvl=lvl,
        ms_hi2=same2 & (tq < tr) & (Q >= C),
        mi22=same2 & (tq <= tr),
    )


def _tri_inverse(G, mk):
    lvl = mk['lvl']
    D = mk['eye2'] + jnp.where(lvl[0], G, 0.0)
    for m in lvl[1:]:
        LD = _dot(jnp.where(m, G, 0.0), D)
        yield
        D = D + _dot(D, LD)
        yield
    return D


def _lockstep(streams):
    streams = list(streams)
    out = [None] * len(streams)
    alive = list(range(len(streams)))
    while alive:
        for i in list(alive):
            try:
                next(streams[i])
            except StopIteration as stop:
                out[i] = stop.value
                alive.remove(i)
    return out


def _select_seq(full, n_seq, sid_rows):
    if n_seq == 1:
        return full
    acc = jnp.where(sid_rows == 0, full[:, 0:LANES], 0.0)
    for s in range(1, n_seq):
        acc = acc + jnp.where(sid_rows == s, full[:, s * LANES:(s + 1) * LANES], 0.0)
    return acc


def _expand_seq(x, n_seq, sid_rows):
    if n_seq == 1:
        return x
    return jnp.concatenate([jnp.where(sid_rows == s, x, 0.0) for s in range(n_seq)], axis=1)


def _wkv_pair_chunk(r, lw, cw, wl, k, v, kk, a, S, mk, Ls):
    C = r.shape[0]
    n_seq = C // Ls
    lo_half = _iota((C, HEAD_PAIR), 1) < 64
    own = (_iota((2 * C, HEAD_PAIR), 1) < 64) == (_iota((2 * C, HEAD_PAIR), 0) < C)
    stack = lambda t: jnp.where(own, jnp.concatenate([t, t], axis=0), 0.0)
    e_neg = jnp.exp(-cw)
    at = -kk * jnp.exp(cw - lw)
    bt = kk * a * e_neg
    kt = k * e_neg
    rt = r * jnp.exp(cw)
    Y = jnp.concatenate([bt, kt], axis=0)
    ga = _dot_nt(stack(at), Y)
    first_cols = _iota((2 * C, 2 * C), 1) < C
    top_rows = _iota((2 * C, 2 * C), 0) < C
    G = (jnp.where(first_cols & top_rows, ga, 0.0)
         + pltpu.roll(jnp.where(first_cols & jnp.logical_not(top_rows), ga, 0.0), C, 1))
    gr = _dot_nt(stack(rt), Y)
    sid2 = _shr(_iota((2 * C, LANES), 0) & (C - 1), Ls)
    XO = _select_seq(_dot_nt(jnp.concatenate([at, rt], axis=0), S), n_seq, sid2)
    yield
    Vz = jnp.concatenate([jnp.zeros_like(v), v], axis=0)
    xk = _dot(jnp.where(mk['ms_hi2'], ga, 0.0), Vz)
    D = yield from _tri_inverse(G, mk)
    X = XO[:C] + jnp.where(lo_half, xk[:C], xk[C:])
    DX = _dot(D, jnp.concatenate([X, X], axis=0))
    yield
    U = jnp.where(lo_half, DX[:C], DX[C:])
    Z = jnp.concatenate([U, v], axis=0)
    oz = _dot(jnp.where(mk['mi22'], gr, 0.0), Z)
    upd = _dot(Z.T, _expand_seq(Y, n_seq, sid2))
    yield
    o = XO[C:] + jnp.where(lo_half, oz[:C], oz[C:])
    bd = _shr(_iota((HEAD_PAIR, HEAD_PAIR), 0), 64) == _shr(_iota((HEAD_PAIR, HEAD_PAIR), 1), 64)
    new = []
    for s in range(n_seq):
        Ss = S[s * LANES:(s + 1) * LANES, :]
        new.append(jnp.where(bd, (Ss + upd[:, s * LANES:(s + 1) * LANES]) * wl[s * Ls:s * Ls + 1, :], 0.0))
    return o, new


def _hgrn_head_chunk(q, kf, b, blast, iv, ST, mk, Ls):
    C = q.shape[0]
    assert Ls == C or Ls <= SUB
    n_seq = C // Ls
    sid = _shr(_iota((C, LANES), 0), Ls)
    if Ls > SUB:
        rows_of = lambda off: jnp.concatenate(
            [jnp.broadcast_to(b[SUB * j + off:SUB * j + off + 1, :], (SUB, LANES)) for j in range(C // SUB)], axis=0)
        bmid = rows_of(SUB // 2 - 1)
        bend = rows_of(SUB - 1)
    else:
        bmid = 0.0
    o = _select_seq(_dot_nt(q * jnp.exp(b), ST), n_seq, sid)
    att = jnp.where(mk['mi'] & mk['sameblk'],
                    _dot_nt(q * jnp.exp(b - bmid), kf * jnp.exp(bmid - b)), 0.0)
    if Ls > SUB:
        ko = kf * jnp.exp(jnp.minimum(bend - b, 0.0))
        rb = _shr(mk['row'], SUB)
        cb = _shr(mk['col'], SUB)
        for j in range(C // SUB - 1):
            ref = b[SUB * j + SUB - 1:SUB * j + SUB, :]
            qo = q * jnp.exp(jnp.minimum(b - ref, 0.0))
            att = att + jnp.where(mk['same'] & (cb == j) & (rb > j), _dot_nt(qo, ko), 0.0)
    ke = kf * jnp.exp(blast - b)
    upd = _dot(iv.T, _expand_seq(ke, n_seq, sid))
    yield
    o = o + _dot(att, iv)
    yield
    dec = jnp.exp(blast)
    new = []
    for s in range(n_seq):
        new.append(ST[s * LANES:(s + 1) * LANES, :] * dec[s * Ls:s * Ls + 1, :]
                   + upd[:, s * LANES:(s + 1) * LANES])
    return o, new


def _ada_body(c_ref, w_ref, b_ref, o_ref):
    o_ref[...] = _dot(_silu(c_ref[...]), w_ref[...]) + b_ref[...]


def _inproj_body(x_ref, sh_ref, sc_ref, n1_ref, w_ref, p_ref):
    h = _rms(x_ref[...], n1_ref[...]) * (1.0 + sc_ref[...]) + sh_ref[...]
    p_ref[...] = jnp.dot(h.astype(BF16), w_ref[...], preferred_element_type=F32)


def _rec_body(Ls, single_seq, R, fuse_in, *refs):
    refs = list(refs)
    if fuse_in:
        xfirst_ref, xnext_ref, sh1_ref, sc1_ref, n1_ref, win_ref = refs[:6]
        del refs[:6]
        p_ref = None
    else:
        p_ref = refs.pop(0)
    shinit_ref = refs.pop(0)
    if single_seq:
        wkvin_ref = hgin_ref = None
    else:
        wkvin_ref, hgin_ref = refs[:2]
        del refs[:2]
    (mu_ref, w0_ref, a0_ref, kk_ref, ka_ref, rk_ref, lnw_ref, lnb_ref, lb_ref, gn_ref, wcomb_ref, wgate_ref,
     y_ref, shout_ref, wkvout_ref, hgout_ref, wkv_sc, hg_sc) = refs[:18]
    p_sc, pn_sc = refs[18:20] if fuse_in else (None, None)
    C = CHUNK
    n_seq = C // Ls
    NP = RWKV_HEADS // 2
    c = pl.program_id(1)
    rows = lambda j: slice(j * C, (j + 1) * C)
    blk = lambda s: slice(s * LANES, (s + 1) * LANES)

    @pl.when(c == 0)
    def _():
        shout_ref[...] = jnp.zeros_like(shout_ref)
        wkv_sc[...] = jnp.zeros_like(wkv_sc)
        if single_seq:
            hg_sc[...] = jnp.zeros_like(hg_sc)
        else:
            for j in range(R):
                for s in range(n_seq):
                    for hp in range(NP):
                        wkv_sc[j, hp, s * LANES:s * LANES + 64, 0:64] = wkvin_ref[j * n_seq + s, 2 * hp]
                        wkv_sc[j, hp, s * LANES + 64:(s + 1) * LANES, 64:128] = wkvin_ref[j * n_seq + s, 2 * hp + 1]
                    for h in range(HGRN_HEADS):
                        hg_sc[j, h, blk(s), :] = hgin_ref[j * n_seq + s, h].T

    def projected_rows(x_ref):
        hs = [_rms(x_ref[j], n1_ref[...]) * (1.0 + sc1_ref[j]) + sh1_ref[j] for j in range(R)]
        return jnp.concatenate(hs, axis=0).astype(BF16)

    if fuse_in:
        @pl.when(c == 0)
        def _():
            p_sc[...] = jnp.dot(projected_rows(xfirst_ref), win_ref[...], preferred_element_type=F32)

        pcols = lambda j, lo, hi: p_sc[j * C:(j + 1) * C, lo:hi]

        def inproj_tiles():
            hn = projected_rows(xnext_ref)
            for t in range(IN_WIDTH // INPROJ_TILE):
                cols = slice(t * INPROJ_TILE, (t + 1) * INPROJ_TILE)
                pn_sc[:, cols] = jnp.dot(hn, win_ref[:, cols], preferred_element_type=F32)
                yield

        tiles = inproj_tiles()
    else:
        pcols = lambda j, lo, hi: p_ref[j, :, lo:hi]
        tiles = iter(())

    def emit_tiles(n):
        for _ in range(n):
            next(tiles, None)

    mk = _make_masks(C, Ls)
    emit_tiles(3)

    rowv = _iota((C, SHIFT_WIDTH), 0)
    if single_seq:
        start = rowv == jnp.where(c == 0, 0, -1)
    else:
        start = (rowv & (Ls - 1)) == 0
    xs = []
    for j in range(R):
        p_rw = pcols(j, 0, SHIFT_WIDTH)
        p_prev = jnp.where(rowv == 0, shout_ref[j, C - 1:C, :], pltpu.roll(p_rw, 1, 0))
        p_prev = jnp.where(start, shinit_ref[j], p_prev)
        shout_ref[j] = p_rw
        xs.append(p_rw + (p_prev - p_rw) * mu_ref[...])
    x = jnp.concatenate(xs, axis=0)
    cat = lambda lo, hi: jnp.concatenate([pcols(j, lo, hi) for j in range(R)], axis=0)

    r = x[:, 0:512]
    k = x[:, 512:1024]
    v = x[:, 1024:1536]
    wa = x[:, 1536:1664]
    gd = x[:, 1664:1792]
    lane = _iota((R * C, LANES), 1)
    da = _dot(jnp.where(lane < 64, jnp.tanh(wa), wa), wcomb_ref[...])
    w_log = -_softplus(-(w0_ref[...] + da[:, :512])) - 0.5
    lw = -jnp.exp(w_log)
    a = _sigmoid(a0_ref[...] + da[:, 512:])
    g = _dot(_sigmoid(gd), wgate_ref[...])
    emit_tiles(4)
    kkr = k * kk_ref[...]
    kmod = k * (1.0 + (a - 1.0) * ka_ref[...])
    rkr = r * kmod * rk_ref[...]
    lnw = lnw_ref[...]
    lnb = lnb_ref[...]

    lbp = lb_ref[...]
    m = jnp.maximum(lbp[0:1, :], lbp[1:2, :])
    e0 = jnp.exp(lbp[0:1, :] - m)
    e1 = jnp.exp(lbp[1:2, :] - m)
    lb = e0 / (e0 + e1)
    q = _silu(cat(1792, 2304))
    f = lb + (1.0 - lb) * _sigmoid(cat(2304, 2816))
    kf = 1.0 - f
    iv = cat(2816, 3328)
    og = cat(3328, 3840)

    logs = jnp.concatenate([lw, jnp.log(f)], axis=1)
    cums, tots = [], []
    for j in range(R):
        if single_seq:
            cum = _dot_2pass_l(mk['mi'].astype(F32), logs[rows(j)])
            tot = cum[C - 1:C, :]
        else:
            both = _dot_2pass_l(jnp.concatenate([mk['mi'], mk['same']], axis=0).astype(F32), logs[rows(j)])
            cum, tot = both[:C], both[C:]
        cums.append(cum)
        tots.append(tot)

    emit_tiles(4)
    gsum = (_shr(_iota((HEAD_PAIR, HEAD_PAIR), 0), 64) == _shr(_iota((HEAD_PAIR, HEAD_PAIR), 1), 64)).astype(F32)
    pairs = [slice(hp * HEAD_PAIR, (hp + 1) * HEAD_PAIR) for hp in range(NP)]
    sums = _group_sums([kkr[:, sl] * kkr[:, sl] for sl in pairs] + [rkr[:, sl] for sl in pairs], gsum)

    def wkv_stream(j, hp):
        sl = pairs[hp]
        kk_s = kkr[rows(j), sl] / jnp.maximum(jnp.sqrt(sums[hp][rows(j)]), 1e-12)
        o, new = yield from _wkv_pair_chunk(r[rows(j), sl], lw[rows(j), sl], cums[j][:, sl],
                                            jnp.exp(tots[j][:, sl]), kmod[rows(j), sl], v[rows(j), sl],
                                            kk_s, a[rows(j), sl], wkv_sc[j, hp], mk, Ls)
        for s in range(n_seq):
            wkv_sc[j, hp, blk(s), :] = new[s]
        return o

    def hgrn_stream(j, h):
        sl = slice(RWKV_WIDTH + h * LANES, RWKV_WIDTH + (h + 1) * LANES)
        hs = blk(h)
        o, new = yield from _hgrn_head_chunk(q[rows(j), hs], kf[rows(j), hs], cums[j][:, sl], tots[j][:, sl],
                                             iv[rows(j), hs], hg_sc[j, h], mk, Ls)
        for s in range(n_seq):
            hg_sc[j, h, blk(s), :] = new[s]
        on = _rms(o, gn_ref[...])
        y_ref[j, :, sl] = (on * _silu(og[rows(j), hs])).astype(y_ref.dtype)

    emit_tiles(IN_WIDTH // INPROJ_TILE)
    outs = _lockstep([wkv_stream(j, hp) for j in range(R) for hp in range(NP)]
                     + [hgrn_stream(j, h) for j in range(R) for h in range(HGRN_HEADS)])

    os_ = [jnp.concatenate([outs[j * NP + hp] for j in range(R)], axis=0) for hp in range(NP)]
    means = _group_sums(os_, gsum)
    ds = [o - mu * (1.0 / 64.0) for o, mu in zip(os_, means)]
    vars_ = _group_sums([d * d for d in ds], gsum)
    for hp, sl in enumerate(pairs):
        on = ds[hp] * lax.rsqrt(vars_[hp] * (1.0 / 64.0) + LNX_EPS) * lnw[:, sl] + lnb[:, sl]
        yv = ((on + sums[NP + hp] * v[:, sl]) * g[:, sl]).astype(y_ref.dtype)
        for j in range(R):
            y_ref[j, :, sl] = yv[rows(j)]

    if fuse_in:
        p_sc[...] = pn_sc[...]

    @pl.when(c == pl.num_programs(1) - 1)
    def _():
        for j in range(R):
            for s in range(n_seq):
                for hp in range(NP):
                    wkvout_ref[j * n_seq + s, 2 * hp] = wkv_sc[j, hp, s * LANES:s * LANES + 64, 0:64]
                    wkvout_ref[j * n_seq + s, 2 * hp + 1] = wkv_sc[j, hp, s * LANES + 64:(s + 1) * LANES, 64:128]
                for h in range(HGRN_HEADS):
                    hgout_ref[j * n_seq + s, h] = hg_sc[j, h, blk(s), :].T


def _out_body(x_ref, ym_ref, gt1_ref, sh2_ref, sc2_ref, gt2_ref, n2_ref, nf_ref,
              wo_ref, wu_ref, wd_ref, o_ref):
    y = jnp.dot(ym_ref[...], wo_ref[...], preferred_element_type=F32)
    x1 = x_ref[...] + gt1_ref[...] * y
    h = (_rms(x1, n2_ref[...]) * (1.0 + sc2_ref[...]) + sh2_ref[...]).astype(BF16)
    acc = jnp.zeros_like(x1)
    FC = 1024
    for j in range(D_FF // FC):
        u = jnp.dot(h, wu_ref[:, j * FC:(j + 1) * FC], preferred_element_type=F32)
        u = jnp.square(jnp.maximum(u, 0.0)).astype(BF16)
        acc = acc + jnp.dot(u, wd_ref[j * FC:(j + 1) * FC, :], preferred_element_type=F32)
    x2 = x1 + gt2_ref[...] * acc
    o_ref[...] = _rms(x2, nf_ref[...])


def _params(sem):
    return pltpu.CompilerParams(dimension_semantics=sem, vmem_limit_bytes=VMEM_LIMIT)


def _full(shape):
    return pl.BlockSpec(shape, lambda *_: (0,) * len(shape))


def _ada(c_all, w_ada, b_ada):
    n = c_all.shape[0]
    TN = 1024
    return pl.pallas_call(
        _ada_body,
        grid=(w_ada.shape[1] // TN,),
        in_specs=[pl.BlockSpec((n, D_MODEL), lambda j: (0, 0)),
                  pl.BlockSpec((D_MODEL, TN), lambda j: (0, j)),
                  pl.BlockSpec((1, TN), lambda j: (0, j))],
        out_specs=pl.BlockSpec((n, TN), lambda j: (0, j)),
        out_shape=jax.ShapeDtypeStruct((n, w_ada.shape[1]), F32),
        compiler_params=_params(("arbitrary",)),
        name="ada",
    )(c_all, w_ada, b_ada)


def _mod_specs(mod, cols, TM, rows_per_mod):
    if mod.ndim == 3:
        return [pl.BlockSpec((None, 1, D_MODEL), lambda i, c=c: (i * TM // rows_per_mod, 0, c)) for c in cols]
    return [pl.BlockSpec((TM, D_MODEL), lambda i, c=c: (i, c)) for c in cols]


def _inproj(x, mod, norm1, w_in, TM, rows_per_mod):
    M = x.shape[0]
    return pl.pallas_call(
        _inproj_body,
        grid=(M // TM,),
        in_specs=[pl.BlockSpec((TM, D_MODEL), lambda i: (i, 0))]
                 + _mod_specs(mod, (0, 1), TM, rows_per_mod)
                 + [_full((1, D_MODEL)), _full((D_MODEL, IN_WIDTH))],
        out_specs=pl.BlockSpec((TM, IN_WIDTH), lambda i: (i, 0)),
        out_shape=jax.ShapeDtypeStruct((M, IN_WIDTH), F32),
        compiler_params=_params(("arbitrary",)),
        name="inproj",
    )(x, mod, mod, norm1, w_in)


def _rec(p, shinit, states, small, wcomb, wgate, Ls, n_batch, R, inproj=None):
    Gt, C = shinit.shape[0], CHUNK
    rows_total = (p if inproj is None else inproj[0]).shape[1]
    NC = rows_total // C
    n_seq = C // Ls
    wkv_spec = pl.BlockSpec((R * n_seq, RWKV_HEADS, 64, 64), lambda g, c: (g, 0, 0, 0))
    hg_spec = pl.BlockSpec((R * n_seq, HGRN_HEADS, LANES, LANES), lambda g, c: (g, 0, 0, 0))
    sh_spec = pl.BlockSpec((R, C, SHIFT_WIDTH), lambda g, c: (g, 0, 0))
    st_in = [] if states is None else list(states)
    st_specs = [] if states is None else [wkv_spec, hg_spec]
    scratch = [pltpu.VMEM((R, RWKV_HEADS // 2, n_seq * LANES, LANES), F32),
               pltpu.VMEM((R, HGRN_HEADS, n_seq * LANES, LANES), F32)]
    if inproj is None:
        lead_in = [p]
        lead_specs = [pl.BlockSpec((R, C, IN_WIDTH), lambda g, c: (g, c, 0))]
    else:
        x, mod, norm1, w_in = inproj
        lead_in = [x, x, mod, mod, norm1, w_in]
        lead_specs = [pl.BlockSpec((R, C, D_MODEL), lambda g, c: (g, 0, 0)),
                      pl.BlockSpec((R, C, D_MODEL), lambda g, c: (g, jnp.minimum(c + 1, NC - 1), 0)),
                      pl.BlockSpec((R, 1, D_MODEL), lambda g, c: (g, 0, 0)),
                      pl.BlockSpec((R, 1, D_MODEL), lambda g, c: (g, 0, 1)),
                      _full(norm1.shape), _full(w_in.shape)]
        scratch += [pltpu.VMEM((R * C, IN_WIDTH), F32), pltpu.VMEM((R * C, IN_WIDTH), F32)]
    return pl.pallas_call(
        functools.partial(_rec_body, Ls, states is None, R, inproj is not None),
        grid=(Gt // R, NC),
        in_specs=lead_specs + [sh_spec] + st_specs
                 + [_full(s.shape) for s in small] + [_full(wcomb.shape), _full(wgate.shape)],
        out_specs=[pl.BlockSpec((R, C, D_MODEL), lambda g, c: (g, c, 0)), sh_spec, wkv_spec, hg_spec],
        out_shape=[jax.ShapeDtypeStruct((Gt, NC * C, D_MODEL), BF16),
                   jax.ShapeDtypeStruct(shinit.shape, F32),
                   jax.ShapeDtypeStruct((n_batch, RWKV_HEADS, 64, 64), F32),
                   jax.ShapeDtypeStruct((n_batch, HGRN_HEADS, LANES, LANES), F32)],
        scratch_shapes=scratch,
        compiler_params=_params(("arbitrary", "arbitrary")),
        name="rec",
    )(*lead_in, shinit, *st_in, *small, wcomb, wgate)


def _out(x, ym, mod, norm2, norm_f, w_out, w_up, w_down, TM, rows_per_mod):
    M = x.shape[0]
    row = lambda i: (i, 0)
    return pl.pallas_call(
        _out_body,
        grid=(M // TM,),
        in_specs=[pl.BlockSpec((TM, D_MODEL), row), pl.BlockSpec((TM, D_MODEL), row)]
                 + _mod_specs(mod, (2, 3, 4, 5), TM, rows_per_mod)
                 + [_full((1, D_MODEL)), _full((1, D_MODEL)),
                    _full(w_out.shape), _full(w_up.shape), _full(w_down.shape)],
        out_specs=pl.BlockSpec((TM, D_MODEL), row),
        out_shape=jax.ShapeDtypeStruct((M, D_MODEL), F32),
        compiler_params=_params(("arbitrary",)),
        name="outmlp",
    )(x, ym, mod, mod, mod, mod, norm2, norm_f, w_out, w_up, w_down)


def kernel(x_prompt, x_sample, c_prompt, c_sample, state_shift, state_wkv, state_hgrn, norm1, norm2, norm_f, w_ada, b_ada, w_in, mu_shift, w0, w_decay_up, a0, w_aaa_up, w_gate_up, k_k, k_a, r_k, lnx_w, lnx_b, hgrn_lb, hgrn_gnorm, w_out, w_up, w_down):
    BP, TP, _ = x_prompt.shape
    BS, TS, _ = x_sample.shape
    l = 0
    row = lambda t: t.reshape(1, -1)

    mod = _ada(jnp.concatenate([c_prompt, c_sample], axis=0), w_ada[l], row(b_ada[l]))
    mod_p = mod[:BP].reshape(BP, 1, 6 * D_MODEL)
    mod_s = jnp.repeat(mod[BP:], TS, axis=0)

    w_in_b = w_in[l].astype(BF16)
    w_out_b = w_out[l].astype(BF16)
    w_up_b = w_up[l].astype(BF16)
    w_down_b = w_down[l].astype(BF16)
    zer = jnp.zeros((64, RWKV_WIDTH), F32)
    wcomb = jnp.concatenate([jnp.concatenate([w_decay_up[l], zer], axis=1),
                             jnp.concatenate([zer, w_aaa_up[l]], axis=1)], axis=0).astype(BF16)
    wgate = w_gate_up[l].astype(BF16)
    small = [row(mu_shift[l]), row(w0[l]), row(a0[l]), row(k_k[l]), row(k_a[l]), row(r_k[l]),
             row(lnx_w[l]), row(lnx_b[l]), hgrn_lb, row(hgrn_gnorm[l])]
    n1, n2, nf = row(norm1[l]), row(norm2[l]), row(norm_f)

    def trunk(x2d, modx, rows_per_mod, shinit, states, Ls, n_batch, TM, R, fuse_in):
        Gt = shinit.shape[0]
        if fuse_in:
            ym, shout, wkv_o, hg_o = _rec(None, shinit, states, small, wcomb, wgate, Ls, n_batch, R,
                                          inproj=(x2d.reshape(Gt, -1, D_MODEL), modx, n1, w_in_b))
        else:
            p = _inproj(x2d, modx, n1, w_in_b, TM, rows_per_mod)
            ym, shout, wkv_o, hg_o = _rec(p.reshape(Gt, -1, IN_WIDTH), shinit, states, small, wcomb, wgate,
                                          Ls, n_batch, R)
        y = _out(x2d, ym.reshape(-1, D_MODEL), modx, n2, nf, w_out_b, w_up_b, w_down_b, TM, rows_per_mod)
        return y, shout, wkv_o, hg_o

    yp, shp, wkvp, hgp = trunk(x_prompt.reshape(BP * TP, D_MODEL), mod_p, TP,
                               jnp.zeros((BP, CHUNK, SHIFT_WIDTH), F32), None, CHUNK, BP, 256, PROMPT_GROUPS_PER_STEP, True)
    y_prompt = yp.reshape(BP, TP, D_MODEL)
    shift_p = shp[:, CHUNK - 1][None]
    wkv_p = wkvp[None]
    hgrn_p = hgp[None]

    n_seq = CHUNK // TS
    GS = BS // n_seq
    shinit_s = jnp.zeros((GS, n_seq, TS, SHIFT_WIDTH), F32).at[:, :, 0].set(
        state_shift[l].reshape(GS, n_seq, SHIFT_WIDTH)).reshape(GS, CHUNK, SHIFT_WIDTH)
    ys, shs, wkvs, hgs = trunk(x_sample.reshape(BS * TS, D_MODEL), mod_s, 1, shinit_s,
                               (state_wkv[l], state_hgrn[l]), TS, BS, 256, 1, False)
    y_sample = ys.reshape(BS, TS, D_MODEL)
    shift_s = shs.reshape(GS, n_seq, TS, SHIFT_WIDTH)[:, :, TS - 1].reshape(BS, SHIFT_WIDTH)[None]
    wkv_s = wkvs[None]
    hgrn_s = hgs[None]

    return (y_prompt, y_sample, shift_p, wkv_p, hgrn_p, shift_s, wkv_s, hgrn_s)
```

```python
import functools

import jax
import jax.numpy as jnp
from jax import lax
from jax.experimental import pallas as pl
from jax.experimental.pallas import tpu as pltpu

F32 = jnp.float32
BF16 = jnp.bfloat16

D_MODEL = 1024
RWKV_HEADS = 8
RWKV_WIDTH = 512
HGRN_HEADS = 4
HGRN_WIDTH = 512
SHIFT_WIDTH = 1792
IN_WIDTH = 3840
D_FF = 4096
NORM_EPS = 1e-6
LNX_EPS = 64e-5

LANES = 128
HEAD_PAIR = 128
CHUNK = 64
SUB = 16
PROMPT_GROUPS_PER_STEP = 4
INPROJ_TILE = 256
VMEM_LIMIT = 58 * 1024 * 1024


def _dot(a, b):
    return jnp.dot(a.astype(BF16), b.astype(BF16), preferred_element_type=F32)


def _dot_nt(a, b):
    return lax.dot_general(a.astype(BF16), b.astype(BF16), (((1,), (1,)), ((), ())),
                           preferred_element_type=F32)


def _dot_2pass_l(m, x):
    mb = m.astype(BF16)
    hi = x.astype(BF16)
    lo = (x - hi.astype(F32)).astype(BF16)
    return jnp.dot(mb, hi, preferred_element_type=F32) + jnp.dot(mb, lo, preferred_element_type=F32)


def _repeat_rows(m, rep):
    if rep == 1:
        return m
    n = m.shape[0]
    sel = (_shr(_iota((n * rep, n), 0), rep) == _iota((n * rep, n), 1)).astype(BF16)
    hi = m.astype(BF16)
    r1 = m - hi.astype(F32)
    mid = r1.astype(BF16)
    lo = (r1 - mid.astype(F32)).astype(BF16)
    d = lambda part: jnp.dot(sel, part, preferred_element_type=F32)
    return d(hi) + d(mid) + d(lo)


def _group_sums(xs, gmat):
    n, rows = len(xs), xs[0].shape[0]
    P = _dot(jnp.concatenate(xs, axis=0), gmat)
    return [P[i * rows:(i + 1) * rows] for i in range(n)]


def _iota(shape, dim):
    return lax.broadcasted_iota(jnp.int32, shape, dim)


def _shr(x, n):
    return lax.shift_right_logical(x, jnp.int32(n.bit_length() - 1))


def _sigmoid(x):
    return 1.0 / (1.0 + jnp.exp(-x))


def _silu(x):
    return x * _sigmoid(x)


def _softplus(x):
    return jnp.maximum(x, 0.0) + jnp.log(1.0 + jnp.exp(-jnp.abs(x)))


def _rms(x, gain):
    return x * lax.rsqrt(jnp.mean(x * x, axis=-1, keepdims=True) + NORM_EPS) * gain


def _make_masks(C, Ls):
    row = _iota((C, C), 0)
    col = _iota((C, C), 1)
    seq = lambda t: _shr(t, Ls)
    blk = lambda t: _shr(t, SUB)
    same = seq(row) == seq(col)
    R = _iota((2 * C, 2 * C), 0)
    Q = _iota((2 * C, 2 * C), 1)
    tr = R & (C - 1)
    tq = Q & (C - 1)
    same2 = seq(tr) == seq(tq)
    lvl = []
    s = 1
    while s < Ls:
        lvl.append((_shr(R, 2 * s) == _shr(Q, 2 * s)) & ((R & (2 * s - 1)) >= s) & ((Q & (2 * s - 1)) < s))
        s *= 2
    return dict(
        row=row, col=col, same=same,
        mi=same & (col <= row),
        sameblk=blk(row) == blk(col),
        eye2=(R == Q).astype(F32),
        lvl=lvl,
        ms_hi2=same2 & (tq < tr) & (Q >= C),
        mi22=same2 & (tq <= tr),
    )


def _tri_inverse(G, mk):
    lvl = mk['lvl']
    D = mk['eye2'] + jnp.where(lvl[0], G, 0.0)
    for m in lvl[1:]:
        LD = _dot(jnp.where(m, G, 0.0), D)
        yield
        D = D + _dot(D, LD)
        yield
    return D


def _lockstep(streams):
    streams = list(streams)
    out = [None] * len(streams)
    alive = list(range(len(streams)))
    while alive:
        for i in list(alive):
            try:
                next(streams[i])
            except StopIteration as stop:
                out[i] = stop.value
                alive.remove(i)
    return out


def _drain(gen):
    while True:
        try:
            next(gen)
        except StopIteration as stop:
            return stop.value


def _select_seq(full, n_seq, sid_rows):
    if n_seq == 1:
        return full
    acc = jnp.where(sid_rows == 0, full[:, 0:LANES], 0.0)
    for s in range(1, n_seq):
        acc = acc + jnp.where(sid_rows == s, full[:, s * LANES:(s + 1) * LANES], 0.0)
    return acc


def _expand_seq(x, n_seq, sid_rows):
    if n_seq == 1:
        return x
    return jnp.concatenate([jnp.where(sid_rows == s, x, 0.0) for s in range(n_seq)], axis=1)


def _wkv_pair_chunk(r, lw, cw, wl, k, v, kk, a, S, mk, Ls):
    C = r.shape[0]
    n_seq = C // Ls
    lo_half = _iota((C, HEAD_PAIR), 1) < 64
    own = (_iota((2 * C, HEAD_PAIR), 1) < 64) == (_iota((2 * C, HEAD_PAIR), 0) < C)
    stack = lambda t: jnp.where(own, jnp.concatenate([t, t], axis=0), 0.0)
    e_neg = jnp.exp(-cw)
    at = -kk * jnp.exp(cw - lw)
    bt = kk * a * e_neg
    kt = k * e_neg
    rt = r * jnp.exp(cw)
    Y = jnp.concatenate([bt, kt], axis=0)
    ga = _dot_nt(stack(at), Y)
    first_cols = _iota((2 * C, 2 * C), 1) < C
    top_rows = _iota((2 * C, 2 * C), 0) < C
    G = (jnp.where(first_cols & top_rows, ga, 0.0)
         + pltpu.roll(jnp.where(first_cols & jnp.logical_not(top_rows), ga, 0.0), C, 1))
    gr = _dot_nt(stack(rt), Y)
    sid2 = _shr(_iota((2 * C, LANES), 0) & (C - 1), Ls)
    XO = _select_seq(_dot_nt(jnp.concatenate([at, rt], axis=0), S), n_seq, sid2)
    yield
    Vz = jnp.concatenate([jnp.zeros_like(v), v], axis=0)
    xk = _dot(jnp.where(mk['ms_hi2'], ga, 0.0), Vz)
    D = yield from _tri_inverse(G, mk)
    X = XO[:C] + jnp.where(lo_half, xk[:C], xk[C:])
    DX = _dot(D, jnp.concatenate([X, X], axis=0))
    yield
    U = jnp.where(lo_half, DX[:C], DX[C:])
    Z = jnp.concatenate([U, v], axis=0)
    oz = _dot(jnp.where(mk['mi22'], gr, 0.0), Z)
    upd = _dot(Z.T, _expand_seq(Y, n_seq, sid2))
    yield
    o = XO[C:] + jnp.where(lo_half, oz[:C], oz[C:])
    bd = _shr(_iota((HEAD_PAIR, HEAD_PAIR), 0), 64) == _shr(_iota((HEAD_PAIR, HEAD_PAIR), 1), 64)
    new = []
    for s in range(n_seq):
        Ss = S[s * LANES:(s + 1) * LANES, :]
        new.append(jnp.where(bd, (Ss + upd[:, s * LANES:(s + 1) * LANES]) * wl[s * Ls:s * Ls + 1, :], 0.0))
    return o, new


def _hgrn_head_chunk(q, kf, b, blast, iv, ST, mk, Ls):
    C = q.shape[0]
    assert Ls == C or Ls <= SUB
    n_seq = C // Ls
    sid = _shr(_iota((C, LANES), 0), Ls)
    if Ls > SUB:
        rows_of = lambda off: jnp.concatenate(
            [jnp.broadcast_to(b[SUB * j + off:SUB * j + off + 1, :], (SUB, LANES)) for j in range(C // SUB)], axis=0)
        bmid = rows_of(SUB // 2 - 1)
        bend = rows_of(SUB - 1)
    else:
        bmid = 0.0
    o = _select_seq(_dot_nt(q * jnp.exp(b), ST), n_seq, sid)
    att = jnp.where(mk['mi'] & mk['sameblk'],
                    _dot_nt(q * jnp.exp(b - bmid), kf * jnp.exp(bmid - b)), 0.0)
    if Ls > SUB:
        ko = kf * jnp.exp(jnp.minimum(bend - b, 0.0))
        rb = _shr(mk['row'], SUB)
        cb = _shr(mk['col'], SUB)
        for j in range(C // SUB - 1):
            ref = b[SUB * j + SUB - 1:SUB * j + SUB, :]
            qo = q * jnp.exp(jnp.minimum(b - ref, 0.0))
            att = att + jnp.where(mk['same'] & (cb == j) & (rb > j), _dot_nt(qo, ko), 0.0)
    ke = kf * jnp.exp(blast - b)
    upd = _dot(iv.T, _expand_seq(ke, n_seq, sid))
    yield
    o = o + _dot(att, iv)
    yield
    dec = jnp.exp(blast)
    new = []
    for s in range(n_seq):
        new.append(ST[s * LANES:(s + 1) * LANES, :] * dec[s * Ls:s * Ls + 1, :]
                   + upd[:, s * LANES:(s + 1) * LANES])
    return o, new


def _ada_body(c_ref, w_ref, b_ref, o_ref):
    o_ref[...] = _dot(_silu(c_ref[...]), w_ref[...]) + b_ref[...]


def _inproj_body(rep, x_ref, sh_ref, sc_ref, n1_ref, w_ref, p_ref):
    h = _rms(x_ref[...], n1_ref[...]) * (1.0 + _repeat_rows(sc_ref[...], rep)) + _repeat_rows(sh_ref[...], rep)
    p_ref[...] = jnp.dot(h.astype(BF16), w_ref[...], preferred_element_type=F32)


PREP_FIELDS = ("r", "lw", "kmod", "v", "kk", "a", "g", "bonus", "q", "kf", "iv", "og")


def _rec_body(Ls, single_seq, R, fuse_in, *refs):
    refs = list(refs)
    if fuse_in:
        xfirst_ref, xnext_ref, sh1_ref, sc1_ref, n1_ref, win_ref = refs[:6]
        del refs[:6]
        p_ref = None
    else:
        p_ref = refs.pop(0)
    shinit_ref = refs.pop(0)
    if single_seq:
        wkvin_ref = hgin_ref = None
    else:
        wkvin_ref, hgin_ref = refs[:2]
        del refs[:2]
    (mu_ref, w0_ref, a0_ref, kk_ref, ka_ref, rk_ref, lnw_ref, lnb_ref, lb_ref, gn_ref, wcomb_ref, wgate_ref,
     y_ref, shout_ref, wkvout_ref, hgout_ref, wkv_sc, hg_sc) = refs[:18]
    if fuse_in:
        assert single_seq
        prep_sc = dict(zip(PREP_FIELDS, refs[18:18 + len(PREP_FIELDS)]))
        cum_sc = refs[18 + len(PREP_FIELDS)]
    C = CHUNK
    n_seq = C // Ls
    NP = RWKV_HEADS // 2
    c = pl.program_id(1)
    rows = lambda j: slice(j * C, (j + 1) * C)
    blk = lambda s: slice(s * LANES, (s + 1) * LANES)
    pairs = [slice(hp * HEAD_PAIR, (hp + 1) * HEAD_PAIR) for hp in range(NP)]
    mk = _make_masks(C, Ls)
    gsum = (_shr(_iota((HEAD_PAIR, HEAD_PAIR), 0), 64) == _shr(_iota((HEAD_PAIR, HEAD_PAIR), 1), 64)).astype(F32)

    def prepare(pcols, first):
        rowv = _iota((C, SHIFT_WIDTH), 0)
        xs = []
        for j in range(R):
            p_rw = pcols(j, 0, SHIFT_WIDTH)
            p_prev = jnp.where(rowv == 0, shout_ref[j, C - 1:C, :], pltpu.roll(p_rw, 1, 0))
            if not single_seq:
                p_prev = jnp.where((rowv & (Ls - 1)) == 0, shinit_ref[j], p_prev)
            elif first:
                p_prev = jnp.where(rowv == 0, shinit_ref[j], p_prev)
            shout_ref[j] = p_rw
            xs.append(p_rw + (p_prev - p_rw) * mu_ref[...])
        x = jnp.concatenate(xs, axis=0)
        cat = lambda lo, hi: jnp.concatenate([pcols(j, lo, hi) for j in range(R)], axis=0)
        r = x[:, 0:512]
        k = x[:, 512:1024]
        v = x[:, 1024:1536]
        wa = x[:, 1536:1664]
        gd = x[:, 1664:1792]
        lane = _iota((R * C, LANES), 1)
        da = _dot(jnp.where(lane < 64, jnp.tanh(wa), wa), wcomb_ref[...])
        g = _dot(_sigmoid(gd), wgate_ref[...])
        yield
        w_log = -_softplus(-(w0_ref[...] + da[:, :512])) - 0.5
        lw = -jnp.exp(w_log)
        a = _sigmoid(a0_ref[...] + da[:, 512:])
        kkr = k * kk_ref[...]
        kmod = k * (1.0 + (a - 1.0) * ka_ref[...])
        rkr = r * kmod * rk_ref[...]
        lbp = lb_ref[...]
        m = jnp.maximum(lbp[0:1, :], lbp[1:2, :])
        e0 = jnp.exp(lbp[0:1, :] - m)
        e1 = jnp.exp(lbp[1:2, :] - m)
        lb = e0 / (e0 + e1)
        q = _silu(cat(1792, 2304))
        f = lb + (1.0 - lb) * _sigmoid(cat(2304, 2816))
        logs = jnp.concatenate([lw, jnp.log(f)], axis=1)
        cums, tots = [], []
        for j in range(R):
            if single_seq:
                cums.append(_dot_2pass_l(mk['mi'].astype(F32), logs[rows(j)]))
            else:
                both = _dot_2pass_l(jnp.concatenate([mk['mi'], mk['same']], axis=0).astype(F32), logs[rows(j)])
                cums.append(both[:C])
                tots.append(both[C:])
        sums = _group_sums([kkr[:, sl] * kkr[:, sl] for sl in pairs] + [rkr[:, sl] for sl in pairs], gsum)
        yield
        kk = jnp.concatenate([kkr[:, sl] / jnp.maximum(jnp.sqrt(sums[hp]), 1e-12)
                              for hp, sl in enumerate(pairs)], axis=1)
        bonus = jnp.concatenate(sums[NP:], axis=1) * v
        d = dict(r=r, lw=lw, kmod=kmod, v=v, kk=kk, a=a, g=g, bonus=bonus, q=q, kf=1.0 - f,
                 iv=cat(2816, 3328), og=cat(3328, 3840))
        return d, jnp.concatenate(cums, axis=0), tots

    def next_chunk(x_ref, first):
        hs = [_rms(x_ref[j], n1_ref[...]) * (1.0 + sc1_ref[j]) + sh1_ref[j] for j in range(R)]
        hn = jnp.concatenate(hs, axis=0).astype(BF16)
        tiles = []
        for t in range(IN_WIDTH // INPROJ_TILE):
            tiles.append(jnp.dot(hn, win_ref[:, t * INPROJ_TILE:(t + 1) * INPROJ_TILE],
                                 preferred_element_type=F32))
            yield
        pcols = lambda j, lo, hi: jnp.concatenate(tiles[lo // INPROJ_TILE:hi // INPROJ_TILE], axis=1)[rows(j)]
        d, cum, _ = yield from prepare(pcols, first)
        return d, cum

    def hand_over(d, cum):
        for name in PREP_FIELDS:
            prep_sc[name][...] = d[name]
        cum_sc[...] = cum

    @pl.when(c == 0)
    def _():
        shout_ref[...] = jnp.zeros_like(shout_ref)
        wkv_sc[...] = jnp.zeros_like(wkv_sc)
        if single_seq:
            hg_sc[...] = jnp.zeros_like(hg_sc)
        else:
            for j in range(R):
                for s in range(n_seq):
                    for hp in range(NP):
                        wkv_sc[j, hp, s * LANES:s * LANES + 64, 0:64] = wkvin_ref[j * n_seq + s, 2 * hp]
                        wkv_sc[j, hp, s * LANES + 64:(s + 1) * LANES, 64:128] = wkvin_ref[j * n_seq + s, 2 * hp + 1]
                    for h in range(HGRN_HEADS):
                        hg_sc[j, h, blk(s), :] = hgin_ref[j * n_seq + s, h].T
        if fuse_in:
            hand_over(*_drain(next_chunk(xfirst_ref, True)))

    if fuse_in:
        d = {name: prep_sc[name][...] for name in PREP_FIELDS}
        cum = cum_sc[...]
        extra = [next_chunk(xnext_ref, False)]
    else:
        d, cum, tots = _drain(prepare(lambda j, lo, hi: p_ref[j, :, lo:hi], True))
        extra = []
    if single_seq:
        tots = [cum[j * C + C - 1:j * C + C, :] for j in range(R)]
    lnw = lnw_ref[...]
    lnb = lnb_ref[...]

    def wkv_stream(j, hp):
        sl = pairs[hp]
        o, new = yield from _wkv_pair_chunk(d["r"][rows(j), sl], d["lw"][rows(j), sl], cum[rows(j), sl],
                                            jnp.exp(tots[j][:, sl]), d["kmod"][rows(j), sl], d["v"][rows(j), sl],
                                            d["kk"][rows(j), sl], d["a"][rows(j), sl], wkv_sc[j, hp], mk, Ls)
        for s in range(n_seq):
            wkv_sc[j, hp, blk(s), :] = new[s]
        return o

    def hgrn_stream(j, h):
        sl = slice(RWKV_WIDTH + h * LANES, RWKV_WIDTH + (h + 1) * LANES)
        hs = blk(h)
        o, new = yield from _hgrn_head_chunk(d["q"][rows(j), hs], d["kf"][rows(j), hs], cum[rows(j), sl],
                                             tots[j][:, sl], d["iv"][rows(j), hs], hg_sc[j, h], mk, Ls)
        for s in range(n_seq):
            hg_sc[j, h, blk(s), :] = new[s]
        on = _rms(o, gn_ref[...])
        y_ref[j, :, sl] = (on * _silu(d["og"][rows(j), hs])).astype(y_ref.dtype)

    outs = _lockstep([wkv_stream(j, hp) for j in range(R) for hp in range(NP)]
                     + [hgrn_stream(j, h) for j in range(R) for h in range(HGRN_HEADS)] + extra)

    os_ = [jnp.concatenate([outs[j * NP + hp] for j in range(R)], axis=0) for hp in range(NP)]
    means = _group_sums(os_, gsum)
    ds = [o - mu * (1.0 / 64.0) for o, mu in zip(os_, means)]
    vars_ = _group_sums([dd * dd for dd in ds], gsum)
    for hp, sl in enumerate(pairs):
        on = ds[hp] * lax.rsqrt(vars_[hp] * (1.0 / 64.0) + LNX_EPS) * lnw[:, sl] + lnb[:, sl]
        yv = ((on + d["bonus"][:, sl]) * d["g"][:, sl]).astype(y_ref.dtype)
        for j in range(R):
            y_ref[j, :, sl] = yv[rows(j)]

    if fuse_in:
        hand_over(*outs[-1])

    @pl.when(c == pl.num_programs(1) - 1)
    def _():
        for j in range(R):
            for s in range(n_seq):
                for hp in range(NP):
                    wkvout_ref[j * n_seq + s, 2 * hp] = wkv_sc[j, hp, s * LANES:s * LANES + 64, 0:64]
                    wkvout_ref[j * n_seq + s, 2 * hp + 1] = wkv_sc[j, hp, s * LANES + 64:(s + 1) * LANES, 64:128]
                for h in range(HGRN_HEADS):
                    hgout_ref[j * n_seq + s, h] = hg_sc[j, h, blk(s), :].T


def _out_body(rep, x_ref, ym_ref, gt1_ref, sh2_ref, sc2_ref, gt2_ref, n2_ref, nf_ref,
              wo_ref, wu_ref, wd_ref, o_ref):
    gt1, sh2, sc2, gt2 = (_repeat_rows(m[...], rep) for m in (gt1_ref, sh2_ref, sc2_ref, gt2_ref))
    y = jnp.dot(ym_ref[...], wo_ref[...], preferred_element_type=F32)
    x1 = x_ref[...] + gt1 * y
    h = (_rms(x1, n2_ref[...]) * (1.0 + sc2) + sh2).astype(BF16)
    acc = jnp.zeros_like(x1)
    FC = 1024
    for j in range(D_FF // FC):
        u = jnp.dot(h, wu_ref[:, j * FC:(j + 1) * FC], preferred_element_type=F32)
        u = jnp.square(jnp.maximum(u, 0.0)).astype(BF16)
        acc = acc + jnp.dot(u, wd_ref[j * FC:(j + 1) * FC, :], preferred_element_type=F32)
    x2 = x1 + gt2 * acc
    o_ref[...] = _rms(x2, nf_ref[...])


def _params(sem):
    return pltpu.CompilerParams(dimension_semantics=sem, vmem_limit_bytes=VMEM_LIMIT)


def _full(shape):
    return pl.BlockSpec(shape, lambda *_: (0,) * len(shape))


def _ada(c_all, w_ada, b_ada):
    n = c_all.shape[0]
    TN = 1024
    return pl.pallas_call(
        _ada_body,
        grid=(w_ada.shape[1] // TN,),
        in_specs=[pl.BlockSpec((n, D_MODEL), lambda j: (0, 0)),
                  pl.BlockSpec((D_MODEL, TN), lambda j: (0, j)),
                  pl.BlockSpec((1, TN), lambda j: (0, j))],
        out_specs=pl.BlockSpec((n, TN), lambda j: (0, j)),
        out_shape=jax.ShapeDtypeStruct((n, w_ada.shape[1]), F32),
        compiler_params=_params(("arbitrary",)),
        name="ada",
    )(c_all, w_ada, b_ada)


def _mod_specs(mod, cols, TM, rows_per_mod):
    if mod.ndim == 3:
        return [pl.BlockSpec((None, 1, D_MODEL), lambda i, c=c: (i * TM // rows_per_mod, 0, c)) for c in cols]
    return [pl.BlockSpec((TM // rows_per_mod, D_MODEL), lambda i, c=c: (i, c)) for c in cols]


def _mod_repeat(mod, rows_per_mod):
    return 1 if mod.ndim == 3 else rows_per_mod


def _inproj(x, mod, norm1, w_in, TM, rows_per_mod):
    M = x.shape[0]
    return pl.pallas_call(
        functools.partial(_inproj_body, _mod_repeat(mod, rows_per_mod)),
        grid=(M // TM,),
        in_specs=[pl.BlockSpec((TM, D_MODEL), lambda i: (i, 0))]
                 + _mod_specs(mod, (0, 1), TM, rows_per_mod)
                 + [_full((1, D_MODEL)), _full((D_MODEL, IN_WIDTH))],
        out_specs=pl.BlockSpec((TM, IN_WIDTH), lambda i: (i, 0)),
        out_shape=jax.ShapeDtypeStruct((M, IN_WIDTH), F32),
        compiler_params=_params(("arbitrary",)),
        name="inproj",
    )(x, mod, mod, norm1, w_in)


def _rec(p, shinit, states, small, wcomb, wgate, Ls, n_batch, R, inproj=None):
    Gt, C = shinit.shape[0], CHUNK
    rows_total = (p if inproj is None else inproj[0]).shape[1]
    NC = rows_total // C
    n_seq = C // Ls
    wkv_spec = pl.BlockSpec((R * n_seq, RWKV_HEADS, 64, 64), lambda g, c: (g, 0, 0, 0))
    hg_spec = pl.BlockSpec((R * n_seq, HGRN_HEADS, LANES, LANES), lambda g, c: (g, 0, 0, 0))
    sh_spec = pl.BlockSpec((R, C, SHIFT_WIDTH), lambda g, c: (g, 0, 0))
    st_in = [] if states is None else list(states)
    st_specs = [] if states is None else [wkv_spec, hg_spec]
    scratch = [pltpu.VMEM((R, RWKV_HEADS // 2, n_seq * LANES, LANES), F32),
               pltpu.VMEM((R, HGRN_HEADS, n_seq * LANES, LANES), F32)]
    if inproj is None:
        lead_in = [p]
        lead_specs = [pl.BlockSpec((R, C, IN_WIDTH), lambda g, c: (g, c, 0))]
    else:
        x, mod, norm1, w_in = inproj
        lead_in = [x, x, mod, mod, norm1, w_in]
        lead_specs = [pl.BlockSpec((R, C, D_MODEL), lambda g, c: (g, 0, 0)),
                      pl.BlockSpec((R, C, D_MODEL), lambda g, c: (g, jnp.minimum(c + 1, NC - 1), 0)),
                      pl.BlockSpec((R, 1, D_MODEL), lambda g, c: (g, 0, 0)),
                      pl.BlockSpec((R, 1, D_MODEL), lambda g, c: (g, 0, 1)),
                      _full(norm1.shape), _full(w_in.shape)]
        scratch += [pltpu.VMEM((R * C, RWKV_WIDTH), F32) for _ in PREP_FIELDS]
        scratch.append(pltpu.VMEM((R * C, 2 * RWKV_WIDTH), F32))
    return pl.pallas_call(
        functools.partial(_rec_body, Ls, states is None, R, inproj is not None),
        grid=(Gt // R, NC),
        in_specs=lead_specs + [sh_spec] + st_specs
                 + [_full(s.shape) for s in small] + [_full(wcomb.shape), _full(wgate.shape)],
        out_specs=[pl.BlockSpec((R, C, D_MODEL), lambda g, c: (g, c, 0)), sh_spec, wkv_spec, hg_spec],
        out_shape=[jax.ShapeDtypeStruct((Gt, NC * C, D_MODEL), BF16),
                   jax.ShapeDtypeStruct(shinit.shape, F32),
                   jax.ShapeDtypeStruct((n_batch, RWKV_HEADS, 64, 64), F32),
                   jax.ShapeDtypeStruct((n_batch, HGRN_HEADS, LANES, LANES), F32)],
        scratch_shapes=scratch,
        compiler_params=_params(("arbitrary", "arbitrary")),
        name="rec",
    )(*lead_in, shinit, *st_in, *small, wcomb, wgate)


def _out(x, ym, mod, norm2, norm_f, w_out, w_up, w_down, TM, rows_per_mod):
    M = x.shape[0]
    row = lambda i: (i, 0)
    return pl.pallas_call(
        functools.partial(_out_body, _mod_repeat(mod, rows_per_mod)),
        grid=(M // TM,),
        in_specs=[pl.BlockSpec((TM, D_MODEL), row), pl.BlockSpec((TM, D_MODEL), row)]
                 + _mod_specs(mod, (2, 3, 4, 5), TM, rows_per_mod)
                 + [_full((1, D_MODEL)), _full((1, D_MODEL)),
                    _full(w_out.shape), _full(w_up.shape), _full(w_down.shape)],
        out_specs=pl.BlockSpec((TM, D_MODEL), row),
        out_shape=jax.ShapeDtypeStruct((M, D_MODEL), F32),
        compiler_params=_params(("arbitrary",)),
        name="outmlp",
    )(x, ym, mod, mod, mod, mod, norm2, norm_f, w_out, w_up, w_down)


def kernel(x_prompt, x_sample, c_prompt, c_sample, state_shift, state_wkv, state_hgrn, norm1, norm2, norm_f, w_ada, b_ada, w_in, mu_shift, w0, w_decay_up, a0, w_aaa_up, w_gate_up, k_k, k_a, r_k, lnx_w, lnx_b, hgrn_lb, hgrn_gnorm, w_out, w_up, w_down):
    BP, TP, _ = x_prompt.shape
    BS, TS, _ = x_sample.shape
    l = 0
    row = lambda t: t.reshape(1, -1)

    mod = _ada(jnp.concatenate([c_prompt, c_sample], axis=0), w_ada[l], row(b_ada[l]))
    mod_p = mod[:BP].reshape(BP, 1, 6 * D_MODEL)
    mod_s = mod[BP:]

    w_in_b = w_in[l].astype(BF16)
    w_out_b = w_out[l].astype(BF16)
    w_up_b = w_up[l].astype(BF16)
    w_down_b = w_down[l].astype(BF16)
    zer = jnp.zeros((64, RWKV_WIDTH), F32)
    wcomb = jnp.concatenate([jnp.concatenate([w_decay_up[l], zer], axis=1),
                             jnp.concatenate([zer, w_aaa_up[l]], axis=1)], axis=0).astype(BF16)
    wgate = w_gate_up[l].astype(BF16)
    small = [row(mu_shift[l]), row(w0[l]), row(a0[l]), row(k_k[l]), row(k_a[l]), row(r_k[l]),
             row(lnx_w[l]), row(lnx_b[l]), hgrn_lb, row(hgrn_gnorm[l])]
    n1, n2, nf = row(norm1[l]), row(norm2[l]), row(norm_f)

    def trunk(x2d, modx, rows_per_mod, shinit, states, Ls, n_batch, TM, R, fuse_in):
        Gt = shinit.shape[0]
        if fuse_in:
            ym, shout, wkv_o, hg_o = _rec(None, shinit, states, small, wcomb, wgate, Ls, n_batch, R,
                                          inproj=(x2d.reshape(Gt, -1, D_MODEL), modx, n1, w_in_b))
        else:
            p = _inproj(x2d, modx, n1, w_in_b, TM, rows_per_mod)
            ym, shout, wkv_o, hg_o = _rec(p.reshape(Gt, -1, IN_WIDTH), shinit, states, small, wcomb, wgate,
                                          Ls, n_batch, R)
        y = _out(x2d, ym.reshape(-1, D_MODEL), modx, n2, nf, w_out_b, w_up_b, w_down_b, TM, rows_per_mod)
        return y, shout, wkv_o, hg_o

    yp, shp, wkvp, hgp = trunk(x_prompt.reshape(BP * TP, D_MODEL), mod_p, TP,
                               jnp.zeros((BP, CHUNK, SHIFT_WIDTH), F32), None, CHUNK, BP, 512, PROMPT_GROUPS_PER_STEP, True)
    y_prompt = yp.reshape(BP, TP, D_MODEL)
    shift_p = shp[:, CHUNK - 1][None]
    wkv_p = wkvp[None]
    hgrn_p = hgp[None]

    n_seq = CHUNK // TS
    GS = BS // n_seq
    shinit_s = jnp.zeros((GS, n_seq, TS, SHIFT_WIDTH), F32).at[:, :, 0].set(
        state_shift[l].reshape(GS, n_seq, SHIFT_WIDTH)).reshape(GS, CHUNK, SHIFT_WIDTH)
    ys, shs, wkvs, hgs = trunk(x_sample.reshape(BS * TS, D_MODEL), mod_s, TS, shinit_s,
                               (state_wkv[l], state_hgrn[l]), TS, BS, 256, 1, False)
    y_sample = ys.reshape(BS, TS, D_MODEL)
    shift_s = shs.reshape(GS, n_seq, TS, SHIFT_WIDTH)[:, :, TS - 1].reshape(BS, SHIFT_WIDTH)[None]
    wkv_s = wkvs[None]
    hgrn_s = hgs[None]

    return (y_prompt, y_sample, shift_p, wkv_p, hgrn_p, shift_s, wkv_s, hgrn_s)
```

```python
import functools

import jax
import jax.numpy as jnp
from jax import lax
from jax.experimental import pallas as pl
from jax.experimental.pallas import tpu as pltpu

F32 = jnp.float32
BF16 = jnp.bfloat16

D_MODEL = 1024
RWKV_HEADS = 8
RWKV_WIDTH = 512
HGRN_HEADS = 4
HGRN_WIDTH = 512
SHIFT_WIDTH = 1792
IN_WIDTH = 3840
D_FF = 4096
NORM_EPS = 1e-6
LNX_EPS = 64e-5

LANES = 128
HEAD_PAIR = 128
CHUNK = 64
SUB = 16
PROMPT_GROUPS_PER_STEP = 4
INPROJ_TILE = 256
VMEM_LIMIT = 58 * 1024 * 1024


def _dot(a, b):
    return jnp.dot(a.astype(BF16), b.astype(BF16), preferred_element_type=F32)


def _dot_nt(a, b):
    return lax.dot_general(a.astype(BF16), b.astype(BF16), (((1,), (1,)), ((), ())),
                           preferred_element_type=F32)


def _dot_2pass_l(m, x):
    mb = m.astype(BF16)
    hi = x.astype(BF16)
    lo = (x - hi.astype(F32)).astype(BF16)
    return jnp.dot(mb, hi, preferred_element_type=F32) + jnp.dot(mb, lo, preferred_element_type=F32)


def _repeat_rows(m, rep):
    if rep == 1:
        return m
    n = m.shape[0]
    sel = (_shr(_iota((n * rep, n), 0), rep) == _iota((n * rep, n), 1)).astype(BF16)
    hi = m.astype(BF16)
    r1 = m - hi.astype(F32)
    mid = r1.astype(BF16)
    lo = (r1 - mid.astype(F32)).astype(BF16)
    d = lambda part: jnp.dot(sel, part, preferred_element_type=F32)
    return d(hi) + d(mid) + d(lo)


def _group_sums(xs, gmat):
    n, rows = len(xs), xs[0].shape[0]
    P = _dot(jnp.concatenate(xs, axis=0), gmat)
    return [P[i * rows:(i + 1) * rows] for i in range(n)]


def _iota(shape, dim):
    return lax.broadcasted_iota(jnp.int32, shape, dim)


def _shr(x, n):
    return lax.shift_right_logical(x, jnp.int32(n.bit_length() - 1))


def _sigmoid(x):
    return 1.0 / (1.0 + jnp.exp(-x))


def _silu(x):
    return x * _sigmoid(x)


def _softplus(x):
    return jnp.maximum(x, 0.0) + jnp.log(1.0 + jnp.exp(-jnp.abs(x)))


def _rms(x, gain):
    return x * lax.rsqrt(jnp.mean(x * x, axis=-1, keepdims=True) + NORM_EPS) * gain


def _make_masks(C, Ls):
    row = _iota((C, C), 0)
    col = _iota((C, C), 1)
    seq = lambda t: _shr(t, Ls)
    blk = lambda t: _shr(t, SUB)
    same = seq(row) == seq(col)
    R = _iota((2 * C, 2 * C), 0)
    Q = _iota((2 * C, 2 * C), 1)
    tr = R & (C - 1)
    tq = Q & (C - 1)
    same2 = seq(tr) == seq(tq)
    lvl = []
    s = 1
    while s < Ls:
        lvl.append((_shr(R, 2 * s) == _shr(Q, 2 * s)) & ((R & (2 * s - 1)) >= s) & ((Q & (2 * s - 1)) < s))
        s *= 2
    return dict(
        row=row, col=col, same=same,
        mi=same & (col <= row),
        sameblk=blk(row) == blk(col),
        eye2=(R == Q).astype(F32),
        lvl=lvl,
        ms_hi2=same2 & (tq < tr) & (Q >= C),
        mi22=same2 & (tq <= tr),
    )


def _tri_inverse(G, mk):
    lvl = mk['lvl']
    D = mk['eye2'] + jnp.where(lvl[0], G, 0.0)
    for m in lvl[1:]:
        LD = _dot(jnp.where(m, G, 0.0), D)
        yield
        D = D + _dot(D, LD)
        yield
    return D


def _lockstep(streams):
    streams = list(streams)
    out = [None] * len(streams)
    alive = list(range(len(streams)))
    while alive:
        for i in list(alive):
            try:
                next(streams[i])
            except StopIteration as stop:
                out[i] = stop.value
                alive.remove(i)
    return out


def _drain(gen):
    while True:
        try:
            next(gen)
        except StopIteration as stop:
            return stop.value


def _select_seq(full, n_seq, sid_rows):
    if n_seq == 1:
        return full
    acc = jnp.where(sid_rows == 0, full[:, 0:LANES], 0.0)
    for s in range(1, n_seq):
        acc = acc + jnp.where(sid_rows == s, full[:, s * LANES:(s + 1) * LANES], 0.0)
    return acc


def _expand_seq(x, n_seq, sid_rows):
    if n_seq == 1:
        return x
    return jnp.concatenate([jnp.where(sid_rows == s, x, 0.0) for s in range(n_seq)], axis=1)


def _wkv_pair_chunk(r, lw, cw, wl, k, v, kk, a, S, mk, Ls):
    C = r.shape[0]
    n_seq = C // Ls
    lo_half = _iota((C, HEAD_PAIR), 1) < 64
    own = (_iota((2 * C, HEAD_PAIR), 1) < 64) == (_iota((2 * C, HEAD_PAIR), 0) < C)
    stack = lambda t: jnp.where(own, jnp.concatenate([t, t], axis=0), 0.0)
    e_neg = jnp.exp(-cw)
    at = -kk * jnp.exp(cw - lw)
    bt = kk * a * e_neg
    kt = k * e_neg
    rt = r * jnp.exp(cw)
    Y = jnp.concatenate([bt, kt], axis=0)
    ga = _dot_nt(stack(at), Y)
    first_cols = _iota((2 * C, 2 * C), 1) < C
    top_rows = _iota((2 * C, 2 * C), 0) < C
    G = (jnp.where(first_cols & top_rows, ga, 0.0)
         + pltpu.roll(jnp.where(first_cols & jnp.logical_not(top_rows), ga, 0.0), C, 1))
    gr = _dot_nt(stack(rt), Y)
    sid2 = _shr(_iota((2 * C, LANES), 0) & (C - 1), Ls)
    XO = _select_seq(_dot_nt(jnp.concatenate([at, rt], axis=0), S), n_seq, sid2)
    yield
    Vz = jnp.concatenate([jnp.zeros_like(v), v], axis=0)
    xk = _dot(jnp.where(mk['ms_hi2'], ga, 0.0), Vz)
    D = yield from _tri_inverse(G, mk)
    X = XO[:C] + jnp.where(lo_half, xk[:C], xk[C:])
    DX = _dot(D, jnp.concatenate([X, X], axis=0))
    yield
    U = jnp.where(lo_half, DX[:C], DX[C:])
    Z = jnp.concatenate([U, v], axis=0)
    oz = _dot(jnp.where(mk['mi22'], gr, 0.0), Z)
    upd = _dot(Z.T, _expand_seq(Y, n_seq, sid2))
    yield
    o = XO[C:] + jnp.where(lo_half, oz[:C], oz[C:])
    bd = _shr(_iota((HEAD_PAIR, HEAD_PAIR), 0), 64) == _shr(_iota((HEAD_PAIR, HEAD_PAIR), 1), 64)
    new = []
    for s in range(n_seq):
        Ss = S[s * LANES:(s + 1) * LANES, :]
        new.append(jnp.where(bd, (Ss + upd[:, s * LANES:(s + 1) * LANES]) * wl[s * Ls:s * Ls + 1, :], 0.0))
    return o, new


def _hgrn_head_chunk(q, kf, b, blast, iv, ST, mk, Ls):
    C = q.shape[0]
    assert Ls == C or Ls <= SUB
    n_seq = C // Ls
    sid = _shr(_iota((C, LANES), 0), Ls)
    if Ls > SUB:
        rows_of = lambda off: jnp.concatenate(
            [jnp.broadcast_to(b[SUB * j + off:SUB * j + off + 1, :], (SUB, LANES)) for j in range(C // SUB)], axis=0)
        bmid = rows_of(SUB // 2 - 1)
        bend = rows_of(SUB - 1)
    else:
        bmid = 0.0
    o = _select_seq(_dot_nt(q * jnp.exp(b), ST), n_seq, sid)
    att = jnp.where(mk['mi'] & mk['sameblk'],
                    _dot_nt(q * jnp.exp(b - bmid), kf * jnp.exp(bmid - b)), 0.0)
    if Ls > SUB:
        ko = kf * jnp.exp(jnp.minimum(bend - b, 0.0))
        rb = _shr(mk['row'], SUB)
        cb = _shr(mk['col'], SUB)
        for j in range(C // SUB - 1):
            ref = b[SUB * j + SUB - 1:SUB * j + SUB, :]
            qo = q * jnp.exp(jnp.minimum(b - ref, 0.0))
            att = att + jnp.where(mk['same'] & (cb == j) & (rb > j), _dot_nt(qo, ko), 0.0)
    ke = kf * jnp.exp(blast - b)
    upd = _dot(iv.T, _expand_seq(ke, n_seq, sid))
    yield
    o = o + _dot(att, iv)
    yield
    dec = jnp.exp(blast)
    new = []
    for s in range(n_seq):
        new.append(ST[s * LANES:(s + 1) * LANES, :] * dec[s * Ls:s * Ls + 1, :]
                   + upd[:, s * LANES:(s + 1) * LANES])
    return o, new


def _ada_body(c_ref, w_ref, b_ref, o_ref):
    o_ref[...] = _dot(_silu(c_ref[...]), w_ref[...]) + b_ref[...]


def _inproj_body(rep, x_ref, sh_ref, sc_ref, n1_ref, w_ref, p_ref):
    h = _rms(x_ref[...], n1_ref[...]) * (1.0 + _repeat_rows(sc_ref[...], rep)) + _repeat_rows(sh_ref[...], rep)
    p_ref[...] = jnp.dot(h.astype(BF16), w_ref[...], preferred_element_type=F32)


PREP_FIELDS = ("r", "lw", "kmod", "v", "kk", "a", "g", "bonus", "q", "kf", "iv", "og")


def _rec_body(Ls, single_seq, R, fuse_in, *refs):
    refs = list(refs)
    if fuse_in:
        xfirst_ref, xnext_ref, sh1_ref, sc1_ref, n1_ref, win_ref = refs[:6]
        del refs[:6]
        p_ref = None
    else:
        p_ref = refs.pop(0)
    shinit_ref = refs.pop(0)
    if single_seq:
        wkvin_ref = hgin_ref = None
    else:
        wkvin_ref, hgin_ref = refs[:2]
        del refs[:2]
    (mu_ref, w0_ref, a0_ref, kk_ref, ka_ref, rk_ref, lnw_ref, lnb_ref, lb_ref, gn_ref, wcomb_ref, wgate_ref,
     y_ref, shout_ref, wkvout_ref, hgout_ref, wkv_sc, hg_sc) = refs[:18]
    if fuse_in:
        assert single_seq
        prep_sc = dict(zip(PREP_FIELDS, refs[18:18 + len(PREP_FIELDS)]))
        cum_sc = refs[18 + len(PREP_FIELDS)]
    C = CHUNK
    n_seq = C // Ls
    NP = RWKV_HEADS // 2
    c = pl.program_id(1)
    rows = lambda j: slice(j * C, (j + 1) * C)
    blk = lambda s: slice(s * LANES, (s + 1) * LANES)
    pairs = [slice(hp * HEAD_PAIR, (hp + 1) * HEAD_PAIR) for hp in range(NP)]
    mk = _make_masks(C, Ls)
    gsum = (_shr(_iota((HEAD_PAIR, HEAD_PAIR), 0), 64) == _shr(_iota((HEAD_PAIR, HEAD_PAIR), 1), 64)).astype(F32)

    def prepare(pcols, first):
        rowv = _iota((C, SHIFT_WIDTH), 0)
        xs = []
        for j in range(R):
            p_rw = pcols(j, 0, SHIFT_WIDTH)
            p_prev = jnp.where(rowv == 0, shout_ref[j, C - 1:C, :], pltpu.roll(p_rw, 1, 0))
            if not single_seq:
                p_prev = jnp.where((rowv & (Ls - 1)) == 0, shinit_ref[j], p_prev)
            elif first:
                p_prev = jnp.where(rowv == 0, shinit_ref[j], p_prev)
            shout_ref[j] = p_rw
            xs.append(p_rw + (p_prev - p_rw) * mu_ref[...])
        x = jnp.concatenate(xs, axis=0)
        cat = lambda lo, hi: jnp.concatenate([pcols(j, lo, hi) for j in range(R)], axis=0)
        r = x[:, 0:512]
        k = x[:, 512:1024]
        v = x[:, 1024:1536]
        wa = x[:, 1536:1664]
        gd = x[:, 1664:1792]
        lane = _iota((R * C, LANES), 1)
        da = _dot(jnp.where(lane < 64, jnp.tanh(wa), wa), wcomb_ref[...])
        g = _dot(_sigmoid(gd), wgate_ref[...])
        yield
        w_log = -_softplus(-(w0_ref[...] + da[:, :512])) - 0.5
        lw = -jnp.exp(w_log)
        a = _sigmoid(a0_ref[...] + da[:, 512:])
        kkr = k * kk_ref[...]
        kmod = k * (1.0 + (a - 1.0) * ka_ref[...])
        rkr = r * kmod * rk_ref[...]
        lbp = lb_ref[...]
        m = jnp.maximum(lbp[0:1, :], lbp[1:2, :])
        e0 = jnp.exp(lbp[0:1, :] - m)
        e1 = jnp.exp(lbp[1:2, :] - m)
        lb = e0 / (e0 + e1)
        q = _silu(cat(1792, 2304))
        f = lb + (1.0 - lb) * _sigmoid(cat(2304, 2816))
        logs = jnp.concatenate([lw, jnp.log(f)], axis=1)
        cums, tots = [], []
        for j in range(R):
            if single_seq:
                cums.append(_dot_2pass_l(mk['mi'].astype(F32), logs[rows(j)]))
            else:
                both = _dot_2pass_l(jnp.concatenate([mk['mi'], mk['same']], axis=0).astype(F32), logs[rows(j)])
                cums.append(both[:C])
                tots.append(both[C:])
        sums = _group_sums([kkr[:, sl] * kkr[:, sl] for sl in pairs] + [rkr[:, sl] for sl in pairs], gsum)
        yield
        kk = jnp.concatenate([kkr[:, sl] / jnp.maximum(jnp.sqrt(sums[hp]), 1e-12)
                              for hp, sl in enumerate(pairs)], axis=1)
        bonus = jnp.concatenate(sums[NP:], axis=1) * v
        d = dict(r=r, lw=lw, kmod=kmod, v=v, kk=kk, a=a, g=g, bonus=bonus, q=q, kf=1.0 - f,
                 iv=cat(2816, 3328), og=cat(3328, 3840))
        return d, jnp.concatenate(cums, axis=0), tots

    def next_chunk(x_ref, first):
        hs = [_rms(x_ref[j], n1_ref[...]) * (1.0 + sc1_ref[j]) + sh1_ref[j] for j in range(R)]
        hn = jnp.concatenate(hs, axis=0).astype(BF16)
        tiles = []
        for t in range(IN_WIDTH // INPROJ_TILE):
            tiles.append(jnp.dot(hn, win_ref[:, t * INPROJ_TILE:(t + 1) * INPROJ_TILE],
                                 preferred_element_type=F32))
            yield
        pcols = lambda j, lo, hi: jnp.concatenate(tiles[lo // INPROJ_TILE:hi // INPROJ_TILE], axis=1)[rows(j)]
        d, cum, _ = yield from prepare(pcols, first)
        return d, cum

    def hand_over(d, cum):
        for name in PREP_FIELDS:
            prep_sc[name][...] = d[name]
        cum_sc[...] = cum

    @pl.when(c == 0)
    def _():
        shout_ref[...] = jnp.zeros_like(shout_ref)
        wkv_sc[...] = jnp.zeros_like(wkv_sc)
        if single_seq:
            hg_sc[...] = jnp.zeros_like(hg_sc)
        else:
            for j in range(R):
                for s in range(n_seq):
                    for hp in range(NP):
                        wkv_sc[j, hp, s * LANES:s * LANES + 64, 0:64] = wkvin_ref[j * n_seq + s, 2 * hp]
                        wkv_sc[j, hp, s * LANES + 64:(s + 1) * LANES, 64:128] = wkvin_ref[j * n_seq + s, 2 * hp + 1]
                    for h in range(HGRN_HEADS):
                        hg_sc[j, h, blk(s), :] = hgin_ref[j * n_seq + s, h].T
        if fuse_in:
            hand_over(*_drain(next_chunk(xfirst_ref, True)))

    if fuse_in:
        d = {name: prep_sc[name][...] for name in PREP_FIELDS}
        cum = cum_sc[...]
        extra = [next_chunk(xnext_ref, False)]
    else:
        d, cum, tots = _drain(prepare(lambda j, lo, hi: p_ref[j, :, lo:hi], True))
        extra = []
    if single_seq:
        tots = [cum[j * C + C - 1:j * C + C, :] for j in range(R)]
    lnw = lnw_ref[...]
    lnb = lnb_ref[...]

    def wkv_stream(j, hp):
        sl = pairs[hp]
        o, new = yield from _wkv_pair_chunk(d["r"][rows(j), sl], d["lw"][rows(j), sl], cum[rows(j), sl],
                                            jnp.exp(tots[j][:, sl]), d["kmod"][rows(j), sl], d["v"][rows(j), sl],
                                            d["kk"][rows(j), sl], d["a"][rows(j), sl], wkv_sc[j, hp], mk, Ls)
        for s in range(n_seq):
            wkv_sc[j, hp, blk(s), :] = new[s]
        return o

    def hgrn_stream(j, h):
        sl = slice(RWKV_WIDTH + h * LANES, RWKV_WIDTH + (h + 1) * LANES)
        hs = blk(h)
        o, new = yield from _hgrn_head_chunk(d["q"][rows(j), hs], d["kf"][rows(j), hs], cum[rows(j), sl],
                                             tots[j][:, sl], d["iv"][rows(j), hs], hg_sc[j, h], mk, Ls)
        for s in range(n_seq):
            hg_sc[j, h, blk(s), :] = new[s]
        on = _rms(o, gn_ref[...])
        y_ref[j, :, sl] = (on * _silu(d["og"][rows(j), hs])).astype(y_ref.dtype)

    outs = _lockstep([wkv_stream(j, hp) for j in range(R) for hp in range(NP)]
                     + [hgrn_stream(j, h) for j in range(R) for h in range(HGRN_HEADS)] + extra)

    os_ = [jnp.concatenate([outs[j * NP + hp] for j in range(R)], axis=0) for hp in range(NP)]
    means = _group_sums(os_, gsum)
    ds = [o - mu * (1.0 / 64.0) for o, mu in zip(os_, means)]
    vars_ = _group_sums([dd * dd for dd in ds], gsum)
    for hp, sl in enumerate(pairs):
        on = ds[hp] * lax.rsqrt(vars_[hp] * (1.0 / 64.0) + LNX_EPS) * lnw[:, sl] + lnb[:, sl]
        yv = ((on + d["bonus"][:, sl]) * d["g"][:, sl]).astype(y_ref.dtype)
        for j in range(R):
            y_ref[j, :, sl] = yv[rows(j)]

    if fuse_in:
        hand_over(*outs[-1])

    @pl.when(c == pl.num_programs(1) - 1)
    def _():
        for j in range(R):
            for s in range(n_seq):
                for hp in range(NP):
                    wkvout_ref[j * n_seq + s, 2 * hp] = wkv_sc[j, hp, s * LANES:s * LANES + 64, 0:64]
                    wkvout_ref[j * n_seq + s, 2 * hp + 1] = wkv_sc[j, hp, s * LANES + 64:(s + 1) * LANES, 64:128]
                for h in range(HGRN_HEADS):
                    hgout_ref[j * n_seq + s, h] = hg_sc[j, h, blk(s), :].T


def _out_body(rep, x_ref, ym_ref, gt1_ref, sh2_ref, sc2_ref, gt2_ref, n2_ref, nf_ref,
              wo_ref, wu_ref, wd_ref, o_ref):
    gt1, sh2, sc2, gt2 = (_repeat_rows(m[...], rep) for m in (gt1_ref, sh2_ref, sc2_ref, gt2_ref))
    y = jnp.dot(ym_ref[...], wo_ref[...], preferred_element_type=F32)
    x1 = x_ref[...] + gt1 * y
    h = (_rms(x1, n2_ref[...]) * (1.0 + sc2) + sh2).astype(BF16)
    acc = jnp.zeros_like(x1)
    FC = 1024
    for j in range(D_FF // FC):
        u = jnp.dot(h, wu_ref[:, j * FC:(j + 1) * FC], preferred_element_type=F32)
        u = jnp.square(jnp.maximum(u, 0.0)).astype(BF16)
        acc = acc + jnp.dot(u, wd_ref[j * FC:(j + 1) * FC, :].astype(BF16), preferred_element_type=F32)
    x2 = x1 + gt2 * acc
    o_ref[...] = _rms(x2, nf_ref[...])


def _params(sem):
    return pltpu.CompilerParams(dimension_semantics=sem, vmem_limit_bytes=VMEM_LIMIT)


def _full(shape):
    return pl.BlockSpec(shape, lambda *_: (0,) * len(shape))


def _single(shape):
    return pl.BlockSpec(shape, lambda *_: (0,) * len(shape), pipeline_mode=pl.Buffered(1))


def _ada(c_all, w_ada, b_ada):
    n = c_all.shape[0]
    TN = 512
    return pl.pallas_call(
        _ada_body,
        grid=(w_ada.shape[1] // TN,),
        in_specs=[pl.BlockSpec((n, D_MODEL), lambda j: (0, 0)),
                  pl.BlockSpec((D_MODEL, TN), lambda j: (0, j)),
                  pl.BlockSpec((1, TN), lambda j: (0, j))],
        out_specs=pl.BlockSpec((n, TN), lambda j: (0, j)),
        out_shape=jax.ShapeDtypeStruct((n, w_ada.shape[1]), F32),
        compiler_params=_params(("arbitrary",)),
        name="ada",
    )(c_all, w_ada, b_ada)


def _mod_specs(mod, cols, TM, rows_per_mod):
    if mod.ndim == 3:
        return [pl.BlockSpec((None, 1, D_MODEL), lambda i, c=c: (i * TM // rows_per_mod, 0, c)) for c in cols]
    return [pl.BlockSpec((TM // rows_per_mod, D_MODEL), lambda i, c=c: (i, c)) for c in cols]


def _mod_repeat(mod, rows_per_mod):
    return 1 if mod.ndim == 3 else rows_per_mod


def _inproj(x, mod, norm1, w_in, TM, rows_per_mod):
    M = x.shape[0]
    return pl.pallas_call(
        functools.partial(_inproj_body, _mod_repeat(mod, rows_per_mod)),
        grid=(M // TM,),
        in_specs=[pl.BlockSpec((TM, D_MODEL), lambda i: (i, 0))]
                 + _mod_specs(mod, (0, 1), TM, rows_per_mod)
                 + [_full((1, D_MODEL)), _full((D_MODEL, IN_WIDTH))],
        out_specs=pl.BlockSpec((TM, IN_WIDTH), lambda i: (i, 0)),
        out_shape=jax.ShapeDtypeStruct((M, IN_WIDTH), F32),
        compiler_params=_params(("arbitrary",)),
        name="inproj",
    )(x, mod, mod, norm1, w_in)


def _rec(p, shinit, states, small, wcomb, wgate, Ls, n_batch, R, inproj=None):
    Gt, C = shinit.shape[0], CHUNK
    rows_total = (p if inproj is None else inproj[0]).shape[1]
    NC = rows_total // C
    n_seq = C // Ls
    wkv_spec = pl.BlockSpec((R * n_seq, RWKV_HEADS, 64, 64), lambda g, c: (g, 0, 0, 0))
    hg_spec = pl.BlockSpec((R * n_seq, HGRN_HEADS, LANES, LANES), lambda g, c: (g, 0, 0, 0))
    sh_spec = pl.BlockSpec((R, C, SHIFT_WIDTH), lambda g, c: (g, 0, 0))
    st_in = [] if states is None else list(states)
    st_specs = [] if states is None else [wkv_spec, hg_spec]
    scratch = [pltpu.VMEM((R, RWKV_HEADS // 2, n_seq * LANES, LANES), F32),
               pltpu.VMEM((R, HGRN_HEADS, n_seq * LANES, LANES), F32)]
    if inproj is None:
        lead_in = [p]
        lead_specs = [pl.BlockSpec((R, C, IN_WIDTH), lambda g, c: (g, c, 0))]
    else:
        x, mod, norm1, w_in = inproj
        lead_in = [x, x, mod, mod, norm1, w_in]
        lead_specs = [pl.BlockSpec((R, C, D_MODEL), lambda g, c: (g, 0, 0)),
                      pl.BlockSpec((R, C, D_MODEL), lambda g, c: (g, jnp.minimum(c + 1, NC - 1), 0)),
                      pl.BlockSpec((R, 1, D_MODEL), lambda g, c: (g, 0, 0)),
                      pl.BlockSpec((R, 1, D_MODEL), lambda g, c: (g, 0, 1)),
                      _full(norm1.shape), _full(w_in.shape)]
        scratch += [pltpu.VMEM((R * C, RWKV_WIDTH), F32) for _ in PREP_FIELDS]
        scratch.append(pltpu.VMEM((R * C, 2 * RWKV_WIDTH), F32))
    return pl.pallas_call(
        functools.partial(_rec_body, Ls, states is None, R, inproj is not None),
        grid=(Gt // R, NC),
        in_specs=lead_specs + [sh_spec] + st_specs
                 + [_full(s.shape) for s in small] + [_full(wcomb.shape), _full(wgate.shape)],
        out_specs=[pl.BlockSpec((R, C, D_MODEL), lambda g, c: (g, c, 0)), sh_spec, wkv_spec, hg_spec],
        out_shape=[jax.ShapeDtypeStruct((Gt, NC * C, D_MODEL), BF16),
                   jax.ShapeDtypeStruct(shinit.shape, F32),
                   jax.ShapeDtypeStruct((n_batch, RWKV_HEADS, 64, 64), F32),
                   jax.ShapeDtypeStruct((n_batch, HGRN_HEADS, LANES, LANES), F32)],
        scratch_shapes=scratch,
        compiler_params=_params(("arbitrary", "arbitrary")),
        name="rec",
    )(*lead_in, shinit, *st_in, *small, wcomb, wgate)


def _out(x, ym, mod, norm2, norm_f, w_out, w_up, w_down, TM, rows_per_mod):
    M = x.shape[0]
    row = lambda i: (i, 0)
    return pl.pallas_call(
        functools.partial(_out_body, _mod_repeat(mod, rows_per_mod)),
        grid=(M // TM,),
        in_specs=[pl.BlockSpec((TM, D_MODEL), row), pl.BlockSpec((TM, D_MODEL), row)]
                 + _mod_specs(mod, (2, 3, 4, 5), TM, rows_per_mod)
                 + [_full((1, D_MODEL)), _full((1, D_MODEL)),
                    _full(w_out.shape), _full(w_up.shape),
                    _single(w_down.shape)],
        out_specs=pl.BlockSpec((TM, D_MODEL), row),
        out_shape=jax.ShapeDtypeStruct((M, D_MODEL), F32),
        compiler_params=_params(("arbitrary",)),
        name="outmlp",
    )(x, ym, mod, mod, mod, mod, norm2, norm_f, w_out, w_up, w_down)


def kernel(x_prompt, x_sample, c_prompt, c_sample, state_shift, state_wkv, state_hgrn, norm1, norm2, norm_f, w_ada, b_ada, w_in, mu_shift, w0, w_decay_up, a0, w_aaa_up, w_gate_up, k_k, k_a, r_k, lnx_w, lnx_b, hgrn_lb, hgrn_gnorm, w_out, w_up, w_down):
    BP, TP, _ = x_prompt.shape
    BS, TS, _ = x_sample.shape
    l = 0
    row = lambda t: t.reshape(1, -1)

    mod = _ada(jnp.concatenate([c_prompt, c_sample], axis=0), w_ada[l], row(b_ada[l]))
    mod_p = mod[:BP].reshape(BP, 1, 6 * D_MODEL)
    mod_s = mod[BP:]

    w_in_b = w_in[l].astype(BF16)
    w_out_b = w_out[l].astype(BF16)
    w_up_b = w_up[l].astype(BF16)
    w_down_b = w_down[l]
    zer = jnp.zeros((64, RWKV_WIDTH), F32)
    wcomb = jnp.concatenate([jnp.concatenate([w_decay_up[l], zer], axis=1),
                             jnp.concatenate([zer, w_aaa_up[l]], axis=1)], axis=0).astype(BF16)
    wgate = w_gate_up[l].astype(BF16)
    small = [row(mu_shift[l]), row(w0[l]), row(a0[l]), row(k_k[l]), row(k_a[l]), row(r_k[l]),
             row(lnx_w[l]), row(lnx_b[l]), hgrn_lb, row(hgrn_gnorm[l])]
    n1, n2, nf = row(norm1[l]), row(norm2[l]), row(norm_f)

    def trunk(x2d, modx, rows_per_mod, shinit, states, Ls, n_batch, TM, R, fuse_in):
        Gt = shinit.shape[0]
        if fuse_in:
            ym, shout, wkv_o, hg_o = _rec(None, shinit, states, small, wcomb, wgate, Ls, n_batch, R,
                                          inproj=(x2d.reshape(Gt, -1, D_MODEL), modx, n1, w_in_b))
        else:
            p = _inproj(x2d, modx, n1, w_in_b, TM, rows_per_mod)
            ym, shout, wkv_o, hg_o = _rec(p.reshape(Gt, -1, IN_WIDTH), shinit, states, small, wcomb, wgate,
                                          Ls, n_batch, R)
        y = _out(x2d, ym.reshape(-1, D_MODEL), modx, n2, nf, w_out_b, w_up_b, w_down_b, TM, rows_per_mod)
        return y, shout, wkv_o, hg_o

    yp, shp, wkvp, hgp = trunk(x_prompt.reshape(BP * TP, D_MODEL), mod_p, TP,
                               jnp.zeros((BP, CHUNK, SHIFT_WIDTH), F32), None, CHUNK, BP, 512, PROMPT_GROUPS_PER_STEP, True)
    y_prompt = yp.reshape(BP, TP, D_MODEL)
    shift_p = shp[:, CHUNK - 1][None]
    wkv_p = wkvp[None]
    hgrn_p = hgp[None]

    n_seq = CHUNK // TS
    GS = BS // n_seq
    shinit_s = jnp.zeros((GS, n_seq, TS, SHIFT_WIDTH), F32).at[:, :, 0].set(
        state_shift[l].reshape(GS, n_seq, SHIFT_WIDTH)).reshape(GS, CHUNK, SHIFT_WIDTH)
    ys, shs, wkvs, hgs = trunk(x_sample.reshape(BS * TS, D_MODEL), mod_s, TS, shinit_s,
                               (state_wkv[l], state_hgrn[l]), TS, BS, 256, 1, False)
    y_sample = ys.reshape(BS, TS, D_MODEL)
    shift_s = shs.reshape(GS, n_seq, TS, SHIFT_WIDTH)[:, :, TS - 1].reshape(BS, SHIFT_WIDTH)[None]
    wkv_s = wkvs[None]
    hgrn_s = hgs[None]

    return (y_prompt, y_sample, shift_p, wkv_p, hgrn_p, shift_s, wkv_s, hgrn_s)
```

```python
import functools

import jax
import jax.numpy as jnp
from jax import lax
from jax.experimental import pallas as pl
from jax.experimental.pallas import tpu as pltpu

F32 = jnp.float32
BF16 = jnp.bfloat16

D_MODEL = 1024
RWKV_HEADS = 8
RWKV_WIDTH = 512
HGRN_HEADS = 4
HGRN_WIDTH = 512
SHIFT_WIDTH = 1792
IN_WIDTH = 3840
D_FF = 4096
NORM_EPS = 1e-6
LNX_EPS = 64e-5

LANES = 128
HEAD_PAIR = 128
CHUNK = 64
SUB = 16
PROMPT_GROUPS_PER_STEP = 8
INPROJ_TILE = 256
VMEM_LIMIT = 62 * 1024 * 1024


def _dot(a, b):
    return jnp.dot(a.astype(BF16), b.astype(BF16), preferred_element_type=F32)


def _dot_nt(a, b):
    return lax.dot_general(a.astype(BF16), b.astype(BF16), (((1,), (1,)), ((), ())),
                           preferred_element_type=F32)


def _dot_2pass_l(m, x):
    mb = m.astype(BF16)
    hi = x.astype(BF16)
    lo = (x - hi.astype(F32)).astype(BF16)
    return jnp.dot(mb, hi, preferred_element_type=F32) + jnp.dot(mb, lo, preferred_element_type=F32)


def _repeat_rows(m, rep):
    if rep == 1:
        return m
    n = m.shape[0]
    sel = (_shr(_iota((n * rep, n), 0), rep) == _iota((n * rep, n), 1)).astype(BF16)
    hi = m.astype(BF16)
    r1 = m - hi.astype(F32)
    mid = r1.astype(BF16)
    lo = (r1 - mid.astype(F32)).astype(BF16)
    d = lambda part: jnp.dot(sel, part, preferred_element_type=F32)
    return d(hi) + d(mid) + d(lo)


def _group_sums(xs, gmat):
    n, rows = len(xs), xs[0].shape[0]
    P = _dot(jnp.concatenate(xs, axis=0), gmat)
    return [P[i * rows:(i + 1) * rows] for i in range(n)]


def _iota(shape, dim):
    return lax.broadcasted_iota(jnp.int32, shape, dim)


def _shr(x, n):
    return lax.shift_right_logical(x, jnp.int32(n.bit_length() - 1))


def _sigmoid(x):
    return 1.0 / (1.0 + jnp.exp(-x))


def _silu(x):
    return x * _sigmoid(x)


def _softplus(x):
    return jnp.maximum(x, 0.0) + jnp.log(1.0 + jnp.exp(-jnp.abs(x)))


def _rms(x, gain):
    return x * lax.rsqrt(jnp.mean(x * x, axis=-1, keepdims=True) + NORM_EPS) * gain


def _make_masks(C, Ls):
    row = _iota((C, C), 0)
    col = _iota((C, C), 1)
    seq = lambda t: _shr(t, Ls)
    blk = lambda t: _shr(t, SUB)
    same = seq(row) == seq(col)
    R = _iota((2 * C, 2 * C), 0)
    Q = _iota((2 * C, 2 * C), 1)
    tr = R & (C - 1)
    tq = Q & (C - 1)
    same2 = seq(tr) == seq(tq)
    lvl = []
    s = 1
    while s < Ls:
        lvl.append((_shr(R, 2 * s) == _shr(Q, 2 * s)) & ((R & (2 * s - 1)) >= s) & ((Q & (2 * s - 1)) < s))
        s *= 2
    return dict(
        row=row, col=col, same=same,
        mi=same & (col <= row),
        sameblk=blk(row) == blk(col),
        eye2=(R == Q).astype(F32),
        lvl=lvl,
        ms_hi2=same2 & (tq < tr) & (Q >= C),
        mi22=same2 & (tq <= tr),
    )


def _tri_inverse(G, mk):
    lvl = mk['lvl']
    D = mk['eye2'] + jnp.where(lvl[0], G, 0.0)
    for m in lvl[1:]:
        LD = _dot(jnp.where(m, G, 0.0), D)
        yield
        D = D + _dot(D, LD)
        yield
    return D


def _lockstep(streams):
    streams = list(streams)
    out = [None] * len(streams)
    alive = list(range(len(streams)))
    while alive:
        for i in list(alive):
            try:
                next(streams[i])
            except StopIteration as stop:
                out[i] = stop.value
                alive.remove(i)
    return out


def _drain(gen):
    while True:
        try:
            next(gen)
        except StopIteration as stop:
            return stop.value


def _select_seq(full, n_seq, sid_rows):
    if n_seq == 1:
        return full
    acc = jnp.where(sid_rows == 0, full[:, 0:LANES], 0.0)
    for s in range(1, n_seq):
        acc = acc + jnp.where(sid_rows == s, full[:, s * LANES:(s + 1) * LANES], 0.0)
    return acc


def _expand_seq(x, n_seq, sid_rows):
    if n_seq == 1:
        return x
    return jnp.concatenate([jnp.where(sid_rows == s, x, 0.0) for s in range(n_seq)], axis=1)


def _wkv_pair_chunk(r, lw, cw, wl, k, v, kk, a, S, mk, Ls):
    C = r.shape[0]
    n_seq = C // Ls
    lo_half = _iota((C, HEAD_PAIR), 1) < 64
    own = (_iota((2 * C, HEAD_PAIR), 1) < 64) == (_iota((2 * C, HEAD_PAIR), 0) < C)
    stack = lambda t: jnp.where(own, jnp.concatenate([t, t], axis=0), 0.0)
    e_neg = jnp.exp(-cw)
    at = -kk * jnp.exp(cw - lw)
    bt = kk * a * e_neg
    kt = k * e_neg
    rt = r * jnp.exp(cw)
    Y = jnp.concatenate([bt, kt], axis=0)
    gar = _dot_nt(jnp.concatenate([stack(at), stack(rt)], axis=0), Y)
    ga = gar[:2 * C]
    first_cols = _iota((2 * C, 2 * C), 1) < C
    top_rows = _iota((2 * C, 2 * C), 0) < C
    G = (jnp.where(first_cols & top_rows, ga, 0.0)
         + pltpu.roll(jnp.where(first_cols & jnp.logical_not(top_rows), ga, 0.0), C, 1))
    gr = gar[2 * C:]
    sid2 = _shr(_iota((2 * C, LANES), 0) & (C - 1), Ls)
    XO = _select_seq(_dot_nt(jnp.concatenate([at, rt], axis=0), S), n_seq, sid2)
    yield
    Vz = jnp.concatenate([jnp.zeros_like(v), v], axis=0)
    xk = _dot(jnp.where(mk['ms_hi2'], ga, 0.0), Vz)
    D = yield from _tri_inverse(G, mk)
    X = XO[:C] + jnp.where(lo_half, xk[:C], xk[C:])
    DX = _dot(D, jnp.concatenate([X, X], axis=0))
    yield
    U = jnp.where(lo_half, DX[:C], DX[C:])
    Z = jnp.concatenate([U, v], axis=0)
    oz = _dot(jnp.where(mk['mi22'], gr, 0.0), Z)
    upd = _dot(Z.T, _expand_seq(Y, n_seq, sid2))
    yield
    o = XO[C:] + jnp.where(lo_half, oz[:C], oz[C:])
    bd = _shr(_iota((HEAD_PAIR, HEAD_PAIR), 0), 64) == _shr(_iota((HEAD_PAIR, HEAD_PAIR), 1), 64)
    new = []
    for s in range(n_seq):
        Ss = S[s * LANES:(s + 1) * LANES, :]
        new.append(jnp.where(bd, (Ss + upd[:, s * LANES:(s + 1) * LANES]) * wl[s * Ls:s * Ls + 1, :], 0.0))
    return o, new


def _hgrn_head_chunk(q, kf, b, blast, iv, ST, mk, Ls):
    C = q.shape[0]
    assert Ls == C or Ls <= SUB
    n_seq = C // Ls
    sid = _shr(_iota((C, LANES), 0), Ls)
    if Ls > SUB:
        rows_of = lambda off: jnp.concatenate(
            [jnp.broadcast_to(b[SUB * j + off:SUB * j + off + 1, :], (SUB, LANES)) for j in range(C // SUB)], axis=0)
        bmid = rows_of(SUB // 2 - 1)
        bend = rows_of(SUB - 1)
    else:
        bmid = 0.0
    o = _select_seq(_dot_nt(q * jnp.exp(b), ST), n_seq, sid)
    att = jnp.where(mk['mi'] & mk['sameblk'],
                    _dot_nt(q * jnp.exp(b - bmid), kf * jnp.exp(bmid - b)), 0.0)
    if Ls > SUB:
        ko = kf * jnp.exp(jnp.minimum(bend - b, 0.0))
        rb = _shr(mk['row'], SUB)
        cb = _shr(mk['col'], SUB)
        nb = C // SUB - 1
        qo = [q * jnp.exp(jnp.minimum(b - b[SUB * j + SUB - 1:SUB * j + SUB, :], 0.0)) for j in range(nb)]
        off = _dot_nt(jnp.concatenate(qo, axis=0), ko)
        for j in range(nb):
            att = att + jnp.where(mk['same'] & (cb == j) & (rb > j), off[j * C:(j + 1) * C], 0.0)
    ke = kf * jnp.exp(blast - b)
    upd = _dot(iv.T, _expand_seq(ke, n_seq, sid))
    yield
    o = o + _dot(att, iv)
    yield
    dec = jnp.exp(blast)
    new = []
    for s in range(n_seq):
        new.append(ST[s * LANES:(s + 1) * LANES, :] * dec[s * Ls:s * Ls + 1, :]
                   + upd[:, s * LANES:(s + 1) * LANES])
    return o, new


def _ada_body(c_ref, w_ref, b_ref, o_ref):
    o_ref[...] = _dot(_silu(c_ref[...]), w_ref[...]) + b_ref[...]


def _inproj_body(rep, x_ref, sh_ref, sc_ref, n1_ref, w_ref, p_ref):
    h = _rms(x_ref[...], n1_ref[...]) * (1.0 + _repeat_rows(sc_ref[...], rep)) + _repeat_rows(sh_ref[...], rep)
    p_ref[...] = jnp.dot(h.astype(BF16), w_ref[...], preferred_element_type=F32)


PREP_FIELDS = ("r", "lw", "kmod", "v", "kk", "a", "g", "bonus", "q", "kf", "iv", "og")


def _rec_body(Ls, single_seq, R, fuse_in, *refs):
    refs = list(refs)
    if fuse_in:
        xfirst_ref, xnext_ref, sh1_ref, sc1_ref, n1_ref, win_ref = refs[:6]
        del refs[:6]
        p_ref = None
    else:
        p_ref = refs.pop(0)
    if single_seq:
        shinit_ref = wkvin_ref = hgin_ref = None
    else:
        shinit_ref, wkvin_ref, hgin_ref = refs[:3]
        del refs[:3]
    (mu_ref, w0_ref, a0_ref, kk_ref, ka_ref, rk_ref, lnw_ref, lnb_ref, lb_ref, gn_ref, wcomb_ref, wgate_ref,
     y_ref, shout_ref, wkvout_ref, hgout_ref, wkv_sc, hg_sc) = refs[:18]
    if fuse_in:
        assert single_seq
        prep_sc = dict(zip(PREP_FIELDS, refs[18:18 + len(PREP_FIELDS)]))
        cum_sc = refs[18 + len(PREP_FIELDS)]
    C = CHUNK
    n_seq = C // Ls
    NP = RWKV_HEADS // 2
    c = pl.program_id(1)
    rows = lambda j: slice(j * C, (j + 1) * C)
    blk = lambda s: slice(s * LANES, (s + 1) * LANES)
    pairs = [slice(hp * HEAD_PAIR, (hp + 1) * HEAD_PAIR) for hp in range(NP)]
    mk = _make_masks(C, Ls)
    gsum = (_shr(_iota((HEAD_PAIR, HEAD_PAIR), 0), 64) == _shr(_iota((HEAD_PAIR, HEAD_PAIR), 1), 64)).astype(F32)

    def prepare(pcols, first):
        rowv = _iota((C, SHIFT_WIDTH), 0)
        xs = []
        for j in range(R):
            p_rw = pcols(j, 0, SHIFT_WIDTH)
            keep = shout_ref.shape[1]
            p_prev = jnp.where(rowv == 0, shout_ref[j, keep - 1:keep, :], pltpu.roll(p_rw, 1, 0))
            if not single_seq:
                p_prev = jnp.where((rowv & (Ls - 1)) == 0, shinit_ref[j], p_prev)
            elif first:
                p_prev = jnp.where(rowv == 0, 0.0, p_prev)
            shout_ref[j] = p_rw[C - keep:, :]
            xs.append(p_rw + (p_prev - p_rw) * mu_ref[...])
        x = jnp.concatenate(xs, axis=0)
        cat = lambda lo, hi: jnp.concatenate([pcols(j, lo, hi) for j in range(R)], axis=0)
        r = x[:, 0:512]
        k = x[:, 512:1024]
        v = x[:, 1024:1536]
        wa = x[:, 1536:1664]
        gd = x[:, 1664:1792]
        lane = _iota((R * C, LANES), 1)
        da = _dot(jnp.where(lane < 64, jnp.tanh(wa), wa), wcomb_ref[...])
        g = _dot(_sigmoid(gd), wgate_ref[...])
        yield
        w_log = -_softplus(-(w0_ref[...] + da[:, :512])) - 0.5
        lw = -jnp.exp(w_log)
        a = _sigmoid(a0_ref[...] + da[:, 512:])
        kkr = k * kk_ref[...]
        kmod = k * (1.0 + (a - 1.0) * ka_ref[...])
        rkr = r * kmod * rk_ref[...]
        lbp = lb_ref[...]
        m = jnp.maximum(lbp[0:1, :], lbp[1:2, :])
        e0 = jnp.exp(lbp[0:1, :] - m)
        e1 = jnp.exp(lbp[1:2, :] - m)
        lb = e0 / (e0 + e1)
        q = _silu(cat(1792, 2304))
        f = lb + (1.0 - lb) * _sigmoid(cat(2304, 2816))
        logs = jnp.concatenate([lw, jnp.log(f)], axis=1)
        cums, tots = [], []
        for j in range(R):
            if single_seq:
                cums.append(_dot_2pass_l(mk['mi'].astype(F32), logs[rows(j)]))
            else:
                both = _dot_2pass_l(jnp.concatenate([mk['mi'], mk['same']], axis=0).astype(F32), logs[rows(j)])
                cums.append(both[:C])
                tots.append(both[C:])
        sums = _group_sums([kkr[:, sl] * kkr[:, sl] for sl in pairs] + [rkr[:, sl] for sl in pairs], gsum)
        yield
        kk = jnp.concatenate([kkr[:, sl] / jnp.maximum(jnp.sqrt(sums[hp]), 1e-12)
                              for hp, sl in enumerate(pairs)], axis=1)
        bonus = jnp.concatenate(sums[NP:], axis=1) * v
        d = dict(r=r, lw=lw, kmod=kmod, v=v, kk=kk, a=a, g=g, bonus=bonus, q=q, kf=1.0 - f,
                 iv=cat(2816, 3328), og=cat(3328, 3840))
        return d, jnp.concatenate(cums, axis=0), tots

    def next_chunk(x_ref, first):
        hs = [_rms(x_ref[j], n1_ref[...]) * (1.0 + sc1_ref[j]) + sh1_ref[j] for j in range(R)]
        hn = jnp.concatenate(hs, axis=0).astype(BF16)
        tiles = []
        for t in range(IN_WIDTH // INPROJ_TILE):
            tiles.append(jnp.dot(hn, win_ref[:, t * INPROJ_TILE:(t + 1) * INPROJ_TILE],
                                 preferred_element_type=F32))
            yield
        pcols = lambda j, lo, hi: jnp.concatenate(tiles[lo // INPROJ_TILE:hi // INPROJ_TILE], axis=1)[rows(j)]
        d, cum, _ = yield from prepare(pcols, first)
        return d, cum

    def hand_over(d, cum):
        for name in PREP_FIELDS:
            prep_sc[name][...] = d[name]
        cum_sc[...] = cum

    @pl.when(c == 0)
    def _():
        shout_ref[...] = jnp.zeros_like(shout_ref)
        wkv_sc[...] = jnp.zeros_like(wkv_sc)
        if single_seq:
            hg_sc[...] = jnp.zeros_like(hg_sc)
        else:
            for j in range(R):
                for s in range(n_seq):
                    for hp in range(NP):
                        wkv_sc[j, hp, s * LANES:s * LANES + 64, 0:64] = wkvin_ref[j * n_seq + s, 2 * hp]
                        wkv_sc[j, hp, s * LANES + 64:(s + 1) * LANES, 64:128] = wkvin_ref[j * n_seq + s, 2 * hp + 1]
                    for h in range(HGRN_HEADS):
                        hg_sc[j, h, blk(s), :] = hgin_ref[j * n_seq + s, h].T
        if fuse_in:
            hand_over(*_drain(next_chunk(xfirst_ref, True)))

    if fuse_in:
        d = {name: prep_sc[name][...] for name in PREP_FIELDS}
        cum = cum_sc[...]
        extra = [next_chunk(xnext_ref, False)]
    else:
        d, cum, tots = _drain(prepare(lambda j, lo, hi: p_ref[j, :, lo:hi], True))
        extra = []
    if single_seq:
        tots = [cum[j * C + C - 1:j * C + C, :] for j in range(R)]
    lnw = lnw_ref[...]
    lnb = lnb_ref[...]

    def wkv_stream(j, hp):
        sl = pairs[hp]
        o, new = yield from _wkv_pair_chunk(d["r"][rows(j), sl], d["lw"][rows(j), sl], cum[rows(j), sl],
                                            jnp.exp(tots[j][:, sl]), d["kmod"][rows(j), sl], d["v"][rows(j), sl],
                                            d["kk"][rows(j), sl], d["a"][rows(j), sl], wkv_sc[j, hp], mk, Ls)
        for s in range(n_seq):
            wkv_sc[j, hp, blk(s), :] = new[s]
        return o

    def hgrn_stream(j, h):
        sl = slice(RWKV_WIDTH + h * LANES, RWKV_WIDTH + (h + 1) * LANES)
        hs = blk(h)
        o, new = yield from _hgrn_head_chunk(d["q"][rows(j), hs], d["kf"][rows(j), hs], cum[rows(j), sl],
                                             tots[j][:, sl], d["iv"][rows(j), hs], hg_sc[j, h], mk, Ls)
        for s in range(n_seq):
            hg_sc[j, h, blk(s), :] = new[s]
        on = _rms(o, gn_ref[...])
        y_ref[j, :, sl] = (on * _silu(d["og"][rows(j), hs])).astype(y_ref.dtype)

    outs = _lockstep([wkv_stream(j, hp) for j in range(R) for hp in range(NP)]
                     + [hgrn_stream(j, h) for j in range(R) for h in range(HGRN_HEADS)] + extra)

    os_ = [jnp.concatenate([outs[j * NP + hp] for j in range(R)], axis=0) for hp in range(NP)]
    means = _group_sums(os_, gsum)
    ds = [o - mu * (1.0 / 64.0) for o, mu in zip(os_, means)]
    vars_ = _group_sums([dd * dd for dd in ds], gsum)
    for hp, sl in enumerate(pairs):
        on = ds[hp] * lax.rsqrt(vars_[hp] * (1.0 / 64.0) + LNX_EPS) * lnw[:, sl] + lnb[:, sl]
        yv = ((on + d["bonus"][:, sl]) * d["g"][:, sl]).astype(y_ref.dtype)
        for j in range(R):
            y_ref[j, :, sl] = yv[rows(j)]

    if fuse_in:
        hand_over(*outs[-1])

    @pl.when(c == pl.num_programs(1) - 1)
    def _():
        for j in range(R):
            for s in range(n_seq):
                for hp in range(NP):
                    wkvout_ref[j * n_seq + s, 2 * hp] = wkv_sc[j, hp, s * LANES:s * LANES + 64, 0:64]
                    wkvout_ref[j * n_seq + s, 2 * hp + 1] = wkv_sc[j, hp, s * LANES + 64:(s + 1) * LANES, 64:128]
                for h in range(HGRN_HEADS):
                    hgout_ref[j * n_seq + s, h] = hg_sc[j, h, blk(s), :].T


def _out_body(rep, x_ref, ym_ref, gt1_ref, sh2_ref, sc2_ref, gt2_ref, n2_ref, nf_ref,
              wo_ref, wu_ref, wd_ref, o_ref):
    gt1, sh2, sc2, gt2 = (_repeat_rows(m[...], rep) for m in (gt1_ref, sh2_ref, sc2_ref, gt2_ref))
    y = jnp.dot(ym_ref[...], wo_ref[...], preferred_element_type=F32)
    x1 = x_ref[...] + gt1 * y
    h = (_rms(x1, n2_ref[...]) * (1.0 + sc2) + sh2).astype(BF16)
    acc = jnp.zeros_like(x1)
    FC = 1024
    for j in range(D_FF // FC):
        u = jnp.dot(h, wu_ref[:, j * FC:(j + 1) * FC], preferred_element_type=F32)
        u = jnp.square(jnp.maximum(u, 0.0)).astype(BF16)
        acc = acc + jnp.dot(u, wd_ref[j * FC:(j + 1) * FC, :], preferred_element_type=F32)
    x2 = x1 + gt2 * acc
    o_ref[...] = _rms(x2, nf_ref[...])


def _params(sem):
    return pltpu.CompilerParams(dimension_semantics=sem, vmem_limit_bytes=VMEM_LIMIT)


def _full(shape):
    return pl.BlockSpec(shape, lambda *_: (0,) * len(shape))


def _ada(c_all, w_ada, b_ada):
    n = c_all.shape[0]
    TN = 1024
    return pl.pallas_call(
        _ada_body,
        grid=(w_ada.shape[1] // TN,),
        in_specs=[pl.BlockSpec((n, D_MODEL), lambda j: (0, 0)),
                  pl.BlockSpec((D_MODEL, TN), lambda j: (0, j)),
                  pl.BlockSpec((1, TN), lambda j: (0, j))],
        out_specs=pl.BlockSpec((n, TN), lambda j: (0, j)),
        out_shape=jax.ShapeDtypeStruct((n, w_ada.shape[1]), F32),
        compiler_params=_params(("arbitrary",)),
        name="ada",
    )(c_all, w_ada, b_ada)


def _mod_specs(mod, cols, TM, rows_per_mod):
    if mod.ndim == 3:
        return [pl.BlockSpec((None, 1, D_MODEL), lambda i, c=c: (i * TM // rows_per_mod, 0, c)) for c in cols]
    return [pl.BlockSpec((TM // rows_per_mod, D_MODEL), lambda i, c=c: (i, c)) for c in cols]


def _mod_repeat(mod, rows_per_mod):
    return 1 if mod.ndim == 3 else rows_per_mod


def _inproj(x, mod, norm1, w_in, TM, rows_per_mod):
    M = x.shape[0]
    return pl.pallas_call(
        functools.partial(_inproj_body, _mod_repeat(mod, rows_per_mod)),
        grid=(M // TM,),
        in_specs=[pl.BlockSpec((TM, D_MODEL), lambda i: (i, 0))]
                 + _mod_specs(mod, (0, 1), TM, rows_per_mod)
                 + [_full((1, D_MODEL)), _full((D_MODEL, IN_WIDTH))],
        out_specs=pl.BlockSpec((TM, IN_WIDTH), lambda i: (i, 0)),
        out_shape=jax.ShapeDtypeStruct((M, IN_WIDTH), F32),
        compiler_params=_params(("arbitrary",)),
        name="inproj",
    )(x, mod, mod, norm1, w_in)


def _rec(p, n_groups, states, small, wcomb, wgate, Ls, n_batch, R, inproj=None):
    Gt, C = n_groups, CHUNK
    rows_total = (p if inproj is None else inproj[0]).shape[1]
    NC = rows_total // C
    n_seq = C // Ls
    keep = 8 if states is None else C
    wkv_spec = pl.BlockSpec((R * n_seq, RWKV_HEADS, 64, 64), lambda g, c: (g, 0, 0, 0))
    hg_spec = pl.BlockSpec((R * n_seq, HGRN_HEADS, LANES, LANES), lambda g, c: (g, 0, 0, 0))
    sh_spec = pl.BlockSpec((R, C, SHIFT_WIDTH), lambda g, c: (g, 0, 0))
    sho_spec = pl.BlockSpec((R, keep, SHIFT_WIDTH), lambda g, c: (g, 0, 0))
    st_in = [] if states is None else list(states)
    st_specs = [] if states is None else [sh_spec, wkv_spec, hg_spec]
    scratch = [pltpu.VMEM((R, RWKV_HEADS // 2, n_seq * LANES, LANES), F32),
               pltpu.VMEM((R, HGRN_HEADS, n_seq * LANES, LANES), F32)]
    if inproj is None:
        lead_in = [p]
        lead_specs = [pl.BlockSpec((R, C, IN_WIDTH), lambda g, c: (g, c, 0))]
    else:
        x, mod, norm1, w_in = inproj
        lead_in = [x, x, mod, mod, norm1, w_in]
        lead_specs = [pl.BlockSpec((R, C, D_MODEL), lambda g, c: (g, 0, 0)),
                      pl.BlockSpec((R, C, D_MODEL), lambda g, c: (g, jnp.minimum(c + 1, NC - 1), 0)),
                      pl.BlockSpec((R, 1, D_MODEL), lambda g, c: (g, 0, 0)),
                      pl.BlockSpec((R, 1, D_MODEL), lambda g, c: (g, 0, 1)),
                      _full(norm1.shape), _full(w_in.shape)]
        scratch += [pltpu.VMEM((R * C, RWKV_WIDTH), F32) for _ in PREP_FIELDS]
        scratch.append(pltpu.VMEM((R * C, 2 * RWKV_WIDTH), F32))
    return pl.pallas_call(
        functools.partial(_rec_body, Ls, states is None, R, inproj is not None),
        grid=(Gt // R, NC),
        in_specs=lead_specs + st_specs
                 + [_full(s.shape) for s in small] + [_full(wcomb.shape), _full(wgate.shape)],
        out_specs=[pl.BlockSpec((R, C, D_MODEL), lambda g, c: (g, c, 0)), sho_spec, wkv_spec, hg_spec],
        out_shape=[jax.ShapeDtypeStruct((Gt, NC * C, D_MODEL), BF16),
                   jax.ShapeDtypeStruct((Gt, keep, SHIFT_WIDTH), F32),
                   jax.ShapeDtypeStruct((n_batch, RWKV_HEADS, 64, 64), F32),
                   jax.ShapeDtypeStruct((n_batch, HGRN_HEADS, LANES, LANES), F32)],
        scratch_shapes=scratch,
        compiler_params=_params(("arbitrary", "arbitrary")),
        name="rec",
    )(*lead_in, *st_in, *small, wcomb, wgate)


def _out(x, ym, mod, norm2, norm_f, w_out, w_up, w_down, TM, rows_per_mod):
    M = x.shape[0]
    row = lambda i: (i, 0)
    return pl.pallas_call(
        functools.partial(_out_body, _mod_repeat(mod, rows_per_mod)),
        grid=(M // TM,),
        in_specs=[pl.BlockSpec((TM, D_MODEL), row), pl.BlockSpec((TM, D_MODEL), row)]
                 + _mod_specs(mod, (2, 3, 4, 5), TM, rows_per_mod)
                 + [_full((1, D_MODEL)), _full((1, D_MODEL)),
                    _full(w_out.shape), _full(w_up.shape), _full(w_down.shape)],
        out_specs=pl.BlockSpec((TM, D_MODEL), row),
        out_shape=jax.ShapeDtypeStruct((M, D_MODEL), F32),
        compiler_params=_params(("arbitrary",)),
        name="outmlp",
    )(x, ym, mod, mod, mod, mod, norm2, norm_f, w_out, w_up, w_down)


def kernel(x_prompt, x_sample, c_prompt, c_sample, state_shift, state_wkv, state_hgrn, norm1, norm2, norm_f, w_ada, b_ada, w_in, mu_shift, w0, w_decay_up, a0, w_aaa_up, w_gate_up, k_k, k_a, r_k, lnx_w, lnx_b, hgrn_lb, hgrn_gnorm, w_out, w_up, w_down):
    BP, TP, _ = x_prompt.shape
    BS, TS, _ = x_sample.shape
    l = 0
    row = lambda t: t.reshape(1, -1)

    mod = _ada(jnp.concatenate([c_prompt, c_sample], axis=0), w_ada[l], row(b_ada[l]))
    mod_p = mod[:BP].reshape(BP, 1, 6 * D_MODEL)
    mod_s = mod[BP:]

    w_in_b = w_in[l].astype(BF16)
    w_out_b = w_out[l].astype(BF16)
    w_up_b = w_up[l].astype(BF16)
    w_down_b = w_down[l].astype(BF16)
    zer = jnp.zeros((64, RWKV_WIDTH), F32)
    wcomb = jnp.concatenate([jnp.concatenate([w_decay_up[l], zer], axis=1),
                             jnp.concatenate([zer, w_aaa_up[l]], axis=1)], axis=0).astype(BF16)
    wgate = w_gate_up[l].astype(BF16)
    small = [row(mu_shift[l]), row(w0[l]), row(a0[l]), row(k_k[l]), row(k_a[l]), row(r_k[l]),
             row(lnx_w[l]), row(lnx_b[l]), hgrn_lb, row(hgrn_gnorm[l])]
    n1, n2, nf = row(norm1[l]), row(norm2[l]), row(norm_f)

    def trunk(x2d, modx, rows_per_mod, Gt, states, Ls, n_batch, TM, R, fuse_in):
        if fuse_in:
            ym, shout, wkv_o, hg_o = _rec(None, Gt, states, small, wcomb, wgate, Ls, n_batch, R,
                                          inproj=(x2d.reshape(Gt, -1, D_MODEL), modx, n1, w_in_b))
        else:
            p = _inproj(x2d, modx, n1, w_in_b, TM, rows_per_mod)
            ym, shout, wkv_o, hg_o = _rec(p.reshape(Gt, -1, IN_WIDTH), Gt, states, small, wcomb, wgate,
                                          Ls, n_batch, R)
        y = _out(x2d, ym.reshape(-1, D_MODEL), modx, n2, nf, w_out_b, w_up_b, w_down_b, TM, rows_per_mod)
        return y, shout, wkv_o, hg_o

    yp, shp, wkvp, hgp = trunk(x_prompt.reshape(BP * TP, D_MODEL), mod_p, TP, BP, None, CHUNK, BP, 512,
                               PROMPT_GROUPS_PER_STEP, True)
    y_prompt = yp.reshape(BP, TP, D_MODEL)
    shift_p = shp[:, -1][None]
    wkv_p = wkvp[None]
    hgrn_p = hgp[None]

    n_seq = CHUNK // TS
    GS = BS // n_seq
    shinit_s = jnp.zeros((GS, n_seq, TS, SHIFT_WIDTH), F32).at[:, :, 0].set(
        state_shift[l].reshape(GS, n_seq, SHIFT_WIDTH)).reshape(GS, CHUNK, SHIFT_WIDTH)
    ys, shs, wkvs, hgs = trunk(x_sample.reshape(BS * TS, D_MODEL), mod_s, TS, GS,
                               (shinit_s, state_wkv[l], state_hgrn[l]), TS, BS, 256, 1, False)
    y_sample = ys.reshape(BS, TS, D_MODEL)
    shift_s = shs.reshape(GS, n_seq, TS, SHIFT_WIDTH)[:, :, TS - 1].reshape(BS, SHIFT_WIDTH)[None]
    wkv_s = wkvs[None]
    hgrn_s = hgs[None]

    return (y_prompt, y_sample, shift_p, wkv_p, hgrn_p, shift_s, wkv_s, hgrn_s)
```

```python
import functools

import jax
import jax.numpy as jnp
from jax import lax
from jax.experimental import pallas as pl
from jax.experimental.pallas import tpu as pltpu

F32 = jnp.float32
BF16 = jnp.bfloat16

D_MODEL = 1024
RWKV_HEADS = 8
RWKV_WIDTH = 512
HGRN_HEADS = 4
HGRN_WIDTH = 512
RWKV_HEAD_DIM = 64
D_DECAY_LORA = 64
D_AAA_LORA = 64
D_GATE_LORA = 128
SHIFT_WIDTH = 1792
IN_WIDTH = 3840
D_FF = 4096
COL_K = RWKV_WIDTH
COL_V = 2 * RWKV_WIDTH
COL_LORA = 3 * RWKV_WIDTH
COL_GATE = COL_LORA + D_DECAY_LORA + D_AAA_LORA
COL_Q = SHIFT_WIDTH
COL_F = COL_Q + HGRN_WIDTH
COL_I = COL_F + HGRN_WIDTH
COL_OG = COL_I + HGRN_WIDTH
NORM_EPS = 1e-6
LNX_EPS = 64e-5

LANES = 128
HEAD_PAIR = 2 * RWKV_HEAD_DIM
CHUNK = 64
SUB = 16
PROMPT_GROUPS_PER_STEP = 8
INPROJ_TILE = 256
FF_TILE = 1024
OUT_ROWS_PROMPT = 1024
ROWS_SAMPLE = 256
V7X_VMEM_BYTES = 64 * 1024 * 1024
VMEM_LIMIT = V7X_VMEM_BYTES - 2 * 1024 * 1024


def _dot(a, b):
    return jnp.dot(a.astype(BF16), b.astype(BF16), preferred_element_type=F32)


def _dot_nt(a, b):
    return lax.dot_general(a.astype(BF16), b.astype(BF16), (((1,), (1,)), ((), ())),
                           preferred_element_type=F32)


def _dot_2pass_l(m, x):
    mb = m.astype(BF16)
    hi = x.astype(BF16)
    lo = (x - hi.astype(F32)).astype(BF16)
    return jnp.dot(mb, hi, preferred_element_type=F32) + jnp.dot(mb, lo, preferred_element_type=F32)


def _repeat_rows(m, rep):
    if rep == 1:
        return m
    n = m.shape[0]
    sel = (_shr(_iota((n * rep, n), 0), rep) == _iota((n * rep, n), 1)).astype(BF16)
    hi = m.astype(BF16)
    r1 = m - hi.astype(F32)
    mid = r1.astype(BF16)
    lo = (r1 - mid.astype(F32)).astype(BF16)
    d = lambda part: jnp.dot(sel, part, preferred_element_type=F32)
    return d(hi) + d(mid) + d(lo)


def _group_sums(xs, gmat):
    n, rows = len(xs), xs[0].shape[0]
    P = _dot(jnp.concatenate(xs, axis=0), gmat)
    return [P[i * rows:(i + 1) * rows] for i in range(n)]


def _iota(shape, dim):
    return lax.broadcasted_iota(jnp.int32, shape, dim)


def _shr(x, n):
    return lax.shift_right_logical(x, jnp.int32(n.bit_length() - 1))


def _sigmoid(x):
    return 1.0 / (1.0 + jnp.exp(-x))


def _silu(x):
    return x * _sigmoid(x)


def _softplus(x):
    return jnp.maximum(x, 0.0) + jnp.log(1.0 + jnp.exp(-jnp.abs(x)))


def _rms(x, gain):
    return x * lax.rsqrt(jnp.mean(x * x, axis=-1, keepdims=True) + NORM_EPS) * gain


def _make_masks(C, Ls):
    row = _iota((C, C), 0)
    col = _iota((C, C), 1)
    seq = lambda t: _shr(t, Ls)
    blk = lambda t: _shr(t, SUB)
    same = seq(row) == seq(col)
    R = _iota((2 * C, 2 * C), 0)
    Q = _iota((2 * C, 2 * C), 1)
    tr = R & (C - 1)
    tq = Q & (C - 1)
    same2 = seq(tr) == seq(tq)
    lvl = []
    s = 1
    while s < Ls:
        lvl.append((_shr(R, 2 * s) == _shr(Q, 2 * s)) & ((R & (2 * s - 1)) >= s) & ((Q & (2 * s - 1)) < s))
        s *= 2
    return dict(
        row=row, col=col, same=same,
        mi=same & (col <= row),
        sameblk=blk(row) == blk(col),
        eye2=(R == Q).astype(F32),
        lvl=lvl,
        ms_hi2=same2 & (tq < tr) & (Q >= C),
        mi22=same2 & (tq <= tr),
    )


def _tri_inverse(G, mk):
    lvl = mk['lvl']
    D = mk['eye2'] + jnp.where(lvl[0], G, 0.0)
    for m in lvl[1:]:
        LD = _dot(jnp.where(m, G, 0.0), D)
        yield
        D = D + _dot(D, LD)
        yield
    return D


def _lockstep(streams):
    streams = list(streams)
    out = [None] * len(streams)
    alive = list(range(len(streams)))
    while alive:
        for i in list(alive):
            try:
                next(streams[i])
            except StopIteration as stop:
                out[i] = stop.value
                alive.remove(i)
    return out


def _drain(gen):
    while True:
        try:
            next(gen)
        except StopIteration as stop:
            return stop.value


def _select_seq(full, n_seq, sid_rows):
    if n_seq == 1:
        return full
    acc = jnp.where(sid_rows == 0, full[:, 0:LANES], 0.0)
    for s in range(1, n_seq):
        acc = acc + jnp.where(sid_rows == s, full[:, s * LANES:(s + 1) * LANES], 0.0)
    return acc


def _expand_seq(x, n_seq, sid_rows):
    if n_seq == 1:
        return x
    return jnp.concatenate([jnp.where(sid_rows == s, x, 0.0) for s in range(n_seq)], axis=1)


def _wkv_pair_chunk(r, lw, cw, wl, k, v, kk, a, S, mk, Ls):
    C = r.shape[0]
    n_seq = C // Ls
    lo_half = _iota((C, HEAD_PAIR), 1) < RWKV_HEAD_DIM
    own = (_iota((2 * C, HEAD_PAIR), 1) < RWKV_HEAD_DIM) == (_iota((2 * C, HEAD_PAIR), 0) < C)
    stack = lambda t: jnp.where(own, jnp.concatenate([t, t], axis=0), 0.0)
    e_neg = jnp.exp(-cw)
    at = -kk * jnp.exp(cw - lw)
    bt = kk * a * e_neg
    kt = k * e_neg
    rt = r * jnp.exp(cw)
    Y = jnp.concatenate([bt, kt], axis=0)
    gar = _dot_nt(jnp.concatenate([stack(at), stack(rt)], axis=0), Y)
    ga = gar[:2 * C]
    first_cols = _iota((2 * C, 2 * C), 1) < C
    top_rows = _iota((2 * C, 2 * C), 0) < C
    G = (jnp.where(first_cols & top_rows, ga, 0.0)
         + pltpu.roll(jnp.where(first_cols & jnp.logical_not(top_rows), ga, 0.0), C, 1))
    gr = gar[2 * C:]
    sid2 = _shr(_iota((2 * C, LANES), 0) & (C - 1), Ls)
    XO = _select_seq(_dot_nt(jnp.concatenate([at, rt], axis=0), S), n_seq, sid2)
    yield
    Vz = jnp.concatenate([jnp.zeros_like(v), v], axis=0)
    xk = _dot(jnp.where(mk['ms_hi2'], ga, 0.0), Vz)
    D = yield from _tri_inverse(G, mk)
    X = XO[:C] + jnp.where(lo_half, xk[:C], xk[C:])
    DX = _dot(D, jnp.concatenate([X, X], axis=0))
    yield
    U = jnp.where(lo_half, DX[:C], DX[C:])
    Z = jnp.concatenate([U, v], axis=0)
    oz = _dot(jnp.where(mk['mi22'], gr, 0.0), Z)
    upd = _dot(Z.T, _expand_seq(Y, n_seq, sid2))
    yield
    o = XO[C:] + jnp.where(lo_half, oz[:C], oz[C:])
    bd = (_shr(_iota((HEAD_PAIR, HEAD_PAIR), 0), RWKV_HEAD_DIM)
          == _shr(_iota((HEAD_PAIR, HEAD_PAIR), 1), RWKV_HEAD_DIM))
    new = []
    for s in range(n_seq):
        Ss = S[s * LANES:(s + 1) * LANES, :]
        new.append(jnp.where(bd, (Ss + upd[:, s * LANES:(s + 1) * LANES]) * wl[s * Ls:s * Ls + 1, :], 0.0))
    return o, new


def _hgrn_head_chunk(q, kf, b, blast, iv, ST, mk, Ls):
    C = q.shape[0]
    assert Ls == C or Ls <= SUB
    n_seq = C // Ls
    sid = _shr(_iota((C, LANES), 0), Ls)
    if Ls > SUB:
        rows_of = lambda off: jnp.concatenate(
            [jnp.broadcast_to(b[SUB * j + off:SUB * j + off + 1, :], (SUB, LANES)) for j in range(C // SUB)], axis=0)
        bmid = rows_of(SUB // 2 - 1)
        bend = rows_of(SUB - 1)
    else:
        bmid = 0.0
    o = _select_seq(_dot_nt(q * jnp.exp(b), ST), n_seq, sid)
    att = jnp.where(mk['mi'] & mk['sameblk'],
                    _dot_nt(q * jnp.exp(b - bmid), kf * jnp.exp(bmid - b)), 0.0)
    if Ls > SUB:
        ko = kf * jnp.exp(jnp.minimum(bend - b, 0.0))
        rb = _shr(mk['row'], SUB)
        cb = _shr(mk['col'], SUB)
        nb = C // SUB - 1
        qo = [q * jnp.exp(jnp.minimum(b - b[SUB * j + SUB - 1:SUB * j + SUB, :], 0.0)) for j in range(nb)]
        off = _dot_nt(jnp.concatenate(qo, axis=0), ko)
        for j in range(nb):
            att = att + jnp.where(mk['same'] & (cb == j) & (rb > j), off[j * C:(j + 1) * C], 0.0)
    ke = kf * jnp.exp(blast - b)
    upd = _dot(iv.T, _expand_seq(ke, n_seq, sid))
    yield
    o = o + _dot(att, iv)
    yield
    dec = jnp.exp(blast)
    new = []
    for s in range(n_seq):
        new.append(ST[s * LANES:(s + 1) * LANES, :] * dec[s * Ls:s * Ls + 1, :]
                   + upd[:, s * LANES:(s + 1) * LANES])
    return o, new


def _ada_body(c_ref, w_ref, b_ref, o_ref):
    o_ref[...] = _dot(_silu(c_ref[...]), w_ref[...]) + b_ref[...]


def _inproj_body(rep, x_ref, sh_ref, sc_ref, n1_ref, w_ref, p_ref):
    h = _rms(x_ref[...], n1_ref[...]) * (1.0 + _repeat_rows(sc_ref[...], rep)) + _repeat_rows(sh_ref[...], rep)
    p_ref[...] = jnp.dot(h.astype(BF16), w_ref[...], preferred_element_type=F32)


PREP_FIELDS = ("r", "lw", "kmod", "v", "kk", "a", "g", "bonus", "q", "kf", "iv", "og")


def _rec_body(Ls, single_seq, R, fuse_in, *refs):
    refs = list(refs)
    if fuse_in:
        xfirst_ref, xnext_ref, sh1_ref, sc1_ref, n1_ref, win_ref = refs[:6]
        del refs[:6]
        p_ref = None
    else:
        p_ref = refs.pop(0)
    if single_seq:
        shinit_ref = wkvin_ref = hgin_ref = None
    else:
        shinit_ref, wkvin_ref, hgin_ref = refs[:3]
        del refs[:3]
    (mu_ref, w0_ref, a0_ref, kk_ref, ka_ref, rk_ref, lnw_ref, lnb_ref, lb_ref, gn_ref, wcomb_ref, wgate_ref,
     y_ref, shout_ref, wkvout_ref, hgout_ref, wkv_sc, hg_sc) = refs[:18]
    if fuse_in:
        assert single_seq
        prep_sc = dict(zip(PREP_FIELDS, refs[18:18 + len(PREP_FIELDS)]))
        cum_sc = refs[18 + len(PREP_FIELDS)]
    C = CHUNK
    n_seq = C // Ls
    NP = RWKV_HEADS // 2
    HD = RWKV_HEAD_DIM
    c = pl.program_id(1)
    rows = lambda j: slice(j * C, (j + 1) * C)
    blk = lambda s: slice(s * LANES, (s + 1) * LANES)
    pairs = [slice(hp * HEAD_PAIR, (hp + 1) * HEAD_PAIR) for hp in range(NP)]
    mk = _make_masks(C, Ls)
    gsum = (_shr(_iota((HEAD_PAIR, HEAD_PAIR), 0), RWKV_HEAD_DIM)
            == _shr(_iota((HEAD_PAIR, HEAD_PAIR), 1), RWKV_HEAD_DIM)).astype(F32)

    def prepare(pcols, first):
        rowv = _iota((C, SHIFT_WIDTH), 0)
        xs = []
        for j in range(R):
            p_rw = pcols(j, 0, SHIFT_WIDTH)
            keep = shout_ref.shape[1]
            p_prev = jnp.where(rowv == 0, shout_ref[j, keep - 1:keep, :], pltpu.roll(p_rw, 1, 0))
            if not single_seq:
                p_prev = jnp.where((rowv & (Ls - 1)) == 0, shinit_ref[j], p_prev)
            elif first:
                p_prev = jnp.where(rowv == 0, 0.0, p_prev)
            shout_ref[j] = p_rw[C - keep:, :]
            xs.append(p_rw + (p_prev - p_rw) * mu_ref[...])
        x = jnp.concatenate(xs, axis=0)
        cat = lambda lo, hi: jnp.concatenate([pcols(j, lo, hi) for j in range(R)], axis=0)
        r = x[:, 0:COL_K]
        k = x[:, COL_K:COL_V]
        v = x[:, COL_V:COL_LORA]
        wa = x[:, COL_LORA:COL_GATE]
        gd = x[:, COL_GATE:SHIFT_WIDTH]
        lane = _iota((R * C, LANES), 1)
        da = _dot(jnp.where(lane < D_DECAY_LORA, jnp.tanh(wa), wa), wcomb_ref[...])
        g = _dot(_sigmoid(gd), wgate_ref[...])
        yield
        w_log = -_softplus(-(w0_ref[...] + da[:, :RWKV_WIDTH])) - 0.5
        lw = -jnp.exp(w_log)
        a = _sigmoid(a0_ref[...] + da[:, RWKV_WIDTH:])
        kkr = k * kk_ref[...]
        kmod = k * (1.0 + (a - 1.0) * ka_ref[...])
        rkr = r * kmod * rk_ref[...]
        lbp = lb_ref[...]
        m = jnp.maximum(lbp[0:1, :], lbp[1:2, :])
        e0 = jnp.exp(lbp[0:1, :] - m)
        e1 = jnp.exp(lbp[1:2, :] - m)
        lb = e0 / (e0 + e1)
        q = _silu(cat(COL_Q, COL_F))
        f = lb + (1.0 - lb) * _sigmoid(cat(COL_F, COL_I))
        logs = jnp.concatenate([lw, jnp.log(f)], axis=1)
        cums, tots = [], []
        for j in range(R):
            if single_seq:
                cums.append(_dot_2pass_l(mk['mi'].astype(F32), logs[rows(j)]))
            else:
                both = _dot_2pass_l(jnp.concatenate([mk['mi'], mk['same']], axis=0).astype(F32), logs[rows(j)])
                cums.append(both[:C])
                tots.append(both[C:])
        sums = _group_sums([kkr[:, sl] * kkr[:, sl] for sl in pairs] + [rkr[:, sl] for sl in pairs], gsum)
        yield
        kk = jnp.concatenate([kkr[:, sl] / jnp.maximum(jnp.sqrt(sums[hp]), 1e-12)
                              for hp, sl in enumerate(pairs)], axis=1)
        bonus = jnp.concatenate(sums[NP:], axis=1) * v
        d = dict(r=r, lw=lw, kmod=kmod, v=v, kk=kk, a=a, g=g, bonus=bonus, q=q, kf=1.0 - f,
                 iv=cat(COL_I, COL_OG), og=cat(COL_OG, IN_WIDTH))
        return d, jnp.concatenate(cums, axis=0), tots

    def next_chunk(x_ref, first):
        hs = [_rms(x_ref[j], n1_ref[...]) * (1.0 + sc1_ref[j]) + sh1_ref[j] for j in range(R)]
        hn = jnp.concatenate(hs, axis=0).astype(BF16)
        tiles = []
        for t in range(IN_WIDTH // INPROJ_TILE):
            tiles.append(jnp.dot(hn, win_ref[:, t * INPROJ_TILE:(t + 1) * INPROJ_TILE],
                                 preferred_element_type=F32))
            yield
        pcols = lambda j, lo, hi: jnp.concatenate(tiles[lo // INPROJ_TILE:hi // INPROJ_TILE], axis=1)[rows(j)]
        d, cum, _ = yield from prepare(pcols, first)
        return d, cum

    def hand_over(d, cum):
        for name in PREP_FIELDS:
            prep_sc[name][...] = d[name]
        cum_sc[...] = cum

    @pl.when(c == 0)
    def _():
        shout_ref[...] = jnp.zeros_like(shout_ref)
        wkv_sc[...] = jnp.zeros_like(wkv_sc)
        if single_seq:
            hg_sc[...] = jnp.zeros_like(hg_sc)
        else:
            for j in range(R):
                for s in range(n_seq):
                    for hp in range(NP):
                        wkv_sc[j, hp, s * LANES:s * LANES + HD, 0:HD] = wkvin_ref[j * n_seq + s, 2 * hp]
                        wkv_sc[j, hp, s * LANES + HD:(s + 1) * LANES, HD:2 * HD] = wkvin_ref[j * n_seq + s, 2 * hp + 1]
                    for h in range(HGRN_HEADS):
                        hg_sc[j, h, blk(s), :] = hgin_ref[j * n_seq + s, h].T
        if fuse_in:
            hand_over(*_drain(next_chunk(xfirst_ref, True)))

    if fuse_in:
        d = {name: prep_sc[name][...] for name in PREP_FIELDS}
        cum = cum_sc[...]
        extra = [next_chunk(xnext_ref, False)]
    else:
        d, cum, tots = _drain(prepare(lambda j, lo, hi: p_ref[j, :, lo:hi], True))
        extra = []
    if single_seq:
        tots = [cum[j * C + C - 1:j * C + C, :] for j in range(R)]
    lnw = lnw_ref[...]
    lnb = lnb_ref[...]

    def wkv_stream(j, hp):
        sl = pairs[hp]
        o, new = yield from _wkv_pair_chunk(d["r"][rows(j), sl], d["lw"][rows(j), sl], cum[rows(j), sl],
                                            jnp.exp(tots[j][:, sl]), d["kmod"][rows(j), sl], d["v"][rows(j), sl],
                                            d["kk"][rows(j), sl], d["a"][rows(j), sl], wkv_sc[j, hp], mk, Ls)
        for s in range(n_seq):
            wkv_sc[j, hp, blk(s), :] = new[s]
        return o

    def hgrn_stream(j, h):
        sl = slice(RWKV_WIDTH + h * LANES, RWKV_WIDTH + (h + 1) * LANES)
        hs = blk(h)
        o, new = yield from _hgrn_head_chunk(d["q"][rows(j), hs], d["kf"][rows(j), hs], cum[rows(j), sl],
                                             tots[j][:, sl], d["iv"][rows(j), hs], hg_sc[j, h], mk, Ls)
        for s in range(n_seq):
            hg_sc[j, h, blk(s), :] = new[s]
        on = _rms(o, gn_ref[...])
        y_ref[j, :, sl] = (on * _silu(d["og"][rows(j), hs])).astype(y_ref.dtype)

    outs = _lockstep([wkv_stream(j, hp) for j in range(R) for hp in range(NP)]
                     + [hgrn_stream(j, h) for j in range(R) for h in range(HGRN_HEADS)] + extra)

    os_ = [jnp.concatenate([outs[j * NP + hp] for j in range(R)], axis=0) for hp in range(NP)]
    means = _group_sums(os_, gsum)
    ds = [o - mu * (1.0 / RWKV_HEAD_DIM) for o, mu in zip(os_, means)]
    vars_ = _group_sums([dd * dd for dd in ds], gsum)
    for hp, sl in enumerate(pairs):
        on = ds[hp] * lax.rsqrt(vars_[hp] * (1.0 / RWKV_HEAD_DIM) + LNX_EPS) * lnw[:, sl] + lnb[:, sl]
        yv = ((on + d["bonus"][:, sl]) * d["g"][:, sl]).astype(y_ref.dtype)
        for j in range(R):
            y_ref[j, :, sl] = yv[rows(j)]

    if fuse_in:
        hand_over(*outs[-1])

    @pl.when(c == pl.num_programs(1) - 1)
    def _():
        for j in range(R):
            for s in range(n_seq):
                for hp in range(NP):
                    wkvout_ref[j * n_seq + s, 2 * hp] = wkv_sc[j, hp, s * LANES:s * LANES + HD, 0:HD]
                    wkvout_ref[j * n_seq + s, 2 * hp + 1] = wkv_sc[j, hp, s * LANES + HD:(s + 1) * LANES, HD:2 * HD]
                for h in range(HGRN_HEADS):
                    hgout_ref[j * n_seq + s, h] = hg_sc[j, h, blk(s), :].T


def _out_body(rep, x_ref, ym_ref, gt1_ref, sh2_ref, sc2_ref, gt2_ref, n2_ref, nf_ref,
              wo_ref, wu_ref, wd_ref, o_ref):
    gt1, sh2, sc2, gt2 = (_repeat_rows(m[...], rep) for m in (gt1_ref, sh2_ref, sc2_ref, gt2_ref))
    y = jnp.dot(ym_ref[...], wo_ref[...], preferred_element_type=F32)
    x1 = x_ref[...] + gt1 * y
    h = (_rms(x1, n2_ref[...]) * (1.0 + sc2) + sh2).astype(BF16)
    acc = jnp.zeros_like(x1)
    FC = FF_TILE
    for j in range(D_FF // FC):
        u = jnp.dot(h, wu_ref[:, j * FC:(j + 1) * FC], preferred_element_type=F32)
        u = jnp.square(jnp.maximum(u, 0.0)).astype(BF16)
        acc = acc + jnp.dot(u, wd_ref[j * FC:(j + 1) * FC, :], preferred_element_type=F32)
    x2 = x1 + gt2 * acc
    o_ref[...] = _rms(x2, nf_ref[...])


def _params(sem):
    return pltpu.CompilerParams(dimension_semantics=sem, vmem_limit_bytes=VMEM_LIMIT)


def _full(shape):
    return pl.BlockSpec(shape, lambda *_: (0,) * len(shape))


def _ada(c_all, w_ada, b_ada):
    n = c_all.shape[0]
    TN = 1024
    return pl.pallas_call(
        _ada_body,
        grid=(w_ada.shape[1] // TN,),
        in_specs=[pl.BlockSpec((n, D_MODEL), lambda j: (0, 0)),
                  pl.BlockSpec((D_MODEL, TN), lambda j: (0, j)),
                  pl.BlockSpec((1, TN), lambda j: (0, j))],
        out_specs=pl.BlockSpec((n, TN), lambda j: (0, j)),
        out_shape=jax.ShapeDtypeStruct((n, w_ada.shape[1]), F32),
        compiler_params=_params(("arbitrary",)),
        name="ada",
    )(c_all, w_ada, b_ada)


def _mod_specs(mod, cols, TM, rows_per_mod):
    if mod.ndim == 3:
        return [pl.BlockSpec((None, 1, D_MODEL), lambda i, c=c: (i * TM // rows_per_mod, 0, c)) for c in cols]
    return [pl.BlockSpec((TM // rows_per_mod, D_MODEL), lambda i, c=c: (i, c)) for c in cols]


def _mod_repeat(mod, rows_per_mod):
    return 1 if mod.ndim == 3 else rows_per_mod


def _inproj(x, mod, norm1, w_in, TM, rows_per_mod):
    M = x.shape[0]
    return pl.pallas_call(
        functools.partial(_inproj_body, _mod_repeat(mod, rows_per_mod)),
        grid=(M // TM,),
        in_specs=[pl.BlockSpec((TM, D_MODEL), lambda i: (i, 0))]
                 + _mod_specs(mod, (0, 1), TM, rows_per_mod)
                 + [_full((1, D_MODEL)), _full((D_MODEL, IN_WIDTH))],
        out_specs=pl.BlockSpec((TM, IN_WIDTH), lambda i: (i, 0)),
        out_shape=jax.ShapeDtypeStruct((M, IN_WIDTH), F32),
        compiler_params=_params(("arbitrary",)),
        name="inproj",
    )(x, mod, mod, norm1, w_in)


def _rec(p, n_groups, states, small, wcomb, wgate, Ls, n_batch, R, inproj=None):
    Gt, C = n_groups, CHUNK
    rows_total = (p if inproj is None else inproj[0]).shape[1]
    NC = rows_total // C
    n_seq = C // Ls
    keep = 8 if states is None else C
    wkv_spec = pl.BlockSpec((R * n_seq, RWKV_HEADS, RWKV_HEAD_DIM, RWKV_HEAD_DIM), lambda g, c: (g, 0, 0, 0))
    hg_spec = pl.BlockSpec((R * n_seq, HGRN_HEADS, LANES, LANES), lambda g, c: (g, 0, 0, 0))
    sh_spec = pl.BlockSpec((R, C, SHIFT_WIDTH), lambda g, c: (g, 0, 0))
    sho_spec = pl.BlockSpec((R, keep, SHIFT_WIDTH), lambda g, c: (g, 0, 0))
    st_in = [] if states is None else list(states)
    st_specs = [] if states is None else [sh_spec, wkv_spec, hg_spec]
    scratch = [pltpu.VMEM((R, RWKV_HEADS // 2, n_seq * LANES, LANES), F32),
               pltpu.VMEM((R, HGRN_HEADS, n_seq * LANES, LANES), F32)]
    if inproj is None:
        lead_in = [p]
        lead_specs = [pl.BlockSpec((R, C, IN_WIDTH), lambda g, c: (g, c, 0))]
    else:
        x, mod, norm1, w_in = inproj
        lead_in = [x, x, mod, mod, norm1, w_in]
        lead_specs = [pl.BlockSpec((R, C, D_MODEL), lambda g, c: (g, 0, 0)),
                      pl.BlockSpec((R, C, D_MODEL), lambda g, c: (g, jnp.minimum(c + 1, NC - 1), 0)),
                      pl.BlockSpec((R, 1, D_MODEL), lambda g, c: (g, 0, 0)),
                      pl.BlockSpec((R, 1, D_MODEL), lambda g, c: (g, 0, 1)),
                      _full(norm1.shape), _full(w_in.shape)]
        scratch += [pltpu.VMEM((R * C, RWKV_WIDTH), F32) for _ in PREP_FIELDS]
        scratch.append(pltpu.VMEM((R * C, 2 * RWKV_WIDTH), F32))
    return pl.pallas_call(
        functools.partial(_rec_body, Ls, states is None, R, inproj is not None),
        grid=(Gt // R, NC),
        in_specs=lead_specs + st_specs
                 + [_full(s.shape) for s in small] + [_full(wcomb.shape), _full(wgate.shape)],
        out_specs=[pl.BlockSpec((R, C, D_MODEL), lambda g, c: (g, c, 0)), sho_spec, wkv_spec, hg_spec],
        out_shape=[jax.ShapeDtypeStruct((Gt, NC * C, D_MODEL), BF16),
                   jax.ShapeDtypeStruct((Gt, keep, SHIFT_WIDTH), F32),
                   jax.ShapeDtypeStruct((n_batch, RWKV_HEADS, RWKV_HEAD_DIM, RWKV_HEAD_DIM), F32),
                   jax.ShapeDtypeStruct((n_batch, HGRN_HEADS, LANES, LANES), F32)],
        scratch_shapes=scratch,
        compiler_params=_params(("arbitrary", "arbitrary")),
        name="rec",
    )(*lead_in, *st_in, *small, wcomb, wgate)


def _out(x, ym, mod, norm2, norm_f, w_out, w_up, w_down, TM, rows_per_mod):
    M = x.shape[0]
    row = lambda i: (i, 0)
    return pl.pallas_call(
        functools.partial(_out_body, _mod_repeat(mod, rows_per_mod)),
        grid=(M // TM,),
        in_specs=[pl.BlockSpec((TM, D_MODEL), row), pl.BlockSpec((TM, D_MODEL), row)]
                 + _mod_specs(mod, (2, 3, 4, 5), TM, rows_per_mod)
                 + [_full((1, D_MODEL)), _full((1, D_MODEL)),
                    _full(w_out.shape), _full(w_up.shape), _full(w_down.shape)],
        out_specs=pl.BlockSpec((TM, D_MODEL), row),
        out_shape=jax.ShapeDtypeStruct((M, D_MODEL), F32),
        compiler_params=_params(("arbitrary",)),
        name="outmlp",
    )(x, ym, mod, mod, mod, mod, norm2, norm_f, w_out, w_up, w_down)


def kernel(x_prompt, x_sample, c_prompt, c_sample, state_shift, state_wkv, state_hgrn, norm1, norm2, norm_f, w_ada, b_ada, w_in, mu_shift, w0, w_decay_up, a0, w_aaa_up, w_gate_up, k_k, k_a, r_k, lnx_w, lnx_b, hgrn_lb, hgrn_gnorm, w_out, w_up, w_down):
    BP, TP, _ = x_prompt.shape
    BS, TS, _ = x_sample.shape
    l = 0
    row = lambda t: t.reshape(1, -1)

    mod = _ada(jnp.concatenate([c_prompt, c_sample], axis=0), w_ada[l], row(b_ada[l]))
    mod_p = mod[:BP].reshape(BP, 1, 6 * D_MODEL)
    mod_s = mod[BP:]

    w_in_b = w_in[l].astype(BF16)
    w_out_b = w_out[l].astype(BF16)
    w_up_b = w_up[l].astype(BF16)
    w_down_b = w_down[l].astype(BF16)
    zer = jnp.zeros((D_DECAY_LORA, RWKV_WIDTH), F32)
    wcomb = jnp.concatenate([jnp.concatenate([w_decay_up[l], zer], axis=1),
                             jnp.concatenate([zer, w_aaa_up[l]], axis=1)], axis=0).astype(BF16)
    wgate = w_gate_up[l].astype(BF16)
    small = [row(mu_shift[l]), row(w0[l]), row(a0[l]), row(k_k[l]), row(k_a[l]), row(r_k[l]),
             row(lnx_w[l]), row(lnx_b[l]), hgrn_lb, row(hgrn_gnorm[l])]
    n1, n2, nf = row(norm1[l]), row(norm2[l]), row(norm_f)

    def trunk(x2d, modx, rows_per_mod, Gt, states, Ls, n_batch, TM, R, fuse_in):
        if fuse_in:
            ym, shout, wkv_o, hg_o = _rec(None, Gt, states, small, wcomb, wgate, Ls, n_batch, R,
                                          inproj=(x2d.reshape(Gt, -1, D_MODEL), modx, n1, w_in_b))
        else:
            p = _inproj(x2d, modx, n1, w_in_b, TM, rows_per_mod)
            ym, shout, wkv_o, hg_o = _rec(p.reshape(Gt, -1, IN_WIDTH), Gt, states, small, wcomb, wgate,
                                          Ls, n_batch, R)
        y = _out(x2d, ym.reshape(-1, D_MODEL), modx, n2, nf, w_out_b, w_up_b, w_down_b, TM, rows_per_mod)
        return y, shout, wkv_o, hg_o

    yp, shp, wkvp, hgp = trunk(x_prompt.reshape(BP * TP, D_MODEL), mod_p, TP, BP, None, CHUNK, BP, OUT_ROWS_PROMPT,
                               PROMPT_GROUPS_PER_STEP, True)
    y_prompt = yp.reshape(BP, TP, D_MODEL)
    shift_p = shp[:, -1][None]
    wkv_p = wkvp[None]
    hgrn_p = hgp[None]

    n_seq = CHUNK // TS
    GS = BS // n_seq
    shinit_s = jnp.zeros((GS, n_seq, TS, SHIFT_WIDTH), F32).at[:, :, 0].set(
        state_shift[l].reshape(GS, n_seq, SHIFT_WIDTH)).reshape(GS, CHUNK, SHIFT_WIDTH)
    ys, shs, wkvs, hgs = trunk(x_sample.reshape(BS * TS, D_MODEL), mod_s, TS, GS,
                               (shinit_s, state_wkv[l], state_hgrn[l]), TS, BS, ROWS_SAMPLE, 1, False)
    y_sample = ys.reshape(BS, TS, D_MODEL)
    shift_s = shs.reshape(GS, n_seq, TS, SHIFT_WIDTH)[:, :, TS - 1].reshape(BS, SHIFT_WIDTH)[None]
    wkv_s = wkvs[None]
    hgrn_s = hgs[None]

    return (y_prompt, y_sample, shift_p, wkv_p, hgrn_p, shift_s, wkv_s, hgrn_s)
```

```python
import functools

import jax
import jax.numpy as jnp
from jax import lax
from jax.experimental import pallas as pl
from jax.experimental.pallas import tpu as pltpu

F32 = jnp.float32
BF16 = jnp.bfloat16

D_MODEL = 1024
RWKV_HEADS = 8
RWKV_WIDTH = 512
HGRN_HEADS = 4
HGRN_WIDTH = 512
RWKV_HEAD_DIM = 64
D_DECAY_LORA = 64
D_AAA_LORA = 64
D_GATE_LORA = 128
SHIFT_WIDTH = 1792
IN_WIDTH = 3840
D_FF = 4096
COL_K = RWKV_WIDTH
COL_V = 2 * RWKV_WIDTH
COL_LORA = 3 * RWKV_WIDTH
COL_GATE = COL_LORA + D_DECAY_LORA + D_AAA_LORA
COL_Q = SHIFT_WIDTH
COL_F = COL_Q + HGRN_WIDTH
COL_I = COL_F + HGRN_WIDTH
COL_OG = COL_I + HGRN_WIDTH
NORM_EPS = 1e-6
LNX_EPS = 64e-5

LANES = 128
HEAD_PAIR = 2 * RWKV_HEAD_DIM
CHUNK = 64
SUB = 16
PROMPT_GROUPS_PER_STEP = 8
INPROJ_TILE = 256
FF_TILE = 1024
OUT_ROWS_PROMPT = 512
ROWS_SAMPLE = 256
V7X_VMEM_BYTES = 64 * 1024 * 1024
VMEM_LIMIT = V7X_VMEM_BYTES - 2 * 1024 * 1024


def _dot(a, b):
    return jnp.dot(a.astype(BF16), b.astype(BF16), preferred_element_type=F32)


def _dot_nt(a, b):
    return lax.dot_general(a.astype(BF16), b.astype(BF16), (((1,), (1,)), ((), ())),
                           preferred_element_type=F32)


def _dot_2pass_l(m, x):
    mb = m.astype(BF16)
    hi = x.astype(BF16)
    lo = (x - hi.astype(F32)).astype(BF16)
    return jnp.dot(mb, hi, preferred_element_type=F32) + jnp.dot(mb, lo, preferred_element_type=F32)


def _repeat_rows(m, rep):
    if rep == 1:
        return m
    n = m.shape[0]
    sel = (_shr(_iota((n * rep, n), 0), rep) == _iota((n * rep, n), 1)).astype(BF16)
    hi = m.astype(BF16)
    r1 = m - hi.astype(F32)
    mid = r1.astype(BF16)
    lo = (r1 - mid.astype(F32)).astype(BF16)
    d = lambda part: jnp.dot(sel, part, preferred_element_type=F32)
    return d(hi) + d(mid) + d(lo)


def _group_sums(xs, gmat):
    n, rows = len(xs), xs[0].shape[0]
    P = _dot(jnp.concatenate(xs, axis=0), gmat)
    return [P[i * rows:(i + 1) * rows] for i in range(n)]


def _iota(shape, dim):
    return lax.broadcasted_iota(jnp.int32, shape, dim)


def _shr(x, n):
    return lax.shift_right_logical(x, jnp.int32(n.bit_length() - 1))


def _sigmoid(x):
    return 1.0 / (1.0 + jnp.exp(-x))


def _silu(x):
    return x * _sigmoid(x)


def _softplus(x):
    return jnp.maximum(x, 0.0) + jnp.log(1.0 + jnp.exp(-jnp.abs(x)))


def _rms(x, gain):
    return x * lax.rsqrt(jnp.mean(x * x, axis=-1, keepdims=True) + NORM_EPS) * gain


def _make_masks(C, Ls):
    row = _iota((C, C), 0)
    col = _iota((C, C), 1)
    seq = lambda t: _shr(t, Ls)
    blk = lambda t: _shr(t, SUB)
    same = seq(row) == seq(col)
    R = _iota((2 * C, 2 * C), 0)
    Q = _iota((2 * C, 2 * C), 1)
    tr = R & (C - 1)
    tq = Q & (C - 1)
    same2 = seq(tr) == seq(tq)
    lvl = []
    s = 1
    while s < Ls:
        lvl.append((_shr(R, 2 * s) == _shr(Q, 2 * s)) & ((R & (2 * s - 1)) >= s) & ((Q & (2 * s - 1)) < s))
        s *= 2
    lane2 = _iota((2 * C, HEAD_PAIR), 1)
    row2 = _iota((2 * C, HEAD_PAIR), 0)
    pr = _iota((HEAD_PAIR, HEAD_PAIR), 0)
    pc = _iota((HEAD_PAIR, HEAD_PAIR), 1)
    return dict(
        row=row, col=col, same=same,
        mi=same & (col <= row),
        diag_blocks=same & (col <= row) & (blk(row) == blk(col)),
        off_blocks=[same & (blk(col) == j) & (blk(row) > j) for j in range(C // SUB - 1)],
        eye2=(R == Q).astype(F32),
        lvl=lvl,
        ms_hi2=same2 & (tq < tr) & (Q >= C),
        mi22=same2 & (tq <= tr),
        lo_half=_iota((C, HEAD_PAIR), 1) < RWKV_HEAD_DIM,
        own=(lane2 < RWKV_HEAD_DIM) == (row2 < C),
        top_left=(Q < C) & (R < C), bottom_left=(Q < C) & (R >= C),
        seq_of_row2=seq(row2 & (C - 1)),
        seq_of_row=seq(_iota((C, LANES), 0)),
        head_diag=_shr(pr, RWKV_HEAD_DIM) == _shr(pc, RWKV_HEAD_DIM),
    )


def _tri_inverse(G, mk):
    lvl = mk['lvl']
    D = mk['eye2'] + jnp.where(lvl[0], G, 0.0)
    for m in lvl[1:]:
        LD = _dot(jnp.where(m, G, 0.0), D)
        yield
        D = D + _dot(D, LD)
        yield
    return D


def _lockstep(streams):
    streams = list(streams)
    out = [None] * len(streams)
    alive = list(range(len(streams)))
    while alive:
        for i in list(alive):
            try:
                next(streams[i])
            except StopIteration as stop:
                out[i] = stop.value
                alive.remove(i)
    return out


def _drain(gen):
    while True:
        try:
            next(gen)
        except StopIteration as stop:
            return stop.value


def _select_seq(full, n_seq, sid_rows):
    if n_seq == 1:
        return full
    acc = jnp.where(sid_rows == 0, full[:, 0:LANES], 0.0)
    for s in range(1, n_seq):
        acc = acc + jnp.where(sid_rows == s, full[:, s * LANES:(s + 1) * LANES], 0.0)
    return acc


def _expand_seq(x, n_seq, sid_rows):
    if n_seq == 1:
        return x
    return jnp.concatenate([jnp.where(sid_rows == s, x, 0.0) for s in range(n_seq)], axis=1)


def _wkv_pair_chunk(r, lw, cw, wl, k, v, kk, a, S, mk, Ls):
    C = r.shape[0]
    n_seq = C // Ls
    lo_half = mk['lo_half']
    stack = lambda t: jnp.where(mk['own'], jnp.concatenate([t, t], axis=0), 0.0)
    e_neg = jnp.exp(-cw)
    at = -kk * jnp.exp(cw - lw)
    bt = kk * a * e_neg
    kt = k * e_neg
    rt = r * jnp.exp(cw)
    Y = jnp.concatenate([bt, kt], axis=0)
    gar = _dot_nt(jnp.concatenate([stack(at), stack(rt)], axis=0), Y)
    ga = gar[:2 * C]
    G = jnp.where(mk['top_left'], ga, 0.0) + pltpu.roll(jnp.where(mk['bottom_left'], ga, 0.0), C, 1)
    gr = gar[2 * C:]
    sid2 = mk['seq_of_row2']
    XO = _select_seq(_dot_nt(jnp.concatenate([at, rt], axis=0), S), n_seq, sid2)
    yield
    Vz = jnp.concatenate([jnp.zeros_like(v), v], axis=0)
    xk = _dot(jnp.where(mk['ms_hi2'], ga, 0.0), Vz)
    D = yield from _tri_inverse(G, mk)
    X = XO[:C] + jnp.where(lo_half, xk[:C], xk[C:])
    DX = _dot(D, jnp.concatenate([X, X], axis=0))
    yield
    U = jnp.where(lo_half, DX[:C], DX[C:])
    Z = jnp.concatenate([U, v], axis=0)
    oz = _dot(jnp.where(mk['mi22'], gr, 0.0), Z)
    upd = _dot(Z.T, _expand_seq(Y, n_seq, sid2))
    yield
    o = XO[C:] + jnp.where(lo_half, oz[:C], oz[C:])
    bd = mk['head_diag']
    new = []
    for s in range(n_seq):
        Ss = S[s * LANES:(s + 1) * LANES, :]
        new.append(jnp.where(bd, (Ss + upd[:, s * LANES:(s + 1) * LANES]) * wl[s * Ls:s * Ls + 1, :], 0.0))
    return o, new


def _hgrn_head_chunk(q, kf, b, blast, iv, ST, mk, Ls):
    C = q.shape[0]
    assert Ls == C or Ls <= SUB
    n_seq = C // Ls
    sid = mk['seq_of_row']
    if Ls > SUB:
        rows_of = lambda off: jnp.concatenate(
            [jnp.broadcast_to(b[SUB * j + off:SUB * j + off + 1, :], (SUB, LANES)) for j in range(C // SUB)], axis=0)
        bmid = rows_of(SUB // 2 - 1)
        bend = rows_of(SUB - 1)
    else:
        bmid = 0.0
    o = _select_seq(_dot_nt(q * jnp.exp(b), ST), n_seq, sid)
    att = jnp.where(mk['diag_blocks'],
                    _dot_nt(q * jnp.exp(b - bmid), kf * jnp.exp(bmid - b)), 0.0)
    if Ls > SUB:
        ko = kf * jnp.exp(jnp.minimum(bend - b, 0.0))
        nb = C // SUB - 1
        qo = [q * jnp.exp(jnp.minimum(b - b[SUB * j + SUB - 1:SUB * j + SUB, :], 0.0)) for j in range(nb)]
        off = _dot_nt(jnp.concatenate(qo, axis=0), ko)
        for j in range(nb):
            att = att + jnp.where(mk['off_blocks'][j], off[j * C:(j + 1) * C], 0.0)
    ke = kf * jnp.exp(blast - b)
    upd = _dot(iv.T, _expand_seq(ke, n_seq, sid))
    yield
    o = o + _dot(att, iv)
    yield
    dec = jnp.exp(blast)
    new = []
    for s in range(n_seq):
        new.append(ST[s * LANES:(s + 1) * LANES, :] * dec[s * Ls:s * Ls + 1, :]
                   + upd[:, s * LANES:(s + 1) * LANES])
    return o, new


def _ada_body(c_ref, w_ref, b_ref, o_ref):
    o_ref[...] = _dot(_silu(c_ref[...]), w_ref[...]) + b_ref[...]


def _inproj_body(rep, x_ref, sh_ref, sc_ref, n1_ref, w_ref, p_ref):
    h = _rms(x_ref[...], n1_ref[...]) * (1.0 + _repeat_rows(sc_ref[...], rep)) + _repeat_rows(sh_ref[...], rep)
    p_ref[...] = jnp.dot(h.astype(BF16), w_ref[...], preferred_element_type=F32)


PREP_FIELDS = ("r", "lw", "kmod", "v", "kk", "a", "g", "bonus", "q", "kf", "iv", "og")


def _rec_body(Ls, single_seq, R, fuse_in, *refs):
    refs = list(refs)
    if fuse_in:
        xfirst_ref, xnext_ref, sh1_ref, sc1_ref, n1_ref, win_ref = refs[:6]
        del refs[:6]
        p_ref = None
    else:
        p_ref = refs.pop(0)
    if single_seq:
        shinit_ref = wkvin_ref = hgin_ref = None
    else:
        shinit_ref, wkvin_ref, hgin_ref = refs[:3]
        del refs[:3]
    (mu_ref, w0_ref, a0_ref, kk_ref, ka_ref, rk_ref, lnw_ref, lnb_ref, lb_ref, gn_ref, wcomb_ref, wgate_ref,
     y_ref, shout_ref, wkvout_ref, hgout_ref, wkv_sc, hg_sc) = refs[:18]
    if fuse_in:
        assert single_seq
        prep_sc = dict(zip(PREP_FIELDS, refs[18:18 + len(PREP_FIELDS)]))
        cum_sc = refs[18 + len(PREP_FIELDS)]
    C = CHUNK
    n_seq = C // Ls
    NP = RWKV_HEADS // 2
    HD = RWKV_HEAD_DIM
    c = pl.program_id(1)
    rows = lambda j: slice(j * C, (j + 1) * C)
    blk = lambda s: slice(s * LANES, (s + 1) * LANES)
    pairs = [slice(hp * HEAD_PAIR, (hp + 1) * HEAD_PAIR) for hp in range(NP)]
    mk = _make_masks(C, Ls)
    gsum = mk['head_diag'].astype(F32)

    def prepare(pcols, first):
        rowv = _iota((C, SHIFT_WIDTH), 0)
        xs = []
        for j in range(R):
            p_rw = pcols(j, 0, SHIFT_WIDTH)
            keep = shout_ref.shape[1]
            p_prev = jnp.where(rowv == 0, shout_ref[j, keep - 1:keep, :], pltpu.roll(p_rw, 1, 0))
            if not single_seq:
                p_prev = jnp.where((rowv & (Ls - 1)) == 0, shinit_ref[j], p_prev)
            elif first:
                p_prev = jnp.where(rowv == 0, 0.0, p_prev)
            shout_ref[j] = p_rw[C - keep:, :]
            xs.append(p_rw + (p_prev - p_rw) * mu_ref[...])
        x = jnp.concatenate(xs, axis=0)
        cat = lambda lo, hi: jnp.concatenate([pcols(j, lo, hi) for j in range(R)], axis=0)
        r = x[:, 0:COL_K]
        k = x[:, COL_K:COL_V]
        v = x[:, COL_V:COL_LORA]
        wa = x[:, COL_LORA:COL_GATE]
        gd = x[:, COL_GATE:SHIFT_WIDTH]
        lane = _iota((R * C, LANES), 1)
        da = _dot(jnp.where(lane < D_DECAY_LORA, jnp.tanh(wa), wa), wcomb_ref[...])
        g = _dot(_sigmoid(gd), wgate_ref[...])
        yield
        w_log = -_softplus(-(w0_ref[...] + da[:, :RWKV_WIDTH])) - 0.5
        lw = -jnp.exp(w_log)
        a = _sigmoid(a0_ref[...] + da[:, RWKV_WIDTH:])
        kkr = k * kk_ref[...]
        kmod = k * (1.0 + (a - 1.0) * ka_ref[...])
        rkr = r * kmod * rk_ref[...]
        lbp = lb_ref[...]
        m = jnp.maximum(lbp[0:1, :], lbp[1:2, :])
        e0 = jnp.exp(lbp[0:1, :] - m)
        e1 = jnp.exp(lbp[1:2, :] - m)
        lb = e0 / (e0 + e1)
        q = _silu(cat(COL_Q, COL_F))
        f = lb + (1.0 - lb) * _sigmoid(cat(COL_F, COL_I))
        logs = jnp.concatenate([lw, jnp.log(f)], axis=1)
        cums, tots = [], []
        for j in range(R):
            if single_seq:
                cums.append(_dot_2pass_l(mk['mi'].astype(F32), logs[rows(j)]))
            else:
                both = _dot_2pass_l(jnp.concatenate([mk['mi'], mk['same']], axis=0).astype(F32), logs[rows(j)])
                cums.append(both[:C])
                tots.append(both[C:])
        sums = _group_sums([kkr[:, sl] * kkr[:, sl] for sl in pairs] + [rkr[:, sl] for sl in pairs], gsum)
        yield
        kk = jnp.concatenate([kkr[:, sl] / jnp.maximum(jnp.sqrt(sums[hp]), 1e-12)
                              for hp, sl in enumerate(pairs)], axis=1)
        bonus = jnp.concatenate(sums[NP:], axis=1) * v
        d = dict(r=r, lw=lw, kmod=kmod, v=v, kk=kk, a=a, g=g, bonus=bonus, q=q, kf=1.0 - f,
                 iv=cat(COL_I, COL_OG), og=cat(COL_OG, IN_WIDTH))
        return d, jnp.concatenate(cums, axis=0), tots

    def next_chunk(x_ref, first):
        hs = [_rms(x_ref[j], n1_ref[...]) * (1.0 + sc1_ref[j]) + sh1_ref[j] for j in range(R)]
        hn = jnp.concatenate(hs, axis=0).astype(BF16)
        tiles = []
        for t in range(IN_WIDTH // INPROJ_TILE):
            tiles.append(jnp.dot(hn, win_ref[:, t * INPROJ_TILE:(t + 1) * INPROJ_TILE],
                                 preferred_element_type=F32))
            yield
        pcols = lambda j, lo, hi: jnp.concatenate(tiles[lo // INPROJ_TILE:hi // INPROJ_TILE], axis=1)[rows(j)]
        d, cum, _ = yield from prepare(pcols, first)
        return d, cum

    def hand_over(d, cum):
        for name in PREP_FIELDS:
            prep_sc[name][...] = d[name]
        cum_sc[...] = cum

    @pl.when(c == 0)
    def _():
        shout_ref[...] = jnp.zeros_like(shout_ref)
        wkv_sc[...] = jnp.zeros_like(wkv_sc)
        if single_seq:
            hg_sc[...] = jnp.zeros_like(hg_sc)
        else:
            for j in range(R):
                for s in range(n_seq):
                    for hp in range(NP):
                        wkv_sc[j, hp, s * LANES:s * LANES + HD, 0:HD] = wkvin_ref[j * n_seq + s, 2 * hp]
                        wkv_sc[j, hp, s * LANES + HD:(s + 1) * LANES, HD:2 * HD] = wkvin_ref[j * n_seq + s, 2 * hp + 1]
                    for h in range(HGRN_HEADS):
                        hg_sc[j, h, blk(s), :] = hgin_ref[j * n_seq + s, h].T
        if fuse_in:
            hand_over(*_drain(next_chunk(xfirst_ref, True)))

    if fuse_in:
        d = {name: prep_sc[name][...] for name in PREP_FIELDS}
        cum = cum_sc[...]
        extra = [next_chunk(xnext_ref, False)]
    else:
        d, cum, tots = _drain(prepare(lambda j, lo, hi: p_ref[j, :, lo:hi], True))
        extra = []
    if single_seq:
        tots = [cum[j * C + C - 1:j * C + C, :] for j in range(R)]
    lnw = lnw_ref[...]
    lnb = lnb_ref[...]

    def wkv_stream(j, hp):
        sl = pairs[hp]
        o, new = yield from _wkv_pair_chunk(d["r"][rows(j), sl], d["lw"][rows(j), sl], cum[rows(j), sl],
                                            jnp.exp(tots[j][:, sl]), d["kmod"][rows(j), sl], d["v"][rows(j), sl],
                                            d["kk"][rows(j), sl], d["a"][rows(j), sl], wkv_sc[j, hp], mk, Ls)
        for s in range(n_seq):
            wkv_sc[j, hp, blk(s), :] = new[s]
        return o

    def hgrn_stream(j, h):
        sl = slice(RWKV_WIDTH + h * LANES, RWKV_WIDTH + (h + 1) * LANES)
        hs = blk(h)
        o, new = yield from _hgrn_head_chunk(d["q"][rows(j), hs], d["kf"][rows(j), hs], cum[rows(j), sl],
                                             tots[j][:, sl], d["iv"][rows(j), hs], hg_sc[j, h], mk, Ls)
        for s in range(n_seq):
            hg_sc[j, h, blk(s), :] = new[s]
        on = _rms(o, gn_ref[...])
        y_ref[j, :, sl] = (on * _silu(d["og"][rows(j), hs])).astype(y_ref.dtype)

    outs = _lockstep([wkv_stream(j, hp) for j in range(R) for hp in range(NP)]
                     + [hgrn_stream(j, h) for j in range(R) for h in range(HGRN_HEADS)] + extra)

    os_ = [jnp.concatenate([outs[j * NP + hp] for j in range(R)], axis=0) for hp in range(NP)]
    means = _group_sums(os_, gsum)
    ds = [o - mu * (1.0 / RWKV_HEAD_DIM) for o, mu in zip(os_, means)]
    vars_ = _group_sums([dd * dd for dd in ds], gsum)
    for hp, sl in enumerate(pairs):
        on = ds[hp] * lax.rsqrt(vars_[hp] * (1.0 / RWKV_HEAD_DIM) + LNX_EPS) * lnw[:, sl] + lnb[:, sl]
        yv = ((on + d["bonus"][:, sl]) * d["g"][:, sl]).astype(y_ref.dtype)
        for j in range(R):
            y_ref[j, :, sl] = yv[rows(j)]

    if fuse_in:
        hand_over(*outs[-1])

    @pl.when(c == pl.num_programs(1) - 1)
    def _():
        for j in range(R):
            for s in range(n_seq):
                for hp in range(NP):
                    wkvout_ref[j * n_seq + s, 2 * hp] = wkv_sc[j, hp, s * LANES:s * LANES + HD, 0:HD]
                    wkvout_ref[j * n_seq + s, 2 * hp + 1] = wkv_sc[j, hp, s * LANES + HD:(s + 1) * LANES, HD:2 * HD]
                for h in range(HGRN_HEADS):
                    hgout_ref[j * n_seq + s, h] = hg_sc[j, h, blk(s), :].T


def _out_body(rep, x_ref, ym_ref, gt1_ref, sh2_ref, sc2_ref, gt2_ref, n2_ref, nf_ref,
              wo_ref, wu_ref, wd_ref, o_ref):
    gt1, sh2, sc2, gt2 = (_repeat_rows(m[...], rep) for m in (gt1_ref, sh2_ref, sc2_ref, gt2_ref))
    y = jnp.dot(ym_ref[...], wo_ref[...], preferred_element_type=F32)
    x1 = x_ref[...] + gt1 * y
    h = (_rms(x1, n2_ref[...]) * (1.0 + sc2) + sh2).astype(BF16)
    acc = jnp.zeros_like(x1)
    FC = FF_TILE
    for j in range(D_FF // FC):
        u = jnp.dot(h, wu_ref[:, j * FC:(j + 1) * FC], preferred_element_type=F32)
        u = jnp.square(jnp.maximum(u, 0.0)).astype(BF16)
        acc = acc + jnp.dot(u, wd_ref[j * FC:(j + 1) * FC, :], preferred_element_type=F32)
    x2 = x1 + gt2 * acc
    o_ref[...] = _rms(x2, nf_ref[...])


def _params(sem):
    return pltpu.CompilerParams(dimension_semantics=sem, vmem_limit_bytes=VMEM_LIMIT)


def _full(shape):
    return pl.BlockSpec(shape, lambda *_: (0,) * len(shape))


def _ada(c_all, w_ada, b_ada):
    n = c_all.shape[0]
    TN = 1024
    return pl.pallas_call(
        _ada_body,
        grid=(w_ada.shape[1] // TN,),
        in_specs=[pl.BlockSpec((n, D_MODEL), lambda j: (0, 0)),
                  pl.BlockSpec((D_MODEL, TN), lambda j: (0, j)),
                  pl.BlockSpec((1, TN), lambda j: (0, j))],
        out_specs=pl.BlockSpec((n, TN), lambda j: (0, j)),
        out_shape=jax.ShapeDtypeStruct((n, w_ada.shape[1]), F32),
        compiler_params=_params(("arbitrary",)),
        name="ada",
    )(c_all, w_ada, b_ada)


def _mod_specs(mod, cols, TM, rows_per_mod):
    if mod.ndim == 3:
        return [pl.BlockSpec((None, 1, D_MODEL), lambda i, c=c: (i * TM // rows_per_mod, 0, c)) for c in cols]
    return [pl.BlockSpec((TM // rows_per_mod, D_MODEL), lambda i, c=c: (i, c)) for c in cols]


def _mod_repeat(mod, rows_per_mod):
    return 1 if mod.ndim == 3 else rows_per_mod


def _inproj(x, mod, norm1, w_in, TM, rows_per_mod):
    M = x.shape[0]
    return pl.pallas_call(
        functools.partial(_inproj_body, _mod_repeat(mod, rows_per_mod)),
        grid=(M // TM,),
        in_specs=[pl.BlockSpec((TM, D_MODEL), lambda i: (i, 0))]
                 + _mod_specs(mod, (0, 1), TM, rows_per_mod)
                 + [_full((1, D_MODEL)), _full((D_MODEL, IN_WIDTH))],
        out_specs=pl.BlockSpec((TM, IN_WIDTH), lambda i: (i, 0)),
        out_shape=jax.ShapeDtypeStruct((M, IN_WIDTH), F32),
        compiler_params=_params(("arbitrary",)),
        name="inproj",
    )(x, mod, mod, norm1, w_in)


def _rec(p, n_groups, states, small, wcomb, wgate, Ls, n_batch, R, inproj=None):
    Gt, C = n_groups, CHUNK
    rows_total = (p if inproj is None else inproj[0]).shape[1]
    NC = rows_total // C
    n_seq = C // Ls
    keep = 8 if states is None else C
    wkv_spec = pl.BlockSpec((R * n_seq, RWKV_HEADS, RWKV_HEAD_DIM, RWKV_HEAD_DIM), lambda g, c: (g, 0, 0, 0))
    hg_spec = pl.BlockSpec((R * n_seq, HGRN_HEADS, LANES, LANES), lambda g, c: (g, 0, 0, 0))
    sh_spec = pl.BlockSpec((R, C, SHIFT_WIDTH), lambda g, c: (g, 0, 0))
    sho_spec = pl.BlockSpec((R, keep, SHIFT_WIDTH), lambda g, c: (g, 0, 0))
    st_in = [] if states is None else list(states)
    st_specs = [] if states is None else [sh_spec, wkv_spec, hg_spec]
    scratch = [pltpu.VMEM((R, RWKV_HEADS // 2, n_seq * LANES, LANES), F32),
               pltpu.VMEM((R, HGRN_HEADS, n_seq * LANES, LANES), F32)]
    if inproj is None:
        lead_in = [p]
        lead_specs = [pl.BlockSpec((R, C, IN_WIDTH), lambda g, c: (g, c, 0))]
    else:
        x, mod, norm1, w_in = inproj
        lead_in = [x, x, mod, mod, norm1, w_in]
        lead_specs = [pl.BlockSpec((R, C, D_MODEL), lambda g, c: (g, 0, 0)),
                      pl.BlockSpec((R, C, D_MODEL), lambda g, c: (g, jnp.minimum(c + 1, NC - 1), 0)),
                      pl.BlockSpec((R, 1, D_MODEL), lambda g, c: (g, 0, 0)),
                      pl.BlockSpec((R, 1, D_MODEL), lambda g, c: (g, 0, 1)),
                      _full(norm1.shape), _full(w_in.shape)]
        scratch += [pltpu.VMEM((R * C, RWKV_WIDTH), F32) for _ in PREP_FIELDS]
        scratch.append(pltpu.VMEM((R * C, 2 * RWKV_WIDTH), F32))
    return pl.pallas_call(
        functools.partial(_rec_body, Ls, states is None, R, inproj is not None),
        grid=(Gt // R, NC),
        in_specs=lead_specs + st_specs
                 + [_full(s.shape) for s in small] + [_full(wcomb.shape), _full(wgate.shape)],
        out_specs=[pl.BlockSpec((R, C, D_MODEL), lambda g, c: (g, c, 0)), sho_spec, wkv_spec, hg_spec],
        out_shape=[jax.ShapeDtypeStruct((Gt, NC * C, D_MODEL), BF16),
                   jax.ShapeDtypeStruct((Gt, keep, SHIFT_WIDTH), F32),
                   jax.ShapeDtypeStruct((n_batch, RWKV_HEADS, RWKV_HEAD_DIM, RWKV_HEAD_DIM), F32),
                   jax.ShapeDtypeStruct((n_batch, HGRN_HEADS, LANES, LANES), F32)],
        scratch_shapes=scratch,
        compiler_params=_params(("arbitrary", "arbitrary")),
        name="rec",
    )(*lead_in, *st_in, *small, wcomb, wgate)


def _out(x, ym, mod, norm2, norm_f, w_out, w_up, w_down, TM, rows_per_mod):
    M = x.shape[0]
    row = lambda i: (i, 0)
    return pl.pallas_call(
        functools.partial(_out_body, _mod_repeat(mod, rows_per_mod)),
        grid=(M // TM,),
        in_specs=[pl.BlockSpec((TM, D_MODEL), row), pl.BlockSpec((TM, D_MODEL), row)]
                 + _mod_specs(mod, (2, 3, 4, 5), TM, rows_per_mod)
                 + [_full((1, D_MODEL)), _full((1, D_MODEL)),
                    _full(w_out.shape), _full(w_up.shape), _full(w_down.shape)],
        out_specs=pl.BlockSpec((TM, D_MODEL), row),
        out_shape=jax.ShapeDtypeStruct((M, D_MODEL), F32),
        compiler_params=_params(("arbitrary",)),
        name="outmlp",
    )(x, ym, mod, mod, mod, mod, norm2, norm_f, w_out, w_up, w_down)


def kernel(x_prompt, x_sample, c_prompt, c_sample, state_shift, state_wkv, state_hgrn, norm1, norm2, norm_f, w_ada, b_ada, w_in, mu_shift, w0, w_decay_up, a0, w_aaa_up, w_gate_up, k_k, k_a, r_k, lnx_w, lnx_b, hgrn_lb, hgrn_gnorm, w_out, w_up, w_down):
    BP, TP, _ = x_prompt.shape
    BS, TS, _ = x_sample.shape
    l = 0
    row = lambda t: t.reshape(1, -1)

    mod = _ada(jnp.concatenate([c_prompt, c_sample], axis=0), w_ada[l], row(b_ada[l]))
    mod_p = mod[:BP].reshape(BP, 1, 6 * D_MODEL)
    mod_s = mod[BP:]

    w_in_b = w_in[l].astype(BF16)
    w_out_b = w_out[l].astype(BF16)
    w_up_b = w_up[l].astype(BF16)
    w_down_b = w_down[l].astype(BF16)
    zer = jnp.zeros((D_DECAY_LORA, RWKV_WIDTH), F32)
    wcomb = jnp.concatenate([jnp.concatenate([w_decay_up[l], zer], axis=1),
                             jnp.concatenate([zer, w_aaa_up[l]], axis=1)], axis=0).astype(BF16)
    wgate = w_gate_up[l].astype(BF16)
    small = [row(mu_shift[l]), row(w0[l]), row(a0[l]), row(k_k[l]), row(k_a[l]), row(r_k[l]),
             row(lnx_w[l]), row(lnx_b[l]), hgrn_lb, row(hgrn_gnorm[l])]
    n1, n2, nf = row(norm1[l]), row(norm2[l]), row(norm_f)

    def trunk(x2d, modx, rows_per_mod, Gt, states, Ls, n_batch, TM, R, fuse_in):
        if fuse_in:
            ym, shout, wkv_o, hg_o = _rec(None, Gt, states, small, wcomb, wgate, Ls, n_batch, R,
                                          inproj=(x2d.reshape(Gt, -1, D_MODEL), modx, n1, w_in_b))
        else:
            p = _inproj(x2d, modx, n1, w_in_b, TM, rows_per_mod)
            ym, shout, wkv_o, hg_o = _rec(p.reshape(Gt, -1, IN_WIDTH), Gt, states, small, wcomb, wgate,
                                          Ls, n_batch, R)
        y = _out(x2d, ym.reshape(-1, D_MODEL), modx, n2, nf, w_out_b, w_up_b, w_down_b, TM, rows_per_mod)
        return y, shout, wkv_o, hg_o

    yp, shp, wkvp, hgp = trunk(x_prompt.reshape(BP * TP, D_MODEL), mod_p, TP, BP, None, CHUNK, BP, OUT_ROWS_PROMPT,
                               PROMPT_GROUPS_PER_STEP, True)
    y_prompt = yp.reshape(BP, TP, D_MODEL)
    shift_p = shp[:, -1][None]
    wkv_p = wkvp[None]
    hgrn_p = hgp[None]

    n_seq = CHUNK // TS
    GS = BS // n_seq
    shinit_s = jnp.zeros((GS, n_seq, TS, SHIFT_WIDTH), F32).at[:, :, 0].set(
        state_shift[l].reshape(GS, n_seq, SHIFT_WIDTH)).reshape(GS, CHUNK, SHIFT_WIDTH)
    ys, shs, wkvs, hgs = trunk(x_sample.reshape(BS * TS, D_MODEL), mod_s, TS, GS,
                               (shinit_s, state_wkv[l], state_hgrn[l]), TS, BS, ROWS_SAMPLE, 1, False)
    y_sample = ys.reshape(BS, TS, D_MODEL)
    shift_s = shs.reshape(GS, n_seq, TS, SHIFT_WIDTH)[:, :, TS - 1].reshape(BS, SHIFT_WIDTH)[None]
    wkv_s = wkvs[None]
    hgrn_s = hgs[None]

    return (y_prompt, y_sample, shift_p, wkv_p, hgrn_p, shift_s, wkv_s, hgrn_s)
```

```python
import functools

import jax
import jax.numpy as jnp
from jax import lax
from jax.experimental import pallas as pl
from jax.experimental.pallas import tpu as pltpu

F32 = jnp.float32
BF16 = jnp.bfloat16

D_MODEL = 1024
RWKV_HEADS = 8
RWKV_WIDTH = 512
HGRN_HEADS = 4
HGRN_WIDTH = 512
RWKV_HEAD_DIM = 64
D_DECAY_LORA = 64
D_AAA_LORA = 64
D_GATE_LORA = 128
SHIFT_WIDTH = 1792
IN_WIDTH = 3840
D_FF = 4096
COL_K = RWKV_WIDTH
COL_V = 2 * RWKV_WIDTH
COL_LORA = 3 * RWKV_WIDTH
COL_GATE = COL_LORA + D_DECAY_LORA + D_AAA_LORA
COL_Q = SHIFT_WIDTH
COL_F = COL_Q + HGRN_WIDTH
COL_I = COL_F + HGRN_WIDTH
COL_OG = COL_I + HGRN_WIDTH
NORM_EPS = 1e-6
LNX_EPS = 64e-5

LANES = 128
HEAD_PAIR = 2 * RWKV_HEAD_DIM
CHUNK = 64
SUB = 16
PROMPT_GROUPS_PER_STEP = 8
INPROJ_TILE = 256
FF_TILE = 1024
OUT_ROWS_PROMPT = 512
ROWS_SAMPLE = 256
V7X_VMEM_BYTES = 64 * 1024 * 1024
VMEM_LIMIT = V7X_VMEM_BYTES - 2 * 1024 * 1024
DENSE_VMEM_LIMIT = 48 * 1024 * 1024


def _dot(a, b):
    return jnp.dot(a.astype(BF16), b.astype(BF16), preferred_element_type=F32)


def _dot_nt(a, b):
    return lax.dot_general(a.astype(BF16), b.astype(BF16), (((1,), (1,)), ((), ())),
                           preferred_element_type=F32)


def _dot_2pass_l(m, x):
    mb = m.astype(BF16)
    hi = x.astype(BF16)
    lo = (x - hi.astype(F32)).astype(BF16)
    return jnp.dot(mb, hi, preferred_element_type=F32) + jnp.dot(mb, lo, preferred_element_type=F32)


def _repeat_rows(m, rep):
    if rep == 1:
        return m
    n = m.shape[0]
    sel = (_shr(_iota((n * rep, n), 0), rep) == _iota((n * rep, n), 1)).astype(BF16)
    hi = m.astype(BF16)
    r1 = m - hi.astype(F32)
    mid = r1.astype(BF16)
    lo = (r1 - mid.astype(F32)).astype(BF16)
    d = lambda part: jnp.dot(sel, part, preferred_element_type=F32)
    return d(hi) + d(mid) + d(lo)


def _group_sums(xs, gmat):
    n, rows = len(xs), xs[0].shape[0]
    P = _dot(jnp.concatenate(xs, axis=0), gmat)
    return [P[i * rows:(i + 1) * rows] for i in range(n)]


def _iota(shape, dim):
    return lax.broadcasted_iota(jnp.int32, shape, dim)


def _shr(x, n):
    return lax.shift_right_logical(x, jnp.int32(n.bit_length() - 1))


def _sigmoid(x):
    return 1.0 / (1.0 + jnp.exp(-x))


def _silu(x):
    return x * _sigmoid(x)


def _softplus(x):
    return jnp.maximum(x, 0.0) + jnp.log(1.0 + jnp.exp(-jnp.abs(x)))


def _rms(x, gain):
    return x * lax.rsqrt(jnp.mean(x * x, axis=-1, keepdims=True) + NORM_EPS) * gain


def _make_masks(C, Ls):
    row = _iota((C, C), 0)
    col = _iota((C, C), 1)
    seq = lambda t: _shr(t, Ls)
    blk = lambda t: _shr(t, SUB)
    same = seq(row) == seq(col)
    R = _iota((2 * C, 2 * C), 0)
    Q = _iota((2 * C, 2 * C), 1)
    tr = R & (C - 1)
    tq = Q & (C - 1)
    same2 = seq(tr) == seq(tq)
    lvl = []
    s = 1
    while s < Ls:
        lvl.append((_shr(R, 2 * s) == _shr(Q, 2 * s)) & ((R & (2 * s - 1)) >= s) & ((Q & (2 * s - 1)) < s))
        s *= 2
    lane2 = _iota((2 * C, HEAD_PAIR), 1)
    row2 = _iota((2 * C, HEAD_PAIR), 0)
    pr = _iota((HEAD_PAIR, HEAD_PAIR), 0)
    pc = _iota((HEAD_PAIR, HEAD_PAIR), 1)
    return dict(
        row=row, col=col, same=same,
        mi=same & (col <= row),
        diag_blocks=same & (col <= row) & (blk(row) == blk(col)),
        off_blocks=[same & (blk(col) == j) & (blk(row) > j) for j in range(C // SUB - 1)],
        eye2=(R == Q).astype(F32),
        lvl=lvl,
        ms_hi2=same2 & (tq < tr) & (Q >= C),
        mi22=same2 & (tq <= tr),
        lo_half=_iota((C, HEAD_PAIR), 1) < RWKV_HEAD_DIM,
        own=(lane2 < RWKV_HEAD_DIM) == (row2 < C),
        top_left=(Q < C) & (R < C), bottom_left=(Q < C) & (R >= C),
        seq_of_row2=seq(row2 & (C - 1)),
        seq_of_row=seq(_iota((C, LANES), 0)),
        head_diag=_shr(pr, RWKV_HEAD_DIM) == _shr(pc, RWKV_HEAD_DIM),
    )


def _tri_inverse(G, mk):
    lvl = mk['lvl']
    D = mk['eye2'] + jnp.where(lvl[0], G, 0.0)
    for m in lvl[1:]:
        LD = _dot(jnp.where(m, G, 0.0), D)
        yield
        D = D + _dot(D, LD)
        yield
    return D


def _lockstep(streams):
    streams = list(streams)
    out = [None] * len(streams)
    alive = list(range(len(streams)))
    while alive:
        for i in list(alive):
            try:
                next(streams[i])
            except StopIteration as stop:
                out[i] = stop.value
                alive.remove(i)
    return out


def _drain(gen):
    while True:
        try:
            next(gen)
        except StopIteration as stop:
            return stop.value


def _select_seq(full, n_seq, sid_rows):
    if n_seq == 1:
        return full
    acc = jnp.where(sid_rows == 0, full[:, 0:LANES], 0.0)
    for s in range(1, n_seq):
        acc = acc + jnp.where(sid_rows == s, full[:, s * LANES:(s + 1) * LANES], 0.0)
    return acc


def _expand_seq(x, n_seq, sid_rows):
    if n_seq == 1:
        return x
    return jnp.concatenate([jnp.where(sid_rows == s, x, 0.0) for s in range(n_seq)], axis=1)


def _wkv_pair_chunk(r, lw, cw, wl, k, v, kk, a, S, mk, Ls):
    C = r.shape[0]
    n_seq = C // Ls
    lo_half = mk['lo_half']
    stack = lambda t: jnp.where(mk['own'], jnp.concatenate([t, t], axis=0), 0.0)
    e_neg = jnp.exp(-cw)
    at = -kk * jnp.exp(cw - lw)
    bt = kk * a * e_neg
    kt = k * e_neg
    rt = r * jnp.exp(cw)
    Y = jnp.concatenate([bt, kt], axis=0)
    gar = _dot_nt(jnp.concatenate([stack(at), stack(rt)], axis=0), Y)
    ga = gar[:2 * C]
    G = jnp.where(mk['top_left'], ga, 0.0) + pltpu.roll(jnp.where(mk['bottom_left'], ga, 0.0), C, 1)
    gr = gar[2 * C:]
    sid2 = mk['seq_of_row2']
    XO = _select_seq(_dot_nt(jnp.concatenate([at, rt], axis=0), S), n_seq, sid2)
    yield
    Vz = jnp.concatenate([jnp.zeros_like(v), v], axis=0)
    xk = _dot(jnp.where(mk['ms_hi2'], ga, 0.0), Vz)
    D = yield from _tri_inverse(G, mk)
    X = XO[:C] + jnp.where(lo_half, xk[:C], xk[C:])
    DX = _dot(D, jnp.concatenate([X, X], axis=0))
    yield
    U = jnp.where(lo_half, DX[:C], DX[C:])
    Z = jnp.concatenate([U, v], axis=0)
    oz = _dot(jnp.where(mk['mi22'], gr, 0.0), Z)
    upd = _dot(Z.T, _expand_seq(Y, n_seq, sid2))
    yield
    o = XO[C:] + jnp.where(lo_half, oz[:C], oz[C:])
    bd = mk['head_diag']
    new = []
    for s in range(n_seq):
        Ss = S[s * LANES:(s + 1) * LANES, :]
        new.append(jnp.where(bd, (Ss + upd[:, s * LANES:(s + 1) * LANES]) * wl[s * Ls:s * Ls + 1, :], 0.0))
    return o, new


def _hgrn_head_chunk(q, kf, b, blast, iv, ST, mk, Ls):
    C = q.shape[0]
    assert Ls == C or Ls <= SUB
    n_seq = C // Ls
    sid = mk['seq_of_row']
    if Ls > SUB:
        rows_of = lambda off: jnp.concatenate(
            [jnp.broadcast_to(b[SUB * j + off:SUB * j + off + 1, :], (SUB, LANES)) for j in range(C // SUB)], axis=0)
        bmid = rows_of(SUB // 2 - 1)
        bend = rows_of(SUB - 1)
    else:
        bmid = 0.0
    o = _select_seq(_dot_nt(q * jnp.exp(b), ST), n_seq, sid)
    att = jnp.where(mk['diag_blocks'],
                    _dot_nt(q * jnp.exp(b - bmid), kf * jnp.exp(bmid - b)), 0.0)
    if Ls > SUB:
        ko = kf * jnp.exp(jnp.minimum(bend - b, 0.0))
        nb = C // SUB - 1
        qo = [q * jnp.exp(jnp.minimum(b - b[SUB * j + SUB - 1:SUB * j + SUB, :], 0.0)) for j in range(nb)]
        off = _dot_nt(jnp.concatenate(qo, axis=0), ko)
        for j in range(nb):
            att = att + jnp.where(mk['off_blocks'][j], off[j * C:(j + 1) * C], 0.0)
    ke = kf * jnp.exp(blast - b)
    upd = _dot(iv.T, _expand_seq(ke, n_seq, sid))
    yield
    o = o + _dot(att, iv)
    yield
    dec = jnp.exp(blast)
    new = []
    for s in range(n_seq):
        new.append(ST[s * LANES:(s + 1) * LANES, :] * dec[s * Ls:s * Ls + 1, :]
                   + upd[:, s * LANES:(s + 1) * LANES])
    return o, new


def _ada_body(c_ref, w_ref, b_ref, o_ref):
    o_ref[...] = _dot(_silu(c_ref[...]), w_ref[...]) + b_ref[...]


def _inproj_body(rep, x_ref, sh_ref, sc_ref, n1_ref, w_ref, p_ref):
    h = _rms(x_ref[...], n1_ref[...]) * (1.0 + _repeat_rows(sc_ref[...], rep)) + _repeat_rows(sh_ref[...], rep)
    p_ref[...] = jnp.dot(h.astype(BF16), w_ref[...], preferred_element_type=F32)


PREP_FIELDS = ("r", "lw", "kmod", "v", "kk", "a", "g", "bonus", "q", "kf", "iv", "og")


def _rec_body(Ls, single_seq, R, fuse_in, *refs):
    refs = list(refs)
    if fuse_in:
        xfirst_ref, xnext_ref, sh1_ref, sc1_ref, n1_ref, win_ref = refs[:6]
        del refs[:6]
        p_ref = None
    else:
        p_ref = refs.pop(0)
    if single_seq:
        shinit_ref = wkvin_ref = hgin_ref = None
    else:
        shinit_ref, wkvin_ref, hgin_ref = refs[:3]
        del refs[:3]
    (mu_ref, w0_ref, a0_ref, kk_ref, ka_ref, rk_ref, lnw_ref, lnb_ref, lb_ref, gn_ref, wcomb_ref, wgate_ref,
     y_ref, shout_ref, wkvout_ref, hgout_ref, wkv_sc, hg_sc) = refs[:18]
    if fuse_in:
        assert single_seq
        prep_sc = dict(zip(PREP_FIELDS, refs[18:18 + len(PREP_FIELDS)]))
        cum_sc = refs[18 + len(PREP_FIELDS)]
    C = CHUNK
    n_seq = C // Ls
    NP = RWKV_HEADS // 2
    HD = RWKV_HEAD_DIM
    c = pl.program_id(1)
    rows = lambda j: slice(j * C, (j + 1) * C)
    blk = lambda s: slice(s * LANES, (s + 1) * LANES)
    pairs = [slice(hp * HEAD_PAIR, (hp + 1) * HEAD_PAIR) for hp in range(NP)]
    mk = _make_masks(C, Ls)
    gsum = mk['head_diag'].astype(F32)

    def prepare(pcols, first):
        rowv = _iota((C, SHIFT_WIDTH), 0)
        xs = []
        for j in range(R):
            p_rw = pcols(j, 0, SHIFT_WIDTH)
            keep = shout_ref.shape[1]
            p_prev = jnp.where(rowv == 0, shout_ref[j, keep - 1:keep, :], pltpu.roll(p_rw, 1, 0))
            if not single_seq:
                p_prev = jnp.where((rowv & (Ls - 1)) == 0, shinit_ref[j], p_prev)
            elif first:
                p_prev = jnp.where(rowv == 0, 0.0, p_prev)
            shout_ref[j] = p_rw[C - keep:, :]
            xs.append(p_rw + (p_prev - p_rw) * mu_ref[...])
        x = jnp.concatenate(xs, axis=0)
        cat = lambda lo, hi: jnp.concatenate([pcols(j, lo, hi) for j in range(R)], axis=0)
        r = x[:, 0:COL_K]
        k = x[:, COL_K:COL_V]
        v = x[:, COL_V:COL_LORA]
        wa = x[:, COL_LORA:COL_GATE]
        gd = x[:, COL_GATE:SHIFT_WIDTH]
        lane = _iota((R * C, LANES), 1)
        da = _dot(jnp.where(lane < D_DECAY_LORA, jnp.tanh(wa), wa), wcomb_ref[...])
        g = _dot(_sigmoid(gd), wgate_ref[...])
        yield
        w_log = -_softplus(-(w0_ref[...] + da[:, :RWKV_WIDTH])) - 0.5
        lw = -jnp.exp(w_log)
        a = _sigmoid(a0_ref[...] + da[:, RWKV_WIDTH:])
        kkr = k * kk_ref[...]
        kmod = k * (1.0 + (a - 1.0) * ka_ref[...])
        rkr = r * kmod * rk_ref[...]
        lbp = lb_ref[...]
        m = jnp.maximum(lbp[0:1, :], lbp[1:2, :])
        e0 = jnp.exp(lbp[0:1, :] - m)
        e1 = jnp.exp(lbp[1:2, :] - m)
        lb = e0 / (e0 + e1)
        q = _silu(cat(COL_Q, COL_F))
        f = lb + (1.0 - lb) * _sigmoid(cat(COL_F, COL_I))
        logs = jnp.concatenate([lw, jnp.log(f)], axis=1)
        cums, tots = [], []
        for j in range(R):
            if single_seq:
                cums.append(_dot_2pass_l(mk['mi'].astype(F32), logs[rows(j)]))
            else:
                both = _dot_2pass_l(jnp.concatenate([mk['mi'], mk['same']], axis=0).astype(F32), logs[rows(j)])
                cums.append(both[:C])
                tots.append(both[C:])
        sums = _group_sums([kkr[:, sl] * kkr[:, sl] for sl in pairs] + [rkr[:, sl] for sl in pairs], gsum)
        yield
        kk = jnp.concatenate([kkr[:, sl] / jnp.maximum(jnp.sqrt(sums[hp]), 1e-12)
                              for hp, sl in enumerate(pairs)], axis=1)
        bonus = jnp.concatenate(sums[NP:], axis=1) * v
        d = dict(r=r, lw=lw, kmod=kmod, v=v, kk=kk, a=a, g=g, bonus=bonus, q=q, kf=1.0 - f,
                 iv=cat(COL_I, COL_OG), og=cat(COL_OG, IN_WIDTH))
        return d, jnp.concatenate(cums, axis=0), tots

    def next_chunk(x_ref, first):
        hs = [_rms(x_ref[j], n1_ref[...]) * (1.0 + sc1_ref[j]) + sh1_ref[j] for j in range(R)]
        hn = jnp.concatenate(hs, axis=0).astype(BF16)
        tiles = []
        for t in range(IN_WIDTH // INPROJ_TILE):
            tiles.append(jnp.dot(hn, win_ref[:, t * INPROJ_TILE:(t + 1) * INPROJ_TILE],
                                 preferred_element_type=F32))
            yield
        pcols = lambda j, lo, hi: jnp.concatenate(tiles[lo // INPROJ_TILE:hi // INPROJ_TILE], axis=1)[rows(j)]
        d, cum, _ = yield from prepare(pcols, first)
        return d, cum

    def hand_over(d, cum):
        for name in PREP_FIELDS:
            prep_sc[name][...] = d[name]
        cum_sc[...] = cum

    @pl.when(c == 0)
    def _():
        shout_ref[...] = jnp.zeros_like(shout_ref)
        wkv_sc[...] = jnp.zeros_like(wkv_sc)
        if single_seq:
            hg_sc[...] = jnp.zeros_like(hg_sc)
        else:
            for j in range(R):
                for s in range(n_seq):
                    for hp in range(NP):
                        wkv_sc[j, hp, s * LANES:s * LANES + HD, 0:HD] = wkvin_ref[j * n_seq + s, 2 * hp]
                        wkv_sc[j, hp, s * LANES + HD:(s + 1) * LANES, HD:2 * HD] = wkvin_ref[j * n_seq + s, 2 * hp + 1]
                    for h in range(HGRN_HEADS):
                        hg_sc[j, h, blk(s), :] = hgin_ref[j * n_seq + s, h].T
        if fuse_in:
            hand_over(*_drain(next_chunk(xfirst_ref, True)))

    if fuse_in:
        d = {name: prep_sc[name][...] for name in PREP_FIELDS}
        cum = cum_sc[...]
        extra = [next_chunk(xnext_ref, False)]
    else:
        d, cum, tots = _drain(prepare(lambda j, lo, hi: p_ref[j, :, lo:hi], True))
        extra = []
    if single_seq:
        tots = [cum[j * C + C - 1:j * C + C, :] for j in range(R)]
    lnw = lnw_ref[...]
    lnb = lnb_ref[...]

    def wkv_stream(j, hp):
        sl = pairs[hp]
        o, new = yield from _wkv_pair_chunk(d["r"][rows(j), sl], d["lw"][rows(j), sl], cum[rows(j), sl],
                                            jnp.exp(tots[j][:, sl]), d["kmod"][rows(j), sl], d["v"][rows(j), sl],
                                            d["kk"][rows(j), sl], d["a"][rows(j), sl], wkv_sc[j, hp], mk, Ls)
        for s in range(n_seq):
            wkv_sc[j, hp, blk(s), :] = new[s]
        return o

    def hgrn_stream(j, h):
        sl = slice(RWKV_WIDTH + h * LANES, RWKV_WIDTH + (h + 1) * LANES)
        hs = blk(h)
        o, new = yield from _hgrn_head_chunk(d["q"][rows(j), hs], d["kf"][rows(j), hs], cum[rows(j), sl],
                                             tots[j][:, sl], d["iv"][rows(j), hs], hg_sc[j, h], mk, Ls)
        for s in range(n_seq):
            hg_sc[j, h, blk(s), :] = new[s]
        on = _rms(o, gn_ref[...])
        y_ref[j, :, sl] = (on * _silu(d["og"][rows(j), hs])).astype(y_ref.dtype)

    outs = _lockstep([wkv_stream(j, hp) for j in range(R) for hp in range(NP)]
                     + [hgrn_stream(j, h) for j in range(R) for h in range(HGRN_HEADS)] + extra)

    os_ = [jnp.concatenate([outs[j * NP + hp] for j in range(R)], axis=0) for hp in range(NP)]
    means = _group_sums(os_, gsum)
    ds = [o - mu * (1.0 / RWKV_HEAD_DIM) for o, mu in zip(os_, means)]
    vars_ = _group_sums([dd * dd for dd in ds], gsum)
    for hp, sl in enumerate(pairs):
        on = ds[hp] * lax.rsqrt(vars_[hp] * (1.0 / RWKV_HEAD_DIM) + LNX_EPS) * lnw[:, sl] + lnb[:, sl]
        yv = ((on + d["bonus"][:, sl]) * d["g"][:, sl]).astype(y_ref.dtype)
        for j in range(R):
            y_ref[j, :, sl] = yv[rows(j)]

    if fuse_in:
        hand_over(*outs[-1])

    @pl.when(c == pl.num_programs(1) - 1)
    def _():
        for j in range(R):
            for s in range(n_seq):
                for hp in range(NP):
                    wkvout_ref[j * n_seq + s, 2 * hp] = wkv_sc[j, hp, s * LANES:s * LANES + HD, 0:HD]
                    wkvout_ref[j * n_seq + s, 2 * hp + 1] = wkv_sc[j, hp, s * LANES + HD:(s + 1) * LANES, HD:2 * HD]
                for h in range(HGRN_HEADS):
                    hgout_ref[j * n_seq + s, h] = hg_sc[j, h, blk(s), :].T


def _out_body(rep, x_ref, ym_ref, gt1_ref, sh2_ref, sc2_ref, gt2_ref, n2_ref, nf_ref,
              wo_ref, wu_ref, wd_ref, o_ref):
    gt1, sh2, sc2, gt2 = (_repeat_rows(m[...], rep) for m in (gt1_ref, sh2_ref, sc2_ref, gt2_ref))
    y = jnp.dot(ym_ref[...], wo_ref[...], preferred_element_type=F32)
    x1 = x_ref[...] + gt1 * y
    h = (_rms(x1, n2_ref[...]) * (1.0 + sc2) + sh2).astype(BF16)
    acc = jnp.zeros_like(x1)
    FC = FF_TILE
    for j in range(D_FF // FC):
        u = jnp.dot(h, wu_ref[:, j * FC:(j + 1) * FC], preferred_element_type=F32)
        u = jnp.square(jnp.maximum(u, 0.0)).astype(BF16)
        acc = acc + jnp.dot(u, wd_ref[j * FC:(j + 1) * FC, :], preferred_element_type=F32)
    x2 = x1 + gt2 * acc
    o_ref[...] = _rms(x2, nf_ref[...])


def _params(sem, vmem_limit=VMEM_LIMIT):
    return pltpu.CompilerParams(dimension_semantics=sem, vmem_limit_bytes=vmem_limit)


def _full(shape):
    return pl.BlockSpec(shape, lambda *_: (0,) * len(shape))


def _ada(c_all, w_ada, b_ada):
    n = c_all.shape[0]
    TN = 1024
    return pl.pallas_call(
        _ada_body,
        grid=(w_ada.shape[1] // TN,),
        in_specs=[pl.BlockSpec((n, D_MODEL), lambda j: (0, 0)),
                  pl.BlockSpec((D_MODEL, TN), lambda j: (0, j)),
                  pl.BlockSpec((1, TN), lambda j: (0, j))],
        out_specs=pl.BlockSpec((n, TN), lambda j: (0, j)),
        out_shape=jax.ShapeDtypeStruct((n, w_ada.shape[1]), F32),
        compiler_params=_params(("arbitrary",), DENSE_VMEM_LIMIT),
        name="ada",
    )(c_all, w_ada, b_ada)


def _mod_specs(mod, cols, TM, rows_per_mod):
    if mod.ndim == 3:
        return [pl.BlockSpec((None, 1, D_MODEL), lambda i, c=c: (i * TM // rows_per_mod, 0, c)) for c in cols]
    return [pl.BlockSpec((TM // rows_per_mod, D_MODEL), lambda i, c=c: (i, c)) for c in cols]


def _mod_repeat(mod, rows_per_mod):
    return 1 if mod.ndim == 3 else rows_per_mod


def _inproj(x, mod, norm1, w_in, TM, rows_per_mod):
    M = x.shape[0]
    return pl.pallas_call(
        functools.partial(_inproj_body, _mod_repeat(mod, rows_per_mod)),
        grid=(M // TM,),
        in_specs=[pl.BlockSpec((TM, D_MODEL), lambda i: (i, 0))]
                 + _mod_specs(mod, (0, 1), TM, rows_per_mod)
                 + [_full((1, D_MODEL)), _full((D_MODEL, IN_WIDTH))],
        out_specs=pl.BlockSpec((TM, IN_WIDTH), lambda i: (i, 0)),
        out_shape=jax.ShapeDtypeStruct((M, IN_WIDTH), F32),
        compiler_params=_params(("arbitrary",), DENSE_VMEM_LIMIT),
        name="inproj",
    )(x, mod, mod, norm1, w_in)


def _rec(p, n_groups, states, small, wcomb, wgate, Ls, n_batch, R, inproj=None):
    Gt, C = n_groups, CHUNK
    rows_total = (p if inproj is None else inproj[0]).shape[1]
    NC = rows_total // C
    n_seq = C // Ls
    keep = 8 if states is None else C
    wkv_spec = pl.BlockSpec((R * n_seq, RWKV_HEADS, RWKV_HEAD_DIM, RWKV_HEAD_DIM), lambda g, c: (g, 0, 0, 0))
    hg_spec = pl.BlockSpec((R * n_seq, HGRN_HEADS, LANES, LANES), lambda g, c: (g, 0, 0, 0))
    sh_spec = pl.BlockSpec((R, C, SHIFT_WIDTH), lambda g, c: (g, 0, 0))
    sho_spec = pl.BlockSpec((R, keep, SHIFT_WIDTH), lambda g, c: (g, 0, 0))
    st_in = [] if states is None else list(states)
    st_specs = [] if states is None else [sh_spec, wkv_spec, hg_spec]
    scratch = [pltpu.VMEM((R, RWKV_HEADS // 2, n_seq * LANES, LANES), F32),
               pltpu.VMEM((R, HGRN_HEADS, n_seq * LANES, LANES), F32)]
    if inproj is None:
        lead_in = [p]
        lead_specs = [pl.BlockSpec((R, C, IN_WIDTH), lambda g, c: (g, c, 0))]
    else:
        x, mod, norm1, w_in = inproj
        lead_in = [x, x, mod, mod, norm1, w_in]
        lead_specs = [pl.BlockSpec((R, C, D_MODEL), lambda g, c: (g, 0, 0)),
                      pl.BlockSpec((R, C, D_MODEL), lambda g, c: (g, jnp.minimum(c + 1, NC - 1), 0)),
                      pl.BlockSpec((R, 1, D_MODEL), lambda g, c: (g, 0, 0)),
                      pl.BlockSpec((R, 1, D_MODEL), lambda g, c: (g, 0, 1)),
                      _full(norm1.shape), _full(w_in.shape)]
        scratch += [pltpu.VMEM((R * C, RWKV_WIDTH), F32) for _ in PREP_FIELDS]
        scratch.append(pltpu.VMEM((R * C, 2 * RWKV_WIDTH), F32))
    return pl.pallas_call(
        functools.partial(_rec_body, Ls, states is None, R, inproj is not None),
        grid=(Gt // R, NC),
        in_specs=lead_specs + st_specs
                 + [_full(s.shape) for s in small] + [_full(wcomb.shape), _full(wgate.shape)],
        out_specs=[pl.BlockSpec((R, C, D_MODEL), lambda g, c: (g, c, 0)), sho_spec, wkv_spec, hg_spec],
        out_shape=[jax.ShapeDtypeStruct((Gt, NC * C, D_MODEL), BF16),
                   jax.ShapeDtypeStruct((Gt, keep, SHIFT_WIDTH), F32),
                   jax.ShapeDtypeStruct((n_batch, RWKV_HEADS, RWKV_HEAD_DIM, RWKV_HEAD_DIM), F32),
                   jax.ShapeDtypeStruct((n_batch, HGRN_HEADS, LANES, LANES), F32)],
        scratch_shapes=scratch,
        compiler_params=_params(("arbitrary", "arbitrary")),
        name="rec",
    )(*lead_in, *st_in, *small, wcomb, wgate)


def _out(x, ym, mod, norm2, norm_f, w_out, w_up, w_down, TM, rows_per_mod):
    M = x.shape[0]
    row = lambda i: (i, 0)
    return pl.pallas_call(
        functools.partial(_out_body, _mod_repeat(mod, rows_per_mod)),
        grid=(M // TM,),
        in_specs=[pl.BlockSpec((TM, D_MODEL), row), pl.BlockSpec((TM, D_MODEL), row)]
                 + _mod_specs(mod, (2, 3, 4, 5), TM, rows_per_mod)
                 + [_full((1, D_MODEL)), _full((1, D_MODEL)),
                    _full(w_out.shape), _full(w_up.shape), _full(w_down.shape)],
        out_specs=pl.BlockSpec((TM, D_MODEL), row),
        out_shape=jax.ShapeDtypeStruct((M, D_MODEL), F32),
        compiler_params=_params(("arbitrary",), DENSE_VMEM_LIMIT),
        name="outmlp",
    )(x, ym, mod, mod, mod, mod, norm2, norm_f, w_out, w_up, w_down)


def kernel(x_prompt, x_sample, c_prompt, c_sample, state_shift, state_wkv, state_hgrn, norm1, norm2, norm_f, w_ada, b_ada, w_in, mu_shift, w0, w_decay_up, a0, w_aaa_up, w_gate_up, k_k, k_a, r_k, lnx_w, lnx_b, hgrn_lb, hgrn_gnorm, w_out, w_up, w_down):
    BP, TP, _ = x_prompt.shape
    BS, TS, _ = x_sample.shape
    l = 0
    row = lambda t: t.reshape(1, -1)

    mod = _ada(jnp.concatenate([c_prompt, c_sample], axis=0), w_ada[l], row(b_ada[l]))
    mod_p = mod[:BP].reshape(BP, 1, 6 * D_MODEL)
    mod_s = mod[BP:]

    w_in_b = w_in[l].astype(BF16)
    w_out_b = w_out[l].astype(BF16)
    w_up_b = w_up[l].astype(BF16)
    w_down_b = w_down[l].astype(BF16)
    zer = jnp.zeros((D_DECAY_LORA, RWKV_WIDTH), F32)
    wcomb = jnp.concatenate([jnp.concatenate([w_decay_up[l], zer], axis=1),
                             jnp.concatenate([zer, w_aaa_up[l]], axis=1)], axis=0).astype(BF16)
    wgate = w_gate_up[l].astype(BF16)
    small = [row(mu_shift[l]), row(w0[l]), row(a0[l]), row(k_k[l]), row(k_a[l]), row(r_k[l]),
             row(lnx_w[l]), row(lnx_b[l]), hgrn_lb, row(hgrn_gnorm[l])]
    n1, n2, nf = row(norm1[l]), row(norm2[l]), row(norm_f)

    def trunk(x2d, modx, rows_per_mod, Gt, states, Ls, n_batch, TM, R, fuse_in):
        if fuse_in:
            ym, shout, wkv_o, hg_o = _rec(None, Gt, states, small, wcomb, wgate, Ls, n_batch, R,
                                          inproj=(x2d.reshape(Gt, -1, D_MODEL), modx, n1, w_in_b))
        else:
            p = _inproj(x2d, modx, n1, w_in_b, TM, rows_per_mod)
            ym, shout, wkv_o, hg_o = _rec(p.reshape(Gt, -1, IN_WIDTH), Gt, states, small, wcomb, wgate,
                                          Ls, n_batch, R)
        y = _out(x2d, ym.reshape(-1, D_MODEL), modx, n2, nf, w_out_b, w_up_b, w_down_b, TM, rows_per_mod)
        return y, shout, wkv_o, hg_o

    yp, shp, wkvp, hgp = trunk(x_prompt.reshape(BP * TP, D_MODEL), mod_p, TP, BP, None, CHUNK, BP, OUT_ROWS_PROMPT,
                               PROMPT_GROUPS_PER_STEP, True)
    y_prompt = yp.reshape(BP, TP, D_MODEL)
    shift_p = shp[:, -1][None]
    wkv_p = wkvp[None]
    hgrn_p = hgp[None]

    n_seq = CHUNK // TS
    GS = BS // n_seq
    shinit_s = jnp.zeros((GS, n_seq, TS, SHIFT_WIDTH), F32).at[:, :, 0].set(
        state_shift[l].reshape(GS, n_seq, SHIFT_WIDTH)).reshape(GS, CHUNK, SHIFT_WIDTH)
    ys, shs, wkvs, hgs = trunk(x_sample.reshape(BS * TS, D_MODEL), mod_s, TS, GS,
                               (shinit_s, state_wkv[l], state_hgrn[l]), TS, BS, ROWS_SAMPLE, 1, False)
    y_sample = ys.reshape(BS, TS, D_MODEL)
    shift_s = shs.reshape(GS, n_seq, TS, SHIFT_WIDTH)[:, :, TS - 1].reshape(BS, SHIFT_WIDTH)[None]
    wkv_s = wkvs[None]
    hgrn_s = hgs[None]

    return (y_prompt, y_sample, shift_p, wkv_p, hgrn_p, shift_s, wkv_s, hgrn_s)
```

```python
import functools

import jax
import jax.numpy as jnp
from jax import lax
from jax.experimental import pallas as pl
from jax.experimental.pallas import tpu as pltpu

F32 = jnp.float32
BF16 = jnp.bfloat16

D_MODEL = 1024
RWKV_HEADS = 8
RWKV_WIDTH = 512
HGRN_HEADS = 4
HGRN_WIDTH = 512
RWKV_HEAD_DIM = 64
D_DECAY_LORA = 64
D_AAA_LORA = 64
D_GATE_LORA = 128
SHIFT_WIDTH = 1792
IN_WIDTH = 3840
D_FF = 4096
COL_K = RWKV_WIDTH
COL_V = 2 * RWKV_WIDTH
COL_LORA = 3 * RWKV_WIDTH
COL_GATE = COL_LORA + D_DECAY_LORA + D_AAA_LORA
COL_Q = SHIFT_WIDTH
COL_F = COL_Q + HGRN_WIDTH
COL_I = COL_F + HGRN_WIDTH
COL_OG = COL_I + HGRN_WIDTH
NORM_EPS = 1e-6
LNX_EPS = 64e-5

LANES = 128
HEAD_PAIR = 2 * RWKV_HEAD_DIM
CHUNK = 64
SUB = 16
PROMPT_GROUPS_PER_STEP = 8
INPROJ_TILE = 256
FF_TILE = 1024
OUT_ROWS_PROMPT = 512
ROWS_SAMPLE = 256
V7X_VMEM_BYTES = 64 * 1024 * 1024
VMEM_LIMIT = V7X_VMEM_BYTES - 2 * 1024 * 1024


def _dot(a, b):
    return jnp.dot(a.astype(BF16), b.astype(BF16), preferred_element_type=F32)


def _dot_nt(a, b):
    return lax.dot_general(a.astype(BF16), b.astype(BF16), (((1,), (1,)), ((), ())),
                           preferred_element_type=F32)


def _dot_2pass_l(m, x):
    mb = m.astype(BF16)
    hi = x.astype(BF16)
    lo = (x - hi.astype(F32)).astype(BF16)
    return jnp.dot(mb, hi, preferred_element_type=F32) + jnp.dot(mb, lo, preferred_element_type=F32)


def _repeat_rows(m, rep):
    if rep == 1:
        return m
    n = m.shape[0]
    sel = (_shr(_iota((n * rep, n), 0), rep) == _iota((n * rep, n), 1)).astype(BF16)
    hi = m.astype(BF16)
    r1 = m - hi.astype(F32)
    mid = r1.astype(BF16)
    lo = (r1 - mid.astype(F32)).astype(BF16)
    d = lambda part: jnp.dot(sel, part, preferred_element_type=F32)
    return d(hi) + d(mid) + d(lo)


def _group_sums(xs, gmat):
    n, rows = len(xs), xs[0].shape[0]
    P = _dot(jnp.concatenate(xs, axis=0), gmat)
    return [P[i * rows:(i + 1) * rows] for i in range(n)]


def _iota(shape, dim):
    return lax.broadcasted_iota(jnp.int32, shape, dim)


def _shr(x, n):
    return lax.shift_right_logical(x, jnp.int32(n.bit_length() - 1))


def _sigmoid(x):
    return 1.0 / (1.0 + jnp.exp(-x))


def _silu(x):
    return x * _sigmoid(x)


def _softplus(x):
    return jnp.maximum(x, 0.0) + jnp.log(1.0 + jnp.exp(-jnp.abs(x)))


def _rms(x, gain):
    return x * lax.rsqrt(jnp.mean(x * x, axis=-1, keepdims=True) + NORM_EPS) * gain


def _make_masks(C, Ls):
    row = _iota((C, C), 0)
    col = _iota((C, C), 1)
    seq = lambda t: _shr(t, Ls)
    blk = lambda t: _shr(t, SUB)
    same = seq(row) == seq(col)
    R = _iota((2 * C, 2 * C), 0)
    Q = _iota((2 * C, 2 * C), 1)
    tr = R & (C - 1)
    tq = Q & (C - 1)
    same2 = seq(tr) == seq(tq)
    lvl = []
    s = 1
    while s < Ls:
        lvl.append((_shr(R, 2 * s) == _shr(Q, 2 * s)) & ((R & (2 * s - 1)) >= s) & ((Q & (2 * s - 1)) < s))
        s *= 2
    lane2 = _iota((2 * C, HEAD_PAIR), 1)
    row2 = _iota((2 * C, HEAD_PAIR), 0)
    pr = _iota((HEAD_PAIR, HEAD_PAIR), 0)
    pc = _iota((HEAD_PAIR, HEAD_PAIR), 1)
    return dict(
        row=row, col=col, same=same,
        mi=same & (col <= row),
        diag_blocks=same & (col <= row) & (blk(row) == blk(col)),
        off_blocks=[same & (blk(col) == j) & (blk(row) > j) for j in range(C // SUB - 1)],
        eye2=(R == Q).astype(F32),
        lvl=lvl,
        ms_hi2=same2 & (tq < tr) & (Q >= C),
        mi22=same2 & (tq <= tr),
        lo_half=_iota((C, HEAD_PAIR), 1) < RWKV_HEAD_DIM,
        own=(lane2 < RWKV_HEAD_DIM) == (row2 < C),
        top_left=(Q < C) & (R < C), bottom_left=(Q < C) & (R >= C),
        seq_of_row2=seq(row2 & (C - 1)),
        seq_of_row=seq(_iota((C, LANES), 0)),
        head_diag=_shr(pr, RWKV_HEAD_DIM) == _shr(pc, RWKV_HEAD_DIM),
    )


def _tri_inverse(G, mk):
    lvl = mk['lvl']
    D = mk['eye2'] + jnp.where(lvl[0], G, 0.0)
    for m in lvl[1:]:
        LD = _dot(jnp.where(m, G, 0.0), D)
        yield
        D = D + _dot(D, LD)
        yield
    return D


def _lockstep(streams):
    streams = list(streams)
    out = [None] * len(streams)
    alive = list(range(len(streams)))
    while alive:
        for i in list(alive):
            try:
                next(streams[i])
            except StopIteration as stop:
                out[i] = stop.value
                alive.remove(i)
    return out


def _drain(gen):
    while True:
        try:
            next(gen)
        except StopIteration as stop:
            return stop.value


def _select_seq(full, n_seq, sid_rows):
    if n_seq == 1:
        return full
    acc = jnp.where(sid_rows == 0, full[:, 0:LANES], 0.0)
    for s in range(1, n_seq):
        acc = acc + jnp.where(sid_rows == s, full[:, s * LANES:(s + 1) * LANES], 0.0)
    return acc


def _expand_seq(x, n_seq, sid_rows):
    if n_seq == 1:
        return x
    return jnp.concatenate([jnp.where(sid_rows == s, x, 0.0) for s in range(n_seq)], axis=1)


def _wkv_pair_chunk(r, lw, cw, wl, k, v, kk, a, S, mk, Ls):
    C = r.shape[0]
    n_seq = C // Ls
    lo_half = mk['lo_half']
    stack = lambda t: jnp.where(mk['own'], jnp.concatenate([t, t], axis=0), 0.0)
    e_neg = jnp.exp(-cw)
    at = -kk * jnp.exp(cw - lw)
    bt = kk * a * e_neg
    kt = k * e_neg
    rt = r * jnp.exp(cw)
    Y = jnp.concatenate([bt, kt], axis=0)
    gar = _dot_nt(jnp.concatenate([stack(at), stack(rt)], axis=0), Y)
    ga = gar[:2 * C]
    G = jnp.where(mk['top_left'], ga, 0.0) + pltpu.roll(jnp.where(mk['bottom_left'], ga, 0.0), C, 1)
    gr = gar[2 * C:]
    sid2 = mk['seq_of_row2']
    XO = _select_seq(_dot_nt(jnp.concatenate([at, rt], axis=0), S), n_seq, sid2)
    yield
    Vz = jnp.concatenate([jnp.zeros_like(v), v], axis=0)
    xk = _dot(jnp.where(mk['ms_hi2'], ga, 0.0), Vz)
    D = yield from _tri_inverse(G, mk)
    X = XO[:C] + jnp.where(lo_half, xk[:C], xk[C:])
    DX = _dot(D, jnp.concatenate([X, X], axis=0))
    yield
    U = jnp.where(lo_half, DX[:C], DX[C:])
    Z = jnp.concatenate([U, v], axis=0)
    oz = _dot(jnp.where(mk['mi22'], gr, 0.0), Z)
    upd = _dot(Z.T, _expand_seq(Y, n_seq, sid2))
    yield
    o = XO[C:] + jnp.where(lo_half, oz[:C], oz[C:])
    bd = mk['head_diag']
    new = []
    for s in range(n_seq):
        Ss = S[s * LANES:(s + 1) * LANES, :]
        new.append(jnp.where(bd, (Ss + upd[:, s * LANES:(s + 1) * LANES]) * wl[s * Ls:s * Ls + 1, :], 0.0))
    return o, new


def _hgrn_head_chunk(q, kf, b, blast, iv, ST, mk, Ls):
    C = q.shape[0]
    assert Ls == C or Ls <= SUB
    n_seq = C // Ls
    sid = mk['seq_of_row']
    if Ls > SUB:
        rows_of = lambda off: jnp.concatenate(
            [jnp.broadcast_to(b[SUB * j + off:SUB * j + off + 1, :], (SUB, LANES)) for j in range(C // SUB)], axis=0)
        bmid = rows_of(SUB // 2 - 1)
        bend = rows_of(SUB - 1)
    else:
        bmid = 0.0
    o = _select_seq(_dot_nt(q * jnp.exp(b), ST), n_seq, sid)
    att = jnp.where(mk['diag_blocks'],
                    _dot_nt(q * jnp.exp(b - bmid), kf * jnp.exp(bmid - b)), 0.0)
    if Ls > SUB:
        ko = kf * jnp.exp(jnp.minimum(bend - b, 0.0))
        nb = C // SUB - 1
        qo = [q * jnp.exp(jnp.minimum(b - b[SUB * j + SUB - 1:SUB * j + SUB, :], 0.0)) for j in range(nb)]
        off = _dot_nt(jnp.concatenate(qo, axis=0), ko)
        for j in range(nb):
            att = att + jnp.where(mk['off_blocks'][j], off[j * C:(j + 1) * C], 0.0)
    ke = kf * jnp.exp(blast - b)
    upd = _dot(iv.T, _expand_seq(ke, n_seq, sid))
    yield
    o = o + _dot(att, iv)
    yield
    dec = jnp.exp(blast)
    new = []
    for s in range(n_seq):
        new.append(ST[s * LANES:(s + 1) * LANES, :] * dec[s * Ls:s * Ls + 1, :]
                   + upd[:, s * LANES:(s + 1) * LANES])
    return o, new


def _ada_body(c_ref, w_ref, b_ref, o_ref):
    o_ref[...] = _dot(_silu(c_ref[...]), w_ref[...]) + b_ref[...]


def _inproj_body(rep, x_ref, sh_ref, sc_ref, n1_ref, w_ref, p_ref):
    h = _rms(x_ref[...], n1_ref[...]) * (1.0 + _repeat_rows(sc_ref[...], rep)) + _repeat_rows(sh_ref[...], rep)
    p_ref[...] = jnp.dot(h.astype(BF16), w_ref[...], preferred_element_type=F32)


PREP_FIELDS = ("r", "lw", "kmod", "v", "kk", "a", "g", "bonus", "q", "kf", "iv", "og")


def _rec_body(Ls, single_seq, R, fuse_in, *refs):
    refs = list(refs)
    if fuse_in:
        xfirst_ref, xnext_ref, sh1_ref, sc1_ref, n1_ref, win_ref = refs[:6]
        del refs[:6]
        p_ref = None
    else:
        p_ref = refs.pop(0)
    if single_seq:
        shinit_ref = wkvin_ref = hgin_ref = None
    else:
        shinit_ref, wkvin_ref, hgin_ref = refs[:3]
        del refs[:3]
    (mu_ref, w0_ref, a0_ref, kk_ref, ka_ref, rk_ref, lnw_ref, lnb_ref, lb_ref, gn_ref, wcomb_ref, wgate_ref,
     y_ref, shout_ref, wkvout_ref, hgout_ref, wkv_sc, hg_sc) = refs[:18]
    if fuse_in:
        assert single_seq
        prep_sc = dict(zip(PREP_FIELDS, refs[18:18 + len(PREP_FIELDS)]))
        cum_sc = refs[18 + len(PREP_FIELDS)]
    C = CHUNK
    n_seq = C // Ls
    NP = RWKV_HEADS // 2
    HD = RWKV_HEAD_DIM
    c = pl.program_id(1)
    rows = lambda j: slice(j * C, (j + 1) * C)
    blk = lambda s: slice(s * LANES, (s + 1) * LANES)
    pairs = [slice(hp * HEAD_PAIR, (hp + 1) * HEAD_PAIR) for hp in range(NP)]
    mk = _make_masks(C, Ls)
    gsum = mk['head_diag'].astype(F32)

    def prepare(pcols, first):
        rowv = _iota((C, SHIFT_WIDTH), 0)
        xs = []
        for j in range(R):
            p_rw = pcols(j, 0, SHIFT_WIDTH)
            keep = shout_ref.shape[1]
            p_prev = jnp.where(rowv == 0, shout_ref[j, keep - 1:keep, :], pltpu.roll(p_rw, 1, 0))
            if not single_seq:
                p_prev = jnp.where((rowv & (Ls - 1)) == 0, shinit_ref[j], p_prev)
            elif first:
                p_prev = jnp.where(rowv == 0, 0.0, p_prev)
            shout_ref[j] = p_rw[C - keep:, :]
            xs.append(p_rw + (p_prev - p_rw) * mu_ref[...])
        x = jnp.concatenate(xs, axis=0)
        cat = lambda lo, hi: jnp.concatenate([pcols(j, lo, hi) for j in range(R)], axis=0)
        r = x[:, 0:COL_K]
        k = x[:, COL_K:COL_V]
        v = x[:, COL_V:COL_LORA]
        wa = x[:, COL_LORA:COL_GATE]
        gd = x[:, COL_GATE:SHIFT_WIDTH]
        lane = _iota((R * C, LANES), 1)
        da = _dot(jnp.where(lane < D_DECAY_LORA, jnp.tanh(wa), wa), wcomb_ref[...])
        g = _dot(_sigmoid(gd), wgate_ref[...])
        yield
        w_log = -_softplus(-(w0_ref[...] + da[:, :RWKV_WIDTH])) - 0.5
        lw = -jnp.exp(w_log)
        a = _sigmoid(a0_ref[...] + da[:, RWKV_WIDTH:])
        kkr = k * kk_ref[...]
        kmod = k * (1.0 + (a - 1.0) * ka_ref[...])
        rkr = r * kmod * rk_ref[...]
        lbp = lb_ref[...]
        m = jnp.maximum(lbp[0:1, :], lbp[1:2, :])
        e0 = jnp.exp(lbp[0:1, :] - m)
        e1 = jnp.exp(lbp[1:2, :] - m)
        lb = e0 / (e0 + e1)
        q = _silu(cat(COL_Q, COL_F))
        f = lb + (1.0 - lb) * _sigmoid(cat(COL_F, COL_I))
        logs = jnp.concatenate([lw, jnp.log(f)], axis=1)
        cums, tots = [], []
        for j in range(R):
            if single_seq:
                cums.append(_dot_2pass_l(mk['mi'].astype(F32), logs[rows(j)]))
            else:
                both = _dot_2pass_l(jnp.concatenate([mk['mi'], mk['same']], axis=0).astype(F32), logs[rows(j)])
                cums.append(both[:C])
                tots.append(both[C:])
        sums = _group_sums([kkr[:, sl] * kkr[:, sl] for sl in pairs] + [rkr[:, sl] for sl in pairs], gsum)
        yield
        kk = jnp.concatenate([kkr[:, sl] / jnp.maximum(jnp.sqrt(sums[hp]), 1e-12)
                              for hp, sl in enumerate(pairs)], axis=1)
        bonus = jnp.concatenate(sums[NP:], axis=1) * v
        d = dict(r=r, lw=lw, kmod=kmod, v=v, kk=kk, a=a, g=g, bonus=bonus, q=q, kf=1.0 - f,
                 iv=cat(COL_I, COL_OG), og=cat(COL_OG, IN_WIDTH))
        return d, jnp.concatenate(cums, axis=0), tots

    def next_chunk(x_ref, first):
        hs = [_rms(x_ref[j], n1_ref[...]) * (1.0 + sc1_ref[j]) + sh1_ref[j] for j in range(R)]
        hn = jnp.concatenate(hs, axis=0).astype(BF16)
        tiles = []
        for t in range(IN_WIDTH // INPROJ_TILE):
            tiles.append(jnp.dot(hn, win_ref[:, t * INPROJ_TILE:(t + 1) * INPROJ_TILE],
                                 preferred_element_type=F32))
            yield
        pcols = lambda j, lo, hi: jnp.concatenate(tiles[lo // INPROJ_TILE:hi // INPROJ_TILE], axis=1)[rows(j)]
        d, cum, _ = yield from prepare(pcols, first)
        return d, cum

    def hand_over(d, cum):
        for name in PREP_FIELDS:
            prep_sc[name][...] = d[name]
        cum_sc[...] = cum

    @pl.when(c == 0)
    def _():
        shout_ref[...] = jnp.zeros_like(shout_ref)
        wkv_sc[...] = jnp.zeros_like(wkv_sc)
        if single_seq:
            hg_sc[...] = jnp.zeros_like(hg_sc)
        else:
            for j in range(R):
                for s in range(n_seq):
                    for hp in range(NP):
                        wkv_sc[j, hp, s * LANES:s * LANES + HD, 0:HD] = wkvin_ref[j * n_seq + s, 2 * hp]
                        wkv_sc[j, hp, s * LANES + HD:(s + 1) * LANES, HD:2 * HD] = wkvin_ref[j * n_seq + s, 2 * hp + 1]
                    for h in range(HGRN_HEADS):
                        hg_sc[j, h, blk(s), :] = hgin_ref[j * n_seq + s, h].T
        if fuse_in:
            hand_over(*_drain(next_chunk(xfirst_ref, True)))

    if fuse_in:
        d = {name: prep_sc[name][...] for name in PREP_FIELDS}
        cum = cum_sc[...]
        extra = [next_chunk(xnext_ref, False)]
    else:
        d, cum, tots = _drain(prepare(lambda j, lo, hi: p_ref[j, :, lo:hi], True))
        extra = []
    if single_seq:
        tots = [cum[j * C + C - 1:j * C + C, :] for j in range(R)]
    lnw = lnw_ref[...]
    lnb = lnb_ref[...]

    def wkv_stream(j, hp):
        sl = pairs[hp]
        o, new = yield from _wkv_pair_chunk(d["r"][rows(j), sl], d["lw"][rows(j), sl], cum[rows(j), sl],
                                            jnp.exp(tots[j][:, sl]), d["kmod"][rows(j), sl], d["v"][rows(j), sl],
                                            d["kk"][rows(j), sl], d["a"][rows(j), sl], wkv_sc[j, hp], mk, Ls)
        for s in range(n_seq):
            wkv_sc[j, hp, blk(s), :] = new[s]
        return o

    def hgrn_stream(j, h):
        sl = slice(RWKV_WIDTH + h * LANES, RWKV_WIDTH + (h + 1) * LANES)
        hs = blk(h)
        o, new = yield from _hgrn_head_chunk(d["q"][rows(j), hs], d["kf"][rows(j), hs], cum[rows(j), sl],
                                             tots[j][:, sl], d["iv"][rows(j), hs], hg_sc[j, h], mk, Ls)
        for s in range(n_seq):
            hg_sc[j, h, blk(s), :] = new[s]
        on = _rms(o, gn_ref[...])
        y_ref[j, :, sl] = (on * _silu(d["og"][rows(j), hs])).astype(y_ref.dtype)

    outs = _lockstep([wkv_stream(j, hp) for j in range(R) for hp in range(NP)]
                     + [hgrn_stream(j, h) for j in range(R) for h in range(HGRN_HEADS)] + extra)

    os_ = [jnp.concatenate([outs[j * NP + hp] for j in range(R)], axis=0) for hp in range(NP)]
    means = _group_sums(os_, gsum)
    ds = [o - mu * (1.0 / RWKV_HEAD_DIM) for o, mu in zip(os_, means)]
    vars_ = _group_sums([dd * dd for dd in ds], gsum)
    for hp, sl in enumerate(pairs):
        on = ds[hp] * lax.rsqrt(vars_[hp] * (1.0 / RWKV_HEAD_DIM) + LNX_EPS) * lnw[:, sl] + lnb[:, sl]
        yv = ((on + d["bonus"][:, sl]) * d["g"][:, sl]).astype(y_ref.dtype)
        for j in range(R):
            y_ref[j, :, sl] = yv[rows(j)]

    if fuse_in:
        hand_over(*outs[-1])

    @pl.when(c == pl.num_programs(1) - 1)
    def _():
        for j in range(R):
            for s in range(n_seq):
                for hp in range(NP):
                    wkvout_ref[j * n_seq + s, 2 * hp] = wkv_sc[j, hp, s * LANES:s * LANES + HD, 0:HD]
                    wkvout_ref[j * n_seq + s, 2 * hp + 1] = wkv_sc[j, hp, s * LANES + HD:(s + 1) * LANES, HD:2 * HD]
                for h in range(HGRN_HEADS):
                    hgout_ref[j * n_seq + s, h] = hg_sc[j, h, blk(s), :].T


def _out_body(rep, x_ref, ym_ref, gt1_ref, sh2_ref, sc2_ref, gt2_ref, n2_ref, nf_ref,
              wo_ref, wu_ref, wd_ref, o_ref):
    gt1, sh2, sc2, gt2 = (_repeat_rows(m[...], rep) for m in (gt1_ref, sh2_ref, sc2_ref, gt2_ref))
    y = jnp.dot(ym_ref[...], wo_ref[...], preferred_element_type=F32)
    x1 = x_ref[...] + gt1 * y
    h = (_rms(x1, n2_ref[...]) * (1.0 + sc2) + sh2).astype(BF16)
    acc = jnp.zeros_like(x1)
    FC = FF_TILE
    for j in range(D_FF // FC):
        u = jnp.dot(h, wu_ref[:, j * FC:(j + 1) * FC], preferred_element_type=F32)
        u = jnp.square(jnp.maximum(u, 0.0)).astype(BF16)
        acc = acc + jnp.dot(u, wd_ref[j * FC:(j + 1) * FC, :], preferred_element_type=F32)
    x2 = x1 + gt2 * acc
    o_ref[...] = _rms(x2, nf_ref[...])


def _params(sem):
    return pltpu.CompilerParams(dimension_semantics=sem, vmem_limit_bytes=VMEM_LIMIT)


def _full(shape):
    return pl.BlockSpec(shape, lambda *_: (0,) * len(shape))


def _ada(c_all, w_ada, b_ada):
    n = c_all.shape[0]
    TN = 1024
    return pl.pallas_call(
        _ada_body,
        grid=(w_ada.shape[1] // TN,),
        in_specs=[pl.BlockSpec((n, D_MODEL), lambda j: (0, 0)),
                  pl.BlockSpec((D_MODEL, TN), lambda j: (0, j)),
                  pl.BlockSpec((1, TN), lambda j: (0, j))],
        out_specs=pl.BlockSpec((n, TN), lambda j: (0, j)),
        out_shape=jax.ShapeDtypeStruct((n, w_ada.shape[1]), F32),
        compiler_params=_params(("arbitrary",)),
        name="ada",
    )(c_all, w_ada, b_ada)


def _mod_specs(mod, cols, TM, rows_per_mod):
    if mod.ndim == 3:
        return [pl.BlockSpec((None, 1, D_MODEL), lambda i, c=c: (i * TM // rows_per_mod, 0, c)) for c in cols]
    return [pl.BlockSpec((TM // rows_per_mod, D_MODEL), lambda i, c=c: (i, c)) for c in cols]


def _mod_repeat(mod, rows_per_mod):
    return 1 if mod.ndim == 3 else rows_per_mod


def _inproj(x, mod, norm1, w_in, TM, rows_per_mod):
    M = x.shape[0]
    return pl.pallas_call(
        functools.partial(_inproj_body, _mod_repeat(mod, rows_per_mod)),
        grid=(M // TM,),
        in_specs=[pl.BlockSpec((TM, D_MODEL), lambda i: (i, 0))]
                 + _mod_specs(mod, (0, 1), TM, rows_per_mod)
                 + [_full((1, D_MODEL)), _full((D_MODEL, IN_WIDTH))],
        out_specs=pl.BlockSpec((TM, IN_WIDTH), lambda i: (i, 0)),
        out_shape=jax.ShapeDtypeStruct((M, IN_WIDTH), F32),
        compiler_params=_params(("arbitrary",)),
        name="inproj",
    )(x, mod, mod, norm1, w_in)


def _rec(p, n_groups, states, small, wcomb, wgate, Ls, n_batch, R, inproj=None):
    Gt, C = n_groups, CHUNK
    rows_total = (p if inproj is None else inproj[0]).shape[1]
    NC = rows_total // C
    n_seq = C // Ls
    keep = 8 if states is None else C
    wkv_spec = pl.BlockSpec((R * n_seq, RWKV_HEADS, RWKV_HEAD_DIM, RWKV_HEAD_DIM), lambda g, c: (g, 0, 0, 0))
    hg_spec = pl.BlockSpec((R * n_seq, HGRN_HEADS, LANES, LANES), lambda g, c: (g, 0, 0, 0))
    sh_spec = pl.BlockSpec((R, C, SHIFT_WIDTH), lambda g, c: (g, 0, 0))
    sho_spec = pl.BlockSpec((R, keep, SHIFT_WIDTH), lambda g, c: (g, 0, 0))
    st_in = [] if states is None else list(states)
    st_specs = [] if states is None else [sh_spec, wkv_spec, hg_spec]
    scratch = [pltpu.VMEM((R, RWKV_HEADS // 2, n_seq * LANES, LANES), F32),
               pltpu.VMEM((R, HGRN_HEADS, n_seq * LANES, LANES), F32)]
    if inproj is None:
        lead_in = [p]
        lead_specs = [pl.BlockSpec((R, C, IN_WIDTH), lambda g, c: (g, c, 0))]
    else:
        x, mod, norm1, w_in = inproj
        lead_in = [x, x, mod, mod, norm1, w_in]
        lead_specs = [pl.BlockSpec((R, C, D_MODEL), lambda g, c: (g, 0, 0)),
                      pl.BlockSpec((R, C, D_MODEL), lambda g, c: (g, jnp.minimum(c + 1, NC - 1), 0)),
                      pl.BlockSpec((R, 1, D_MODEL), lambda g, c: (g, 0, 0)),
                      pl.BlockSpec((R, 1, D_MODEL), lambda g, c: (g, 0, 1)),
                      _full(norm1.shape), _full(w_in.shape)]
        scratch += [pltpu.VMEM((R * C, RWKV_WIDTH), F32) for _ in PREP_FIELDS]
        scratch.append(pltpu.VMEM((R * C, 2 * RWKV_WIDTH), F32))
    return pl.pallas_call(
        functools.partial(_rec_body, Ls, states is None, R, inproj is not None),
        grid=(Gt // R, NC),
        in_specs=lead_specs + st_specs
                 + [_full(s.shape) for s in small] + [_full(wcomb.shape), _full(wgate.shape)],
        out_specs=[pl.BlockSpec((R, C, D_MODEL), lambda g, c: (g, c, 0)), sho_spec, wkv_spec, hg_spec],
        out_shape=[jax.ShapeDtypeStruct((Gt, NC * C, D_MODEL), BF16),
                   jax.ShapeDtypeStruct((Gt, keep, SHIFT_WIDTH), F32),
                   jax.ShapeDtypeStruct((n_batch, RWKV_HEADS, RWKV_HEAD_DIM, RWKV_HEAD_DIM), F32),
                   jax.ShapeDtypeStruct((n_batch, HGRN_HEADS, LANES, LANES), F32)],
        scratch_shapes=scratch,
        compiler_params=_params(("arbitrary", "arbitrary")),
        name="rec",
    )(*lead_in, *st_in, *small, wcomb, wgate)


def _out(x, ym, mod, norm2, norm_f, w_out, w_up, w_down, TM, rows_per_mod):
    M = x.shape[0]
    row = lambda i: (i, 0)
    return pl.pallas_call(
        functools.partial(_out_body, _mod_repeat(mod, rows_per_mod)),
        grid=(M // TM,),
        in_specs=[pl.BlockSpec((TM, D_MODEL), row), pl.BlockSpec((TM, D_MODEL), row)]
                 + _mod_specs(mod, (2, 3, 4, 5), TM, rows_per_mod)
                 + [_full((1, D_MODEL)), _full((1, D_MODEL)),
                    _full(w_out.shape), _full(w_up.shape), _full(w_down.shape)],
        out_specs=pl.BlockSpec((TM, D_MODEL), row),
        out_shape=jax.ShapeDtypeStruct((M, D_MODEL), F32),
        compiler_params=_params(("arbitrary",)),
        name="outmlp",
    )(x, ym, mod, mod, mod, mod, norm2, norm_f, w_out, w_up, w_down)


def kernel(x_prompt, x_sample, c_prompt, c_sample, state_shift, state_wkv, state_hgrn, norm1, norm2, norm_f, w_ada, b_ada, w_in, mu_shift, w0, w_decay_up, a0, w_aaa_up, w_gate_up, k_k, k_a, r_k, lnx_w, lnx_b, hgrn_lb, hgrn_gnorm, w_out, w_up, w_down):
    BP, TP, _ = x_prompt.shape
    BS, TS, _ = x_sample.shape
    l = 0
    row = lambda t: t.reshape(1, -1)

    mod = _ada(jnp.concatenate([c_prompt, c_sample], axis=0), w_ada[l], row(b_ada[l]))
    mod_p = mod[:BP].reshape(BP, 1, 6 * D_MODEL)
    mod_s = mod[BP:]

    w_in_b = w_in[l].astype(BF16)
    w_out_b = w_out[l].astype(BF16)
    w_up_b = w_up[l].astype(BF16)
    w_down_b = w_down[l].astype(BF16)
    zer = jnp.zeros((D_DECAY_LORA, RWKV_WIDTH), F32)
    wcomb = jnp.concatenate([jnp.concatenate([w_decay_up[l], zer], axis=1),
                             jnp.concatenate([zer, w_aaa_up[l]], axis=1)], axis=0).astype(BF16)
    wgate = w_gate_up[l].astype(BF16)
    small = [row(mu_shift[l]), row(w0[l]), row(a0[l]), row(k_k[l]), row(k_a[l]), row(r_k[l]),
             row(lnx_w[l]), row(lnx_b[l]), hgrn_lb, row(hgrn_gnorm[l])]
    n1, n2, nf = row(norm1[l]), row(norm2[l]), row(norm_f)

    def trunk(x2d, modx, rows_per_mod, Gt, states, Ls, n_batch, TM, R, fuse_in):
        if fuse_in:
            ym, shout, wkv_o, hg_o = _rec(None, Gt, states, small, wcomb, wgate, Ls, n_batch, R,
                                          inproj=(x2d.reshape(Gt, -1, D_MODEL), modx, n1, w_in_b))
        else:
            p = _inproj(x2d, modx, n1, w_in_b, TM, rows_per_mod)
            ym, shout, wkv_o, hg_o = _rec(p.reshape(Gt, -1, IN_WIDTH), Gt, states, small, wcomb, wgate,
                                          Ls, n_batch, R)
        y = _out(x2d, ym.reshape(-1, D_MODEL), modx, n2, nf, w_out_b, w_up_b, w_down_b, TM, rows_per_mod)
        return y, shout, wkv_o, hg_o

    n_seq = CHUNK // TS
    GS = BS // n_seq
    shinit_s = jnp.zeros((GS, n_seq, TS, SHIFT_WIDTH), F32).at[:, :, 0].set(
        state_shift[l].reshape(GS, n_seq, SHIFT_WIDTH)).reshape(GS, CHUNK, SHIFT_WIDTH)
    ys, shs, wkvs, hgs = trunk(x_sample.reshape(BS * TS, D_MODEL), mod_s, TS, GS,
                               (shinit_s, state_wkv[l], state_hgrn[l]), TS, BS, ROWS_SAMPLE, 1, False)
    y_sample = ys.reshape(BS, TS, D_MODEL)
    shift_s = shs.reshape(GS, n_seq, TS, SHIFT_WIDTH)[:, :, TS - 1].reshape(BS, SHIFT_WIDTH)[None]
    wkv_s = wkvs[None]
    hgrn_s = hgs[None]

    yp, shp, wkvp, hgp = trunk(x_prompt.reshape(BP * TP, D_MODEL), mod_p, TP, BP, None, CHUNK, BP, OUT_ROWS_PROMPT,
                               PROMPT_GROUPS_PER_STEP, True)
    y_prompt = yp.reshape(BP, TP, D_MODEL)
    shift_p = shp[:, -1][None]
    wkv_p = wkvp[None]
    hgrn_p = hgp[None]

    return (y_prompt, y_sample, shift_p, wkv_p, hgrn_p, shift_s, wkv_s, hgrn_s)
```

```python
import functools

import jax
import jax.numpy as jnp
from jax import lax
from jax.experimental import pallas as pl
from jax.experimental.pallas import tpu as pltpu

F32 = jnp.float32
BF16 = jnp.bfloat16

D_MODEL = 1024
RWKV_HEADS = 8
RWKV_WIDTH = 512
HGRN_HEADS = 4
HGRN_WIDTH = 512
RWKV_HEAD_DIM = 64
D_DECAY_LORA = 64
D_AAA_LORA = 64
D_GATE_LORA = 128
SHIFT_WIDTH = 1792
IN_WIDTH = 3840
D_FF = 4096
COL_K = RWKV_WIDTH
COL_V = 2 * RWKV_WIDTH
COL_LORA = 3 * RWKV_WIDTH
COL_GATE = COL_LORA + D_DECAY_LORA + D_AAA_LORA
COL_Q = SHIFT_WIDTH
COL_F = COL_Q + HGRN_WIDTH
COL_I = COL_F + HGRN_WIDTH
COL_OG = COL_I + HGRN_WIDTH
NORM_EPS = 1e-6
LNX_EPS = 64e-5

LANES = 128
HEAD_PAIR = 2 * RWKV_HEAD_DIM
CHUNK = 64
SUB = 16
PROMPT_GROUPS_PER_STEP = 8
INPROJ_TILE = 256
FF_TILE = 1024
OUT_ROWS_PROMPT = 512
ROWS_SAMPLE = 512
V7X_VMEM_BYTES = 64 * 1024 * 1024
VMEM_LIMIT = V7X_VMEM_BYTES - 2 * 1024 * 1024


def _dot(a, b):
    return jnp.dot(a.astype(BF16), b.astype(BF16), preferred_element_type=F32)


def _dot_nt(a, b):
    return lax.dot_general(a.astype(BF16), b.astype(BF16), (((1,), (1,)), ((), ())),
                           preferred_element_type=F32)


def _dot_2pass_l(m, x):
    mb = m.astype(BF16)
    hi = x.astype(BF16)
    lo = (x - hi.astype(F32)).astype(BF16)
    return jnp.dot(mb, hi, preferred_element_type=F32) + jnp.dot(mb, lo, preferred_element_type=F32)


def _repeat_rows(m, rep):
    if rep == 1:
        return m
    n = m.shape[0]
    sel = (_shr(_iota((n * rep, n), 0), rep) == _iota((n * rep, n), 1)).astype(BF16)
    hi = m.astype(BF16)
    r1 = m - hi.astype(F32)
    mid = r1.astype(BF16)
    lo = (r1 - mid.astype(F32)).astype(BF16)
    d = lambda part: jnp.dot(sel, part, preferred_element_type=F32)
    return d(hi) + d(mid) + d(lo)


def _group_sums(xs, gmat):
    n, rows = len(xs), xs[0].shape[0]
    P = _dot(jnp.concatenate(xs, axis=0), gmat)
    return [P[i * rows:(i + 1) * rows] for i in range(n)]


def _iota(shape, dim):
    return lax.broadcasted_iota(jnp.int32, shape, dim)


def _shr(x, n):
    return lax.shift_right_logical(x, jnp.int32(n.bit_length() - 1))


def _sigmoid(x):
    return 1.0 / (1.0 + jnp.exp(-x))


def _silu(x):
    return x * _sigmoid(x)


def _softplus(x):
    return jnp.maximum(x, 0.0) + jnp.log(1.0 + jnp.exp(-jnp.abs(x)))


def _rms(x, gain):
    return x * lax.rsqrt(jnp.mean(x * x, axis=-1, keepdims=True) + NORM_EPS) * gain


def _make_masks(C, Ls):
    row = _iota((C, C), 0)
    col = _iota((C, C), 1)
    seq = lambda t: _shr(t, Ls)
    blk = lambda t: _shr(t, SUB)
    same = seq(row) == seq(col)
    R = _iota((2 * C, 2 * C), 0)
    Q = _iota((2 * C, 2 * C), 1)
    tr = R & (C - 1)
    tq = Q & (C - 1)
    same2 = seq(tr) == seq(tq)
    lvl = []
    s = 1
    while s < Ls:
        lvl.append((_shr(R, 2 * s) == _shr(Q, 2 * s)) & ((R & (2 * s - 1)) >= s) & ((Q & (2 * s - 1)) < s))
        s *= 2
    lane2 = _iota((2 * C, HEAD_PAIR), 1)
    row2 = _iota((2 * C, HEAD_PAIR), 0)
    pr = _iota((HEAD_PAIR, HEAD_PAIR), 0)
    pc = _iota((HEAD_PAIR, HEAD_PAIR), 1)
    return dict(
        row=row, col=col, same=same,
        mi=same & (col <= row),
        diag_blocks=same & (col <= row) & (blk(row) == blk(col)),
        off_blocks=[same & (blk(col) == j) & (blk(row) > j) for j in range(C // SUB - 1)],
        eye2=(R == Q).astype(F32),
        lvl=lvl,
        ms_hi2=same2 & (tq < tr) & (Q >= C),
        mi22=same2 & (tq <= tr),
        lo_half=_iota((C, HEAD_PAIR), 1) < RWKV_HEAD_DIM,
        own=(lane2 < RWKV_HEAD_DIM) == (row2 < C),
        top_left=(Q < C) & (R < C), bottom_left=(Q < C) & (R >= C),
        seq_of_row2=seq(row2 & (C - 1)),
        seq_of_row=seq(_iota((C, LANES), 0)),
        head_diag=_shr(pr, RWKV_HEAD_DIM) == _shr(pc, RWKV_HEAD_DIM),
    )


def _tri_inverse(G, mk):
    lvl = mk['lvl']
    D = mk['eye2'] + jnp.where(lvl[0], G, 0.0)
    for m in lvl[1:]:
        LD = _dot(jnp.where(m, G, 0.0), D)
        yield
        D = D + _dot(D, LD)
        yield
    return D


def _lockstep(streams):
    streams = list(streams)
    out = [None] * len(streams)
    alive = list(range(len(streams)))
    while alive:
        for i in list(alive):
            try:
                next(streams[i])
            except StopIteration as stop:
                out[i] = stop.value
                alive.remove(i)
    return out


def _drain(gen):
    while True:
        try:
            next(gen)
        except StopIteration as stop:
            return stop.value


def _select_seq(full, n_seq, sid_rows):
    if n_seq == 1:
        return full
    acc = jnp.where(sid_rows == 0, full[:, 0:LANES], 0.0)
    for s in range(1, n_seq):
        acc = acc + jnp.where(sid_rows == s, full[:, s * LANES:(s + 1) * LANES], 0.0)
    return acc


def _expand_seq(x, n_seq, sid_rows):
    if n_seq == 1:
        return x
    return jnp.concatenate([jnp.where(sid_rows == s, x, 0.0) for s in range(n_seq)], axis=1)


def _wkv_pair_chunk(r, lw, cw, wl, k, v, kk, a, S, mk, Ls):
    C = r.shape[0]
    n_seq = C // Ls
    lo_half = mk['lo_half']
    stack = lambda t: jnp.where(mk['own'], jnp.concatenate([t, t], axis=0), 0.0)
    e_neg = jnp.exp(-cw)
    at = -kk * jnp.exp(cw - lw)
    bt = kk * a * e_neg
    kt = k * e_neg
    rt = r * jnp.exp(cw)
    Y = jnp.concatenate([bt, kt], axis=0)
    gar = _dot_nt(jnp.concatenate([stack(at), stack(rt)], axis=0), Y)
    ga = gar[:2 * C]
    G = jnp.where(mk['top_left'], ga, 0.0) + pltpu.roll(jnp.where(mk['bottom_left'], ga, 0.0), C, 1)
    gr = gar[2 * C:]
    sid2 = mk['seq_of_row2']
    XO = _select_seq(_dot_nt(jnp.concatenate([at, rt], axis=0), S), n_seq, sid2)
    yield
    Vz = jnp.concatenate([jnp.zeros_like(v), v], axis=0)
    xk = _dot(jnp.where(mk['ms_hi2'], ga, 0.0), Vz)
    D = yield from _tri_inverse(G, mk)
    X = XO[:C] + jnp.where(lo_half, xk[:C], xk[C:])
    DX = _dot(D, jnp.concatenate([X, X], axis=0))
    yield
    U = jnp.where(lo_half, DX[:C], DX[C:])
    Z = jnp.concatenate([U, v], axis=0)
    oz = _dot(jnp.where(mk['mi22'], gr, 0.0), Z)
    upd = _dot(Z.T, _expand_seq(Y, n_seq, sid2))
    yield
    o = XO[C:] + jnp.where(lo_half, oz[:C], oz[C:])
    bd = mk['head_diag']
    new = []
    for s in range(n_seq):
        Ss = S[s * LANES:(s + 1) * LANES, :]
        new.append(jnp.where(bd, (Ss + upd[:, s * LANES:(s + 1) * LANES]) * wl[s * Ls:s * Ls + 1, :], 0.0))
    return o, new


def _hgrn_head_chunk(q, kf, b, blast, iv, ST, mk, Ls):
    C = q.shape[0]
    assert Ls == C or Ls <= SUB
    n_seq = C // Ls
    sid = mk['seq_of_row']
    if Ls > SUB:
        rows_of = lambda off: jnp.concatenate(
            [jnp.broadcast_to(b[SUB * j + off:SUB * j + off + 1, :], (SUB, LANES)) for j in range(C // SUB)], axis=0)
        bmid = rows_of(SUB // 2 - 1)
        bend = rows_of(SUB - 1)
    else:
        bmid = 0.0
    o = _select_seq(_dot_nt(q * jnp.exp(b), ST), n_seq, sid)
    att = jnp.where(mk['diag_blocks'],
                    _dot_nt(q * jnp.exp(b - bmid), kf * jnp.exp(bmid - b)), 0.0)
    if Ls > SUB:
        ko = kf * jnp.exp(jnp.minimum(bend - b, 0.0))
        nb = C // SUB - 1
        qo = [q * jnp.exp(jnp.minimum(b - b[SUB * j + SUB - 1:SUB * j + SUB, :], 0.0)) for j in range(nb)]
        off = _dot_nt(jnp.concatenate(qo, axis=0), ko)
        for j in range(nb):
            att = att + jnp.where(mk['off_blocks'][j], off[j * C:(j + 1) * C], 0.0)
    ke = kf * jnp.exp(blast - b)
    upd = _dot(iv.T, _expand_seq(ke, n_seq, sid))
    yield
    o = o + _dot(att, iv)
    yield
    dec = jnp.exp(blast)
    new = []
    for s in range(n_seq):
        new.append(ST[s * LANES:(s + 1) * LANES, :] * dec[s * Ls:s * Ls + 1, :]
                   + upd[:, s * LANES:(s + 1) * LANES])
    return o, new


def _ada_body(c_ref, w_ref, b_ref, o_ref):
    o_ref[...] = _dot(_silu(c_ref[...]), w_ref[...]) + b_ref[...]


def _inproj_body(rep, x_ref, sh_ref, sc_ref, n1_ref, w_ref, p_ref):
    h = _rms(x_ref[...], n1_ref[...]) * (1.0 + _repeat_rows(sc_ref[...], rep)) + _repeat_rows(sh_ref[...], rep)
    p_ref[...] = jnp.dot(h.astype(BF16), w_ref[...], preferred_element_type=F32)


PREP_FIELDS = ("r", "lw", "kmod", "v", "kk", "a", "g", "bonus", "q", "kf", "iv", "og")


def _rec_body(Ls, single_seq, R, fuse_in, *refs):
    refs = list(refs)
    if fuse_in:
        xfirst_ref, xnext_ref, sh1_ref, sc1_ref, n1_ref, win_ref = refs[:6]
        del refs[:6]
        p_ref = None
    else:
        p_ref = refs.pop(0)
    if single_seq:
        shinit_ref = wkvin_ref = hgin_ref = None
    else:
        shinit_ref, wkvin_ref, hgin_ref = refs[:3]
        del refs[:3]
    (mu_ref, w0_ref, a0_ref, kk_ref, ka_ref, rk_ref, lnw_ref, lnb_ref, lb_ref, gn_ref, wcomb_ref, wgate_ref,
     y_ref, shout_ref, wkvout_ref, hgout_ref, wkv_sc, hg_sc) = refs[:18]
    if fuse_in:
        assert single_seq
        prep_sc = dict(zip(PREP_FIELDS, refs[18:18 + len(PREP_FIELDS)]))
        cum_sc = refs[18 + len(PREP_FIELDS)]
    C = CHUNK
    n_seq = C // Ls
    NP = RWKV_HEADS // 2
    HD = RWKV_HEAD_DIM
    c = pl.program_id(1)
    rows = lambda j: slice(j * C, (j + 1) * C)
    blk = lambda s: slice(s * LANES, (s + 1) * LANES)
    pairs = [slice(hp * HEAD_PAIR, (hp + 1) * HEAD_PAIR) for hp in range(NP)]
    mk = _make_masks(C, Ls)
    gsum = mk['head_diag'].astype(F32)

    def prepare(pcols, first):
        rowv = _iota((C, SHIFT_WIDTH), 0)
        xs = []
        for j in range(R):
            p_rw = pcols(j, 0, SHIFT_WIDTH)
            keep = shout_ref.shape[1]
            p_prev = jnp.where(rowv == 0, shout_ref[j, keep - 1:keep, :], pltpu.roll(p_rw, 1, 0))
            if not single_seq:
                p_prev = jnp.where((rowv & (Ls - 1)) == 0, shinit_ref[j], p_prev)
            elif first:
                p_prev = jnp.where(rowv == 0, 0.0, p_prev)
            shout_ref[j] = p_rw[C - keep:, :]
            xs.append(p_rw + (p_prev - p_rw) * mu_ref[...])
        x = jnp.concatenate(xs, axis=0)
        cat = lambda lo, hi: jnp.concatenate([pcols(j, lo, hi) for j in range(R)], axis=0)
        r = x[:, 0:COL_K]
        k = x[:, COL_K:COL_V]
        v = x[:, COL_V:COL_LORA]
        wa = x[:, COL_LORA:COL_GATE]
        gd = x[:, COL_GATE:SHIFT_WIDTH]
        lane = _iota((R * C, LANES), 1)
        da = _dot(jnp.where(lane < D_DECAY_LORA, jnp.tanh(wa), wa), wcomb_ref[...])
        g = _dot(_sigmoid(gd), wgate_ref[...])
        yield
        w_log = -_softplus(-(w0_ref[...] + da[:, :RWKV_WIDTH])) - 0.5
        lw = -jnp.exp(w_log)
        a = _sigmoid(a0_ref[...] + da[:, RWKV_WIDTH:])
        kkr = k * kk_ref[...]
        kmod = k * (1.0 + (a - 1.0) * ka_ref[...])
        rkr = r * kmod * rk_ref[...]
        lbp = lb_ref[...]
        m = jnp.maximum(lbp[0:1, :], lbp[1:2, :])
        e0 = jnp.exp(lbp[0:1, :] - m)
        e1 = jnp.exp(lbp[1:2, :] - m)
        lb = e0 / (e0 + e1)
        q = _silu(cat(COL_Q, COL_F))
        f = lb + (1.0 - lb) * _sigmoid(cat(COL_F, COL_I))
        logs = jnp.concatenate([lw, jnp.log(f)], axis=1)
        cums, tots = [], []
        for j in range(R):
            if single_seq:
                cums.append(_dot_2pass_l(mk['mi'].astype(F32), logs[rows(j)]))
            else:
                both = _dot_2pass_l(jnp.concatenate([mk['mi'], mk['same']], axis=0).astype(F32), logs[rows(j)])
                cums.append(both[:C])
                tots.append(both[C:])
        sums = _group_sums([kkr[:, sl] * kkr[:, sl] for sl in pairs] + [rkr[:, sl] for sl in pairs], gsum)
        yield
        kk = jnp.concatenate([kkr[:, sl] / jnp.maximum(jnp.sqrt(sums[hp]), 1e-12)
                              for hp, sl in enumerate(pairs)], axis=1)
        bonus = jnp.concatenate(sums[NP:], axis=1) * v
        d = dict(r=r, lw=lw, kmod=kmod, v=v, kk=kk, a=a, g=g, bonus=bonus, q=q, kf=1.0 - f,
                 iv=cat(COL_I, COL_OG), og=cat(COL_OG, IN_WIDTH))
        return d, jnp.concatenate(cums, axis=0), tots

    def next_chunk(x_ref, first):
        hs = [_rms(x_ref[j], n1_ref[...]) * (1.0 + sc1_ref[j]) + sh1_ref[j] for j in range(R)]
        hn = jnp.concatenate(hs, axis=0).astype(BF16)
        tiles = []
        for t in range(IN_WIDTH // INPROJ_TILE):
            tiles.append(jnp.dot(hn, win_ref[:, t * INPROJ_TILE:(t + 1) * INPROJ_TILE],
                                 preferred_element_type=F32))
            yield
        pcols = lambda j, lo, hi: jnp.concatenate(tiles[lo // INPROJ_TILE:hi // INPROJ_TILE], axis=1)[rows(j)]
        d, cum, _ = yield from prepare(pcols, first)
        return d, cum

    def hand_over(d, cum):
        for name in PREP_FIELDS:
            prep_sc[name][...] = d[name]
        cum_sc[...] = cum

    @pl.when(c == 0)
    def _():
        shout_ref[...] = jnp.zeros_like(shout_ref)
        wkv_sc[...] = jnp.zeros_like(wkv_sc)
        if single_seq:
            hg_sc[...] = jnp.zeros_like(hg_sc)
        else:
            for j in range(R):
                for s in range(n_seq):
                    for hp in range(NP):
                        wkv_sc[j, hp, s * LANES:s * LANES + HD, 0:HD] = wkvin_ref[j * n_seq + s, 2 * hp]
                        wkv_sc[j, hp, s * LANES + HD:(s + 1) * LANES, HD:2 * HD] = wkvin_ref[j * n_seq + s, 2 * hp + 1]
                    for h in range(HGRN_HEADS):
                        hg_sc[j, h, blk(s), :] = hgin_ref[j * n_seq + s, h].T
        if fuse_in:
            hand_over(*_drain(next_chunk(xfirst_ref, True)))

    if fuse_in:
        d = {name: prep_sc[name][...] for name in PREP_FIELDS}
        cum = cum_sc[...]
        extra = [next_chunk(xnext_ref, False)]
    else:
        d, cum, tots = _drain(prepare(lambda j, lo, hi: p_ref[j, :, lo:hi], True))
        extra = []
    if single_seq:
        tots = [cum[j * C + C - 1:j * C + C, :] for j in range(R)]
    lnw = lnw_ref[...]
    lnb = lnb_ref[...]

    def wkv_stream(j, hp):
        sl = pairs[hp]
        o, new = yield from _wkv_pair_chunk(d["r"][rows(j), sl], d["lw"][rows(j), sl], cum[rows(j), sl],
                                            jnp.exp(tots[j][:, sl]), d["kmod"][rows(j), sl], d["v"][rows(j), sl],
                                            d["kk"][rows(j), sl], d["a"][rows(j), sl], wkv_sc[j, hp], mk, Ls)
        for s in range(n_seq):
            wkv_sc[j, hp, blk(s), :] = new[s]
        return o

    def hgrn_stream(j, h):
        sl = slice(RWKV_WIDTH + h * LANES, RWKV_WIDTH + (h + 1) * LANES)
        hs = blk(h)
        o, new = yield from _hgrn_head_chunk(d["q"][rows(j), hs], d["kf"][rows(j), hs], cum[rows(j), sl],
                                             tots[j][:, sl], d["iv"][rows(j), hs], hg_sc[j, h], mk, Ls)
        for s in range(n_seq):
            hg_sc[j, h, blk(s), :] = new[s]
        on = _rms(o, gn_ref[...])
        y_ref[j, :, sl] = (on * _silu(d["og"][rows(j), hs])).astype(y_ref.dtype)

    outs = _lockstep([wkv_stream(j, hp) for j in range(R) for hp in range(NP)]
                     + [hgrn_stream(j, h) for j in range(R) for h in range(HGRN_HEADS)] + extra)

    os_ = [jnp.concatenate([outs[j * NP + hp] for j in range(R)], axis=0) for hp in range(NP)]
    means = _group_sums(os_, gsum)
    ds = [o - mu * (1.0 / RWKV_HEAD_DIM) for o, mu in zip(os_, means)]
    vars_ = _group_sums([dd * dd for dd in ds], gsum)
    for hp, sl in enumerate(pairs):
        on = ds[hp] * lax.rsqrt(vars_[hp] * (1.0 / RWKV_HEAD_DIM) + LNX_EPS) * lnw[:, sl] + lnb[:, sl]
        yv = ((on + d["bonus"][:, sl]) * d["g"][:, sl]).astype(y_ref.dtype)
        for j in range(R):
            y_ref[j, :, sl] = yv[rows(j)]

    if fuse_in:
        hand_over(*outs[-1])

    @pl.when(c == pl.num_programs(1) - 1)
    def _():
        for j in range(R):
            for s in range(n_seq):
                for hp in range(NP):
                    wkvout_ref[j * n_seq + s, 2 * hp] = wkv_sc[j, hp, s * LANES:s * LANES + HD, 0:HD]
                    wkvout_ref[j * n_seq + s, 2 * hp + 1] = wkv_sc[j, hp, s * LANES + HD:(s + 1) * LANES, HD:2 * HD]
                for h in range(HGRN_HEADS):
                    hgout_ref[j * n_seq + s, h] = hg_sc[j, h, blk(s), :].T


def _out_body(rep, x_ref, ym_ref, gt1_ref, sh2_ref, sc2_ref, gt2_ref, n2_ref, nf_ref,
              wo_ref, wu_ref, wd_ref, o_ref):
    gt1, sh2, sc2, gt2 = (_repeat_rows(m[...], rep) for m in (gt1_ref, sh2_ref, sc2_ref, gt2_ref))
    y = jnp.dot(ym_ref[...], wo_ref[...], preferred_element_type=F32)
    x1 = x_ref[...] + gt1 * y
    h = (_rms(x1, n2_ref[...]) * (1.0 + sc2) + sh2).astype(BF16)
    acc = jnp.zeros_like(x1)
    FC = FF_TILE
    for j in range(D_FF // FC):
        u = jnp.dot(h, wu_ref[:, j * FC:(j + 1) * FC], preferred_element_type=F32)
        u = jnp.square(jnp.maximum(u, 0.0)).astype(BF16)
        acc = acc + jnp.dot(u, wd_ref[j * FC:(j + 1) * FC, :], preferred_element_type=F32)
    x2 = x1 + gt2 * acc
    o_ref[...] = _rms(x2, nf_ref[...])


def _params(sem):
    return pltpu.CompilerParams(dimension_semantics=sem, vmem_limit_bytes=VMEM_LIMIT)


def _full(shape):
    return pl.BlockSpec(shape, lambda *_: (0,) * len(shape))


def _ada(c_all, w_ada, b_ada):
    n = c_all.shape[0]
    TN = 1024
    return pl.pallas_call(
        _ada_body,
        grid=(w_ada.shape[1] // TN,),
        in_specs=[pl.BlockSpec((n, D_MODEL), lambda j: (0, 0)),
                  pl.BlockSpec((D_MODEL, TN), lambda j: (0, j)),
                  pl.BlockSpec((1, TN), lambda j: (0, j))],
        out_specs=pl.BlockSpec((n, TN), lambda j: (0, j)),
        out_shape=jax.ShapeDtypeStruct((n, w_ada.shape[1]), F32),
        compiler_params=_params(("arbitrary",)),
        name="ada",
    )(c_all, w_ada, b_ada)


def _mod_specs(mod, cols, TM, rows_per_mod):
    if mod.ndim == 3:
        return [pl.BlockSpec((None, 1, D_MODEL), lambda i, c=c: (i * TM // rows_per_mod, 0, c)) for c in cols]
    return [pl.BlockSpec((TM // rows_per_mod, D_MODEL), lambda i, c=c: (i, c)) for c in cols]


def _mod_repeat(mod, rows_per_mod):
    return 1 if mod.ndim == 3 else rows_per_mod


def _inproj(x, mod, norm1, w_in, TM, rows_per_mod):
    M = x.shape[0]
    return pl.pallas_call(
        functools.partial(_inproj_body, _mod_repeat(mod, rows_per_mod)),
        grid=(M // TM,),
        in_specs=[pl.BlockSpec((TM, D_MODEL), lambda i: (i, 0))]
                 + _mod_specs(mod, (0, 1), TM, rows_per_mod)
                 + [_full((1, D_MODEL)), _full((D_MODEL, IN_WIDTH))],
        out_specs=pl.BlockSpec((TM, IN_WIDTH), lambda i: (i, 0)),
        out_shape=jax.ShapeDtypeStruct((M, IN_WIDTH), F32),
        compiler_params=_params(("arbitrary",)),
        name="inproj",
    )(x, mod, mod, norm1, w_in)


def _rec(p, n_groups, states, small, wcomb, wgate, Ls, n_batch, R, inproj=None):
    Gt, C = n_groups, CHUNK
    rows_total = (p if inproj is None else inproj[0]).shape[1]
    NC = rows_total // C
    n_seq = C // Ls
    keep = 8 if states is None else C
    wkv_spec = pl.BlockSpec((R * n_seq, RWKV_HEADS, RWKV_HEAD_DIM, RWKV_HEAD_DIM), lambda g, c: (g, 0, 0, 0))
    hg_spec = pl.BlockSpec((R * n_seq, HGRN_HEADS, LANES, LANES), lambda g, c: (g, 0, 0, 0))
    sh_spec = pl.BlockSpec((R, C, SHIFT_WIDTH), lambda g, c: (g, 0, 0))
    sho_spec = pl.BlockSpec((R, keep, SHIFT_WIDTH), lambda g, c: (g, 0, 0))
    st_in = [] if states is None else list(states)
    st_specs = [] if states is None else [sh_spec, wkv_spec, hg_spec]
    scratch = [pltpu.VMEM((R, RWKV_HEADS // 2, n_seq * LANES, LANES), F32),
               pltpu.VMEM((R, HGRN_HEADS, n_seq * LANES, LANES), F32)]
    if inproj is None:
        lead_in = [p]
        lead_specs = [pl.BlockSpec((R, C, IN_WIDTH), lambda g, c: (g, c, 0))]
    else:
        x, mod, norm1, w_in = inproj
        lead_in = [x, x, mod, mod, norm1, w_in]
        lead_specs = [pl.BlockSpec((R, C, D_MODEL), lambda g, c: (g, 0, 0)),
                      pl.BlockSpec((R, C, D_MODEL), lambda g, c: (g, jnp.minimum(c + 1, NC - 1), 0)),
                      pl.BlockSpec((R, 1, D_MODEL), lambda g, c: (g, 0, 0)),
                      pl.BlockSpec((R, 1, D_MODEL), lambda g, c: (g, 0, 1)),
                      _full(norm1.shape), _full(w_in.shape)]
        scratch += [pltpu.VMEM((R * C, RWKV_WIDTH), F32) for _ in PREP_FIELDS]
        scratch.append(pltpu.VMEM((R * C, 2 * RWKV_WIDTH), F32))
    return pl.pallas_call(
        functools.partial(_rec_body, Ls, states is None, R, inproj is not None),
        grid=(Gt // R, NC),
        in_specs=lead_specs + st_specs
                 + [_full(s.shape) for s in small] + [_full(wcomb.shape), _full(wgate.shape)],
        out_specs=[pl.BlockSpec((R, C, D_MODEL), lambda g, c: (g, c, 0)), sho_spec, wkv_spec, hg_spec],
        out_shape=[jax.ShapeDtypeStruct((Gt, NC * C, D_MODEL), BF16),
                   jax.ShapeDtypeStruct((Gt, keep, SHIFT_WIDTH), F32),
                   jax.ShapeDtypeStruct((n_batch, RWKV_HEADS, RWKV_HEAD_DIM, RWKV_HEAD_DIM), F32),
                   jax.ShapeDtypeStruct((n_batch, HGRN_HEADS, LANES, LANES), F32)],
        scratch_shapes=scratch,
        compiler_params=_params(("arbitrary", "arbitrary")),
        name="rec",
    )(*lead_in, *st_in, *small, wcomb, wgate)


def _out(x, ym, mod, norm2, norm_f, w_out, w_up, w_down, TM, rows_per_mod):
    M = x.shape[0]
    row = lambda i: (i, 0)
    return pl.pallas_call(
        functools.partial(_out_body, _mod_repeat(mod, rows_per_mod)),
        grid=(M // TM,),
        in_specs=[pl.BlockSpec((TM, D_MODEL), row), pl.BlockSpec((TM, D_MODEL), row)]
                 + _mod_specs(mod, (2, 3, 4, 5), TM, rows_per_mod)
                 + [_full((1, D_MODEL)), _full((1, D_MODEL)),
                    _full(w_out.shape), _full(w_up.shape), _full(w_down.shape)],
        out_specs=pl.BlockSpec((TM, D_MODEL), row),
        out_shape=jax.ShapeDtypeStruct((M, D_MODEL), F32),
        compiler_params=_params(("arbitrary",)),
        name="outmlp",
    )(x, ym, mod, mod, mod, mod, norm2, norm_f, w_out, w_up, w_down)


def kernel(x_prompt, x_sample, c_prompt, c_sample, state_shift, state_wkv, state_hgrn, norm1, norm2, norm_f, w_ada, b_ada, w_in, mu_shift, w0, w_decay_up, a0, w_aaa_up, w_gate_up, k_k, k_a, r_k, lnx_w, lnx_b, hgrn_lb, hgrn_gnorm, w_out, w_up, w_down):
    BP, TP, _ = x_prompt.shape
    BS, TS, _ = x_sample.shape
    l = 0
    row = lambda t: t.reshape(1, -1)

    mod = _ada(jnp.concatenate([c_prompt, c_sample], axis=0), w_ada[l], row(b_ada[l]))
    mod_p = mod[:BP].reshape(BP, 1, 6 * D_MODEL)
    mod_s = mod[BP:]

    w_in_b = w_in[l].astype(BF16)
    w_out_b = w_out[l].astype(BF16)
    w_up_b = w_up[l].astype(BF16)
    w_down_b = w_down[l].astype(BF16)
    zer = jnp.zeros((D_DECAY_LORA, RWKV_WIDTH), F32)
    wcomb = jnp.concatenate([jnp.concatenate([w_decay_up[l], zer], axis=1),
                             jnp.concatenate([zer, w_aaa_up[l]], axis=1)], axis=0).astype(BF16)
    wgate = w_gate_up[l].astype(BF16)
    small = [row(mu_shift[l]), row(w0[l]), row(a0[l]), row(k_k[l]), row(k_a[l]), row(r_k[l]),
             row(lnx_w[l]), row(lnx_b[l]), hgrn_lb, row(hgrn_gnorm[l])]
    n1, n2, nf = row(norm1[l]), row(norm2[l]), row(norm_f)

    def trunk(x2d, modx, rows_per_mod, Gt, states, Ls, n_batch, TM, R, fuse_in):
        if fuse_in:
            ym, shout, wkv_o, hg_o = _rec(None, Gt, states, small, wcomb, wgate, Ls, n_batch, R,
                                          inproj=(x2d.reshape(Gt, -1, D_MODEL), modx, n1, w_in_b))
        else:
            p = _inproj(x2d, modx, n1, w_in_b, TM, rows_per_mod)
            ym, shout, wkv_o, hg_o = _rec(p.reshape(Gt, -1, IN_WIDTH), Gt, states, small, wcomb, wgate,
                                          Ls, n_batch, R)
        y = _out(x2d, ym.reshape(-1, D_MODEL), modx, n2, nf, w_out_b, w_up_b, w_down_b, TM, rows_per_mod)
        return y, shout, wkv_o, hg_o

    yp, shp, wkvp, hgp = trunk(x_prompt.reshape(BP * TP, D_MODEL), mod_p, TP, BP, None, CHUNK, BP, OUT_ROWS_PROMPT,
                               PROMPT_GROUPS_PER_STEP, True)
    y_prompt = yp.reshape(BP, TP, D_MODEL)
    shift_p = shp[:, -1][None]
    wkv_p = wkvp[None]
    hgrn_p = hgp[None]

    n_seq = CHUNK // TS
    GS = BS // n_seq
    shinit_s = jnp.zeros((GS, n_seq, TS, SHIFT_WIDTH), F32).at[:, :, 0].set(
        state_shift[l].reshape(GS, n_seq, SHIFT_WIDTH)).reshape(GS, CHUNK, SHIFT_WIDTH)
    ys, shs, wkvs, hgs = trunk(x_sample.reshape(BS * TS, D_MODEL), mod_s, TS, GS,
                               (shinit_s, state_wkv[l], state_hgrn[l]), TS, BS, ROWS_SAMPLE, 1, False)
    y_sample = ys.reshape(BS, TS, D_MODEL)
    shift_s = shs.reshape(GS, n_seq, TS, SHIFT_WIDTH)[:, :, TS - 1].reshape(BS, SHIFT_WIDTH)[None]
    wkv_s = wkvs[None]
    hgrn_s = hgs[None]

    return (y_prompt, y_sample, shift_p, wkv_p, hgrn_p, shift_s, wkv_s, hgrn_s)
```

```python
import functools

import jax
import jax.numpy as jnp
from jax import lax
from jax.experimental import pallas as pl
from jax.experimental.pallas import tpu as pltpu

F32 = jnp.float32
BF16 = jnp.bfloat16

D_MODEL = 1024
RWKV_HEADS = 8
RWKV_WIDTH = 512
HGRN_HEADS = 4
HGRN_WIDTH = 512
RWKV_HEAD_DIM = 64
D_DECAY_LORA = 64
D_AAA_LORA = 64
D_GATE_LORA = 128
SHIFT_WIDTH = 1792
IN_WIDTH = 3840
D_FF = 4096
COL_K = RWKV_WIDTH
COL_V = 2 * RWKV_WIDTH
COL_LORA = 3 * RWKV_WIDTH
COL_GATE = COL_LORA + D_DECAY_LORA + D_AAA_LORA
COL_Q = SHIFT_WIDTH
COL_F = COL_Q + HGRN_WIDTH
COL_I = COL_F + HGRN_WIDTH
COL_OG = COL_I + HGRN_WIDTH
NORM_EPS = 1e-6
LNX_EPS = 64e-5

LANES = 128
HEAD_PAIR = 2 * RWKV_HEAD_DIM
CHUNK = 64
SUB = 16
PROMPT_GROUPS_PER_STEP = 8
INPROJ_TILE = 256
FF_TILE = 1024
OUT_ROWS_PROMPT = 512
ROWS_SAMPLE = 256
V7X_VMEM_BYTES = 64 * 1024 * 1024
VMEM_LIMIT = V7X_VMEM_BYTES - 2 * 1024 * 1024


def _dot(a, b):
    return jnp.dot(a.astype(BF16), b.astype(BF16), preferred_element_type=F32)


def _dot_nt(a, b):
    return lax.dot_general(a.astype(BF16), b.astype(BF16), (((1,), (1,)), ((), ())),
                           preferred_element_type=F32)


def _dot_2pass_l(m, x):
    mb = m.astype(BF16)
    hi = x.astype(BF16)
    lo = (x - hi.astype(F32)).astype(BF16)
    return jnp.dot(mb, hi, preferred_element_type=F32) + jnp.dot(mb, lo, preferred_element_type=F32)


def _repeat_rows(m, rep):
    if rep == 1:
        return m
    n = m.shape[0]
    sel = (_shr(_iota((n * rep, n), 0), rep) == _iota((n * rep, n), 1)).astype(BF16)
    hi = m.astype(BF16)
    r1 = m - hi.astype(F32)
    mid = r1.astype(BF16)
    lo = (r1 - mid.astype(F32)).astype(BF16)
    d = lambda part: jnp.dot(sel, part, preferred_element_type=F32)
    return d(hi) + d(mid) + d(lo)


def _last_rows(m, rep):
    if rep == 1:
        return m
    n = m.shape[0] // rep
    sel = (_iota((n, n * rep), 0) * rep + (rep - 1) == _iota((n, n * rep), 1)).astype(BF16)
    hi = m.astype(BF16)
    r1 = m - hi.astype(F32)
    mid = r1.astype(BF16)
    lo = (r1 - mid.astype(F32)).astype(BF16)
    d = lambda part: jnp.dot(sel, part, preferred_element_type=F32)
    return d(hi) + d(mid) + d(lo)


def _group_sums(xs, gmat):
    n, rows = len(xs), xs[0].shape[0]
    P = _dot(jnp.concatenate(xs, axis=0), gmat)
    return [P[i * rows:(i + 1) * rows] for i in range(n)]


def _iota(shape, dim):
    return lax.broadcasted_iota(jnp.int32, shape, dim)


def _shr(x, n):
    return lax.shift_right_logical(x, jnp.int32(n.bit_length() - 1))


def _sigmoid(x):
    return 1.0 / (1.0 + jnp.exp(-x))


def _silu(x):
    return x * _sigmoid(x)


def _softplus(x):
    return jnp.maximum(x, 0.0) + jnp.log(1.0 + jnp.exp(-jnp.abs(x)))


def _rms(x, gain):
    return x * lax.rsqrt(jnp.mean(x * x, axis=-1, keepdims=True) + NORM_EPS) * gain


def _make_masks(C, Ls):
    row = _iota((C, C), 0)
    col = _iota((C, C), 1)
    seq = lambda t: _shr(t, Ls)
    blk = lambda t: _shr(t, SUB)
    same = seq(row) == seq(col)
    R = _iota((2 * C, 2 * C), 0)
    Q = _iota((2 * C, 2 * C), 1)
    tr = R & (C - 1)
    tq = Q & (C - 1)
    same2 = seq(tr) == seq(tq)
    lvl = []
    s = 1
    while s < Ls:
        lvl.append((_shr(R, 2 * s) == _shr(Q, 2 * s)) & ((R & (2 * s - 1)) >= s) & ((Q & (2 * s - 1)) < s))
        s *= 2
    lane2 = _iota((2 * C, HEAD_PAIR), 1)
    row2 = _iota((2 * C, HEAD_PAIR), 0)
    pr = _iota((HEAD_PAIR, HEAD_PAIR), 0)
    pc = _iota((HEAD_PAIR, HEAD_PAIR), 1)
    return dict(
        row=row, col=col, same=same,
        mi=same & (col <= row),
        diag_blocks=same & (col <= row) & (blk(row) == blk(col)),
        off_blocks=[same & (blk(col) == j) & (blk(row) > j) for j in range(C // SUB - 1)],
        eye2=(R == Q).astype(F32),
        lvl=lvl,
        ms_hi2=same2 & (tq < tr) & (Q >= C),
        mi22=same2 & (tq <= tr),
        lo_half=_iota((C, HEAD_PAIR), 1) < RWKV_HEAD_DIM,
        own=(lane2 < RWKV_HEAD_DIM) == (row2 < C),
        top_left=(Q < C) & (R < C), bottom_left=(Q < C) & (R >= C),
        seq_of_row2=seq(row2 & (C - 1)),
        seq_of_row=seq(_iota((C, LANES), 0)),
        head_diag=_shr(pr, RWKV_HEAD_DIM) == _shr(pc, RWKV_HEAD_DIM),
    )


def _tri_inverse(G, mk):
    lvl = mk['lvl']
    D = mk['eye2'] + jnp.where(lvl[0], G, 0.0)
    for m in lvl[1:]:
        LD = _dot(jnp.where(m, G, 0.0), D)
        yield
        D = D + _dot(D, LD)
        yield
    return D


def _lockstep(streams):
    streams = list(streams)
    out = [None] * len(streams)
    alive = list(range(len(streams)))
    while alive:
        for i in list(alive):
            try:
                next(streams[i])
            except StopIteration as stop:
                out[i] = stop.value
                alive.remove(i)
    return out


def _drain(gen):
    while True:
        try:
            next(gen)
        except StopIteration as stop:
            return stop.value


def _select_seq(full, n_seq, sid_rows):
    if n_seq == 1:
        return full
    acc = jnp.where(sid_rows == 0, full[:, 0:LANES], 0.0)
    for s in range(1, n_seq):
        acc = acc + jnp.where(sid_rows == s, full[:, s * LANES:(s + 1) * LANES], 0.0)
    return acc


def _expand_seq(x, n_seq, sid_rows):
    if n_seq == 1:
        return x
    return jnp.concatenate([jnp.where(sid_rows == s, x, 0.0) for s in range(n_seq)], axis=1)


def _wkv_pair_chunk(r, lw, cw, wl, k, v, kk, a, S, mk, Ls):
    C = r.shape[0]
    n_seq = C // Ls
    lo_half = mk['lo_half']
    stack = lambda t: jnp.where(mk['own'], jnp.concatenate([t, t], axis=0), 0.0)
    e_neg = jnp.exp(-cw)
    at = -kk * jnp.exp(cw - lw)
    bt = kk * a * e_neg
    kt = k * e_neg
    rt = r * jnp.exp(cw)
    Y = jnp.concatenate([bt, kt], axis=0)
    gar = _dot_nt(jnp.concatenate([stack(at), stack(rt)], axis=0), Y)
    ga = gar[:2 * C]
    G = jnp.where(mk['top_left'], ga, 0.0) + pltpu.roll(jnp.where(mk['bottom_left'], ga, 0.0), C, 1)
    gr = gar[2 * C:]
    sid2 = mk['seq_of_row2']
    XO = _select_seq(_dot_nt(jnp.concatenate([at, rt], axis=0), S), n_seq, sid2)
    yield
    Vz = jnp.concatenate([jnp.zeros_like(v), v], axis=0)
    xk = _dot(jnp.where(mk['ms_hi2'], ga, 0.0), Vz)
    D = yield from _tri_inverse(G, mk)
    X = XO[:C] + jnp.where(lo_half, xk[:C], xk[C:])
    DX = _dot(D, jnp.concatenate([X, X], axis=0))
    yield
    U = jnp.where(lo_half, DX[:C], DX[C:])
    Z = jnp.concatenate([U, v], axis=0)
    oz = _dot(jnp.where(mk['mi22'], gr, 0.0), Z)
    upd = _dot(Z.T, _expand_seq(Y, n_seq, sid2))
    yield
    o = XO[C:] + jnp.where(lo_half, oz[:C], oz[C:])
    bd = mk['head_diag']
    new = []
    for s in range(n_seq):
        Ss = S[s * LANES:(s + 1) * LANES, :]
        new.append(jnp.where(bd, (Ss + upd[:, s * LANES:(s + 1) * LANES]) * wl[s * Ls:s * Ls + 1, :], 0.0))
    return o, new


def _hgrn_head_chunk(q, kf, b, blast, iv, ST, mk, Ls):
    C = q.shape[0]
    assert Ls == C or Ls <= SUB
    n_seq = C // Ls
    sid = mk['seq_of_row']
    if Ls > SUB:
        rows_of = lambda off: jnp.concatenate(
            [jnp.broadcast_to(b[SUB * j + off:SUB * j + off + 1, :], (SUB, LANES)) for j in range(C // SUB)], axis=0)
        bmid = rows_of(SUB // 2 - 1)
        bend = rows_of(SUB - 1)
    else:
        bmid = 0.0
    o = _select_seq(_dot_nt(q * jnp.exp(b), ST), n_seq, sid)
    att = jnp.where(mk['diag_blocks'],
                    _dot_nt(q * jnp.exp(b - bmid), kf * jnp.exp(bmid - b)), 0.0)
    if Ls > SUB:
        ko = kf * jnp.exp(jnp.minimum(bend - b, 0.0))
        nb = C // SUB - 1
        qo = [q * jnp.exp(jnp.minimum(b - b[SUB * j + SUB - 1:SUB * j + SUB, :], 0.0)) for j in range(nb)]
        off = _dot_nt(jnp.concatenate(qo, axis=0), ko)
        for j in range(nb):
            att = att + jnp.where(mk['off_blocks'][j], off[j * C:(j + 1) * C], 0.0)
    ke = kf * jnp.exp(blast - b)
    upd = _dot(iv.T, _expand_seq(ke, n_seq, sid))
    yield
    o = o + _dot(att, iv)
    yield
    dec = jnp.exp(blast)
    new = []
    for s in range(n_seq):
        new.append(ST[s * LANES:(s + 1) * LANES, :] * dec[s * Ls:s * Ls + 1, :]
                   + upd[:, s * LANES:(s + 1) * LANES])
    return o, new


def _ada_body(c_ref, w_ref, b_ref, o_ref):
    o_ref[...] = _dot(_silu(c_ref[...]), w_ref[...]) + b_ref[...]


def _inproj_body(rep, x_ref, sh_ref, sc_ref, n1_ref, w_ref, p_ref):
    h = _rms(x_ref[...], n1_ref[...]) * (1.0 + _repeat_rows(sc_ref[...], rep)) + _repeat_rows(sh_ref[...], rep)
    p_ref[...] = jnp.dot(h.astype(BF16), w_ref[...], preferred_element_type=F32)


PREP_FIELDS = ("r", "lw", "kmod", "v", "kk", "a", "g", "bonus", "q", "kf", "iv", "og")


def _rec_body(Ls, single_seq, R, fuse_in, *refs):
    refs = list(refs)
    if fuse_in:
        xfirst_ref, xnext_ref, sh1_ref, sc1_ref, n1_ref, win_ref = refs[:6]
        del refs[:6]
        p_ref = None
    else:
        p_ref = refs.pop(0)
    if single_seq:
        shinit_ref = wkvin_ref = hgin_ref = None
    else:
        shinit_ref, wkvin_ref, hgin_ref = refs[:3]
        del refs[:3]
    (mu_ref, w0_ref, a0_ref, kk_ref, ka_ref, rk_ref, lnw_ref, lnb_ref, lb_ref, gn_ref, wcomb_ref, wgate_ref,
     y_ref, shout_ref, wkvout_ref, hgout_ref, wkv_sc, hg_sc) = refs[:18]
    if fuse_in:
        assert single_seq
        prep_sc = dict(zip(PREP_FIELDS, refs[18:18 + len(PREP_FIELDS)]))
        cum_sc = refs[18 + len(PREP_FIELDS)]
    C = CHUNK
    n_seq = C // Ls
    NP = RWKV_HEADS // 2
    HD = RWKV_HEAD_DIM
    c = pl.program_id(1)
    rows = lambda j: slice(j * C, (j + 1) * C)
    blk = lambda s: slice(s * LANES, (s + 1) * LANES)
    pairs = [slice(hp * HEAD_PAIR, (hp + 1) * HEAD_PAIR) for hp in range(NP)]
    mk = _make_masks(C, Ls)
    gsum = mk['head_diag'].astype(F32)

    def prepare(pcols, first):
        rowv = _iota((C, SHIFT_WIDTH), 0)
        xs = []
        for j in range(R):
            p_rw = pcols(j, 0, SHIFT_WIDTH)
            if not single_seq:
                p_prev = jnp.where((rowv & (Ls - 1)) == 0, _repeat_rows(shinit_ref[j], Ls),
                                   pltpu.roll(p_rw, 1, 0))
                shout_ref[j] = _last_rows(p_rw, Ls)
            else:
                keep = shout_ref.shape[1]
                p_prev = jnp.where(rowv == 0, shout_ref[j, keep - 1:keep, :], pltpu.roll(p_rw, 1, 0))
                if first:
                    p_prev = jnp.where(rowv == 0, 0.0, p_prev)
                shout_ref[j] = p_rw[C - keep:, :]
            xs.append(p_rw + (p_prev - p_rw) * mu_ref[...])
        x = jnp.concatenate(xs, axis=0)
        cat = lambda lo, hi: jnp.concatenate([pcols(j, lo, hi) for j in range(R)], axis=0)
        r = x[:, 0:COL_K]
        k = x[:, COL_K:COL_V]
        v = x[:, COL_V:COL_LORA]
        wa = x[:, COL_LORA:COL_GATE]
        gd = x[:, COL_GATE:SHIFT_WIDTH]
        lane = _iota((R * C, LANES), 1)
        da = _dot(jnp.where(lane < D_DECAY_LORA, jnp.tanh(wa), wa), wcomb_ref[...])
        g = _dot(_sigmoid(gd), wgate_ref[...])
        yield
        w_log = -_softplus(-(w0_ref[...] + da[:, :RWKV_WIDTH])) - 0.5
        lw = -jnp.exp(w_log)
        a = _sigmoid(a0_ref[...] + da[:, RWKV_WIDTH:])
        kkr = k * kk_ref[...]
        kmod = k * (1.0 + (a - 1.0) * ka_ref[...])
        rkr = r * kmod * rk_ref[...]
        lbp = lb_ref[...]
        m = jnp.maximum(lbp[0:1, :], lbp[1:2, :])
        e0 = jnp.exp(lbp[0:1, :] - m)
        e1 = jnp.exp(lbp[1:2, :] - m)
        lb = e0 / (e0 + e1)
        q = _silu(cat(COL_Q, COL_F))
        f = lb + (1.0 - lb) * _sigmoid(cat(COL_F, COL_I))
        logs = jnp.concatenate([lw, jnp.log(f)], axis=1)
        cums, tots = [], []
        for j in range(R):
            if single_seq:
                cums.append(_dot_2pass_l(mk['mi'].astype(F32), logs[rows(j)]))
            else:
                both = _dot_2pass_l(jnp.concatenate([mk['mi'], mk['same']], axis=0).astype(F32), logs[rows(j)])
                cums.append(both[:C])
                tots.append(both[C:])
        sums = _group_sums([kkr[:, sl] * kkr[:, sl] for sl in pairs] + [rkr[:, sl] for sl in pairs], gsum)
        yield
        kk = jnp.concatenate([kkr[:, sl] / jnp.maximum(jnp.sqrt(sums[hp]), 1e-12)
                              for hp, sl in enumerate(pairs)], axis=1)
        bonus = jnp.concatenate(sums[NP:], axis=1) * v
        d = dict(r=r, lw=lw, kmod=kmod, v=v, kk=kk, a=a, g=g, bonus=bonus, q=q, kf=1.0 - f,
                 iv=cat(COL_I, COL_OG), og=cat(COL_OG, IN_WIDTH))
        return d, jnp.concatenate(cums, axis=0), tots

    def next_chunk(x_ref, first):
        hs = [_rms(x_ref[j], n1_ref[...]) * (1.0 + sc1_ref[j]) + sh1_ref[j] for j in range(R)]
        hn = jnp.concatenate(hs, axis=0).astype(BF16)
        tiles = []
        for t in range(IN_WIDTH // INPROJ_TILE):
            tiles.append(jnp.dot(hn, win_ref[:, t * INPROJ_TILE:(t + 1) * INPROJ_TILE],
                                 preferred_element_type=F32))
            yield
        pcols = lambda j, lo, hi: jnp.concatenate(tiles[lo // INPROJ_TILE:hi // INPROJ_TILE], axis=1)[rows(j)]
        d, cum, _ = yield from prepare(pcols, first)
        return d, cum

    def hand_over(d, cum):
        for name in PREP_FIELDS:
            prep_sc[name][...] = d[name]
        cum_sc[...] = cum

    @pl.when(c == 0)
    def _():
        shout_ref[...] = jnp.zeros_like(shout_ref)
        wkv_sc[...] = jnp.zeros_like(wkv_sc)
        if single_seq:
            hg_sc[...] = jnp.zeros_like(hg_sc)
        else:
            for j in range(R):
                for s in range(n_seq):
                    for hp in range(NP):
                        wkv_sc[j, hp, s * LANES:s * LANES + HD, 0:HD] = wkvin_ref[j * n_seq + s, 2 * hp]
                        wkv_sc[j, hp, s * LANES + HD:(s + 1) * LANES, HD:2 * HD] = wkvin_ref[j * n_seq + s, 2 * hp + 1]
                    for h in range(HGRN_HEADS):
                        hg_sc[j, h, blk(s), :] = hgin_ref[j * n_seq + s, h].T
        if fuse_in:
            hand_over(*_drain(next_chunk(xfirst_ref, True)))

    if fuse_in:
        d = {name: prep_sc[name][...] for name in PREP_FIELDS}
        cum = cum_sc[...]
        extra = [next_chunk(xnext_ref, False)]
    else:
        d, cum, tots = _drain(prepare(lambda j, lo, hi: p_ref[j, :, lo:hi], True))
        extra = []
    if single_seq:
        tots = [cum[j * C + C - 1:j * C + C, :] for j in range(R)]
    lnw = lnw_ref[...]
    lnb = lnb_ref[...]

    def wkv_stream(j, hp):
        sl = pairs[hp]
        o, new = yield from _wkv_pair_chunk(d["r"][rows(j), sl], d["lw"][rows(j), sl], cum[rows(j), sl],
                                            jnp.exp(tots[j][:, sl]), d["kmod"][rows(j), sl], d["v"][rows(j), sl],
                                            d["kk"][rows(j), sl], d["a"][rows(j), sl], wkv_sc[j, hp], mk, Ls)
        for s in range(n_seq):
            wkv_sc[j, hp, blk(s), :] = new[s]
        return o

    def hgrn_stream(j, h):
        sl = slice(RWKV_WIDTH + h * LANES, RWKV_WIDTH + (h + 1) * LANES)
        hs = blk(h)
        o, new = yield from _hgrn_head_chunk(d["q"][rows(j), hs], d["kf"][rows(j), hs], cum[rows(j), sl],
                                             tots[j][:, sl], d["iv"][rows(j), hs], hg_sc[j, h], mk, Ls)
        for s in range(n_seq):
            hg_sc[j, h, blk(s), :] = new[s]
        on = _rms(o, gn_ref[...])
        y_ref[j, :, sl] = (on * _silu(d["og"][rows(j), hs])).astype(y_ref.dtype)

    outs = _lockstep([wkv_stream(j, hp) for j in range(R) for hp in range(NP)]
                     + [hgrn_stream(j, h) for j in range(R) for h in range(HGRN_HEADS)] + extra)

    os_ = [jnp.concatenate([outs[j * NP + hp] for j in range(R)], axis=0) for hp in range(NP)]
    means = _group_sums(os_, gsum)
    ds = [o - mu * (1.0 / RWKV_HEAD_DIM) for o, mu in zip(os_, means)]
    vars_ = _group_sums([dd * dd for dd in ds], gsum)
    for hp, sl in enumerate(pairs):
        on = ds[hp] * lax.rsqrt(vars_[hp] * (1.0 / RWKV_HEAD_DIM) + LNX_EPS) * lnw[:, sl] + lnb[:, sl]
        yv = ((on + d["bonus"][:, sl]) * d["g"][:, sl]).astype(y_ref.dtype)
        for j in range(R):
            y_ref[j, :, sl] = yv[rows(j)]

    if fuse_in:
        hand_over(*outs[-1])

    @pl.when(c == pl.num_programs(1) - 1)
    def _():
        for j in range(R):
            for s in range(n_seq):
                for hp in range(NP):
                    wkvout_ref[j * n_seq + s, 2 * hp] = wkv_sc[j, hp, s * LANES:s * LANES + HD, 0:HD]
                    wkvout_ref[j * n_seq + s, 2 * hp + 1] = wkv_sc[j, hp, s * LANES + HD:(s + 1) * LANES, HD:2 * HD]
                for h in range(HGRN_HEADS):
                    hgout_ref[j * n_seq + s, h] = hg_sc[j, h, blk(s), :].T


def _out_body(rep, x_ref, ym_ref, gt1_ref, sh2_ref, sc2_ref, gt2_ref, n2_ref, nf_ref,
              wo_ref, wu_ref, wd_ref, o_ref):
    gt1, sh2, sc2, gt2 = (_repeat_rows(m[...], rep) for m in (gt1_ref, sh2_ref, sc2_ref, gt2_ref))
    y = jnp.dot(ym_ref[...], wo_ref[...], preferred_element_type=F32)
    x1 = x_ref[...] + gt1 * y
    h = (_rms(x1, n2_ref[...]) * (1.0 + sc2) + sh2).astype(BF16)
    acc = jnp.zeros_like(x1)
    FC = FF_TILE
    for j in range(D_FF // FC):
        u = jnp.dot(h, wu_ref[:, j * FC:(j + 1) * FC], preferred_element_type=F32)
        u = jnp.square(jnp.maximum(u, 0.0)).astype(BF16)
        acc = acc + jnp.dot(u, wd_ref[j * FC:(j + 1) * FC, :], preferred_element_type=F32)
    x2 = x1 + gt2 * acc
    o_ref[...] = _rms(x2, nf_ref[...])


def _params(sem):
    return pltpu.CompilerParams(dimension_semantics=sem, vmem_limit_bytes=VMEM_LIMIT)


def _full(shape):
    return pl.BlockSpec(shape, lambda *_: (0,) * len(shape))


def _ada(c_all, w_ada, b_ada):
    n = c_all.shape[0]
    TN = 1024
    return pl.pallas_call(
        _ada_body,
        grid=(w_ada.shape[1] // TN,),
        in_specs=[pl.BlockSpec((n, D_MODEL), lambda j: (0, 0)),
                  pl.BlockSpec((D_MODEL, TN), lambda j: (0, j)),
                  pl.BlockSpec((1, TN), lambda j: (0, j))],
        out_specs=pl.BlockSpec((n, TN), lambda j: (0, j)),
        out_shape=jax.ShapeDtypeStruct((n, w_ada.shape[1]), F32),
        compiler_params=_params(("arbitrary",)),
        name="ada",
    )(c_all, w_ada, b_ada)


def _mod_specs(mod, cols, TM, rows_per_mod):
    if mod.ndim == 3:
        return [pl.BlockSpec((None, 1, D_MODEL), lambda i, c=c: (i * TM // rows_per_mod, 0, c)) for c in cols]
    return [pl.BlockSpec((TM // rows_per_mod, D_MODEL), lambda i, c=c: (i, c)) for c in cols]


def _mod_repeat(mod, rows_per_mod):
    return 1 if mod.ndim == 3 else rows_per_mod


def _inproj(x, mod, norm1, w_in, TM, rows_per_mod):
    M = x.shape[0]
    return pl.pallas_call(
        functools.partial(_inproj_body, _mod_repeat(mod, rows_per_mod)),
        grid=(M // TM,),
        in_specs=[pl.BlockSpec((TM, D_MODEL), lambda i: (i, 0))]
                 + _mod_specs(mod, (0, 1), TM, rows_per_mod)
                 + [_full((1, D_MODEL)), _full((D_MODEL, IN_WIDTH))],
        out_specs=pl.BlockSpec((TM, IN_WIDTH), lambda i: (i, 0)),
        out_shape=jax.ShapeDtypeStruct((M, IN_WIDTH), F32),
        compiler_params=_params(("arbitrary",)),
        name="inproj",
    )(x, mod, mod, norm1, w_in)


def _rec(p, n_groups, states, small, wcomb, wgate, Ls, n_batch, R, inproj=None):
    Gt, C = n_groups, CHUNK
    rows_total = (p if inproj is None else inproj[0]).shape[1]
    NC = rows_total // C
    n_seq = C // Ls
    keep = 8 if states is None else n_seq
    wkv_spec = pl.BlockSpec((R * n_seq, RWKV_HEADS, RWKV_HEAD_DIM, RWKV_HEAD_DIM), lambda g, c: (g, 0, 0, 0))
    hg_spec = pl.BlockSpec((R * n_seq, HGRN_HEADS, LANES, LANES), lambda g, c: (g, 0, 0, 0))
    sh_spec = pl.BlockSpec((R, n_seq, SHIFT_WIDTH), lambda g, c: (g, 0, 0))
    sho_spec = pl.BlockSpec((R, keep, SHIFT_WIDTH), lambda g, c: (g, 0, 0))
    st_in = [] if states is None else list(states)
    st_specs = [] if states is None else [sh_spec, wkv_spec, hg_spec]
    scratch = [pltpu.VMEM((R, RWKV_HEADS // 2, n_seq * LANES, LANES), F32),
               pltpu.VMEM((R, HGRN_HEADS, n_seq * LANES, LANES), F32)]
    if inproj is None:
        lead_in = [p]
        lead_specs = [pl.BlockSpec((R, C, IN_WIDTH), lambda g, c: (g, c, 0))]
    else:
        x, mod, norm1, w_in = inproj
        lead_in = [x, x, mod, mod, norm1, w_in]
        lead_specs = [pl.BlockSpec((R, C, D_MODEL), lambda g, c: (g, 0, 0)),
                      pl.BlockSpec((R, C, D_MODEL), lambda g, c: (g, jnp.minimum(c + 1, NC - 1), 0)),
                      pl.BlockSpec((R, 1, D_MODEL), lambda g, c: (g, 0, 0)),
                      pl.BlockSpec((R, 1, D_MODEL), lambda g, c: (g, 0, 1)),
                      _full(norm1.shape), _full(w_in.shape)]
        scratch += [pltpu.VMEM((R * C, RWKV_WIDTH), F32) for _ in PREP_FIELDS]
        scratch.append(pltpu.VMEM((R * C, 2 * RWKV_WIDTH), F32))
    return pl.pallas_call(
        functools.partial(_rec_body, Ls, states is None, R, inproj is not None),
        grid=(Gt // R, NC),
        in_specs=lead_specs + st_specs
                 + [_full(s.shape) for s in small] + [_full(wcomb.shape), _full(wgate.shape)],
        out_specs=[pl.BlockSpec((R, C, D_MODEL), lambda g, c: (g, c, 0)), sho_spec, wkv_spec, hg_spec],
        out_shape=[jax.ShapeDtypeStruct((Gt, NC * C, D_MODEL), BF16),
                   jax.ShapeDtypeStruct((Gt, keep, SHIFT_WIDTH), F32),
                   jax.ShapeDtypeStruct((n_batch, RWKV_HEADS, RWKV_HEAD_DIM, RWKV_HEAD_DIM), F32),
                   jax.ShapeDtypeStruct((n_batch, HGRN_HEADS, LANES, LANES), F32)],
        scratch_shapes=scratch,
        compiler_params=_params(("arbitrary", "arbitrary")),
        name="rec",
    )(*lead_in, *st_in, *small, wcomb, wgate)


def _out(x, ym, mod, norm2, norm_f, w_out, w_up, w_down, TM, rows_per_mod):
    M = x.shape[0]
    row = lambda i: (i, 0)
    return pl.pallas_call(
        functools.partial(_out_body, _mod_repeat(mod, rows_per_mod)),
        grid=(M // TM,),
        in_specs=[pl.BlockSpec((TM, D_MODEL), row), pl.BlockSpec((TM, D_MODEL), row)]
                 + _mod_specs(mod, (2, 3, 4, 5), TM, rows_per_mod)
                 + [_full((1, D_MODEL)), _full((1, D_MODEL)),
                    _full(w_out.shape), _full(w_up.shape), _full(w_down.shape)],
        out_specs=pl.BlockSpec((TM, D_MODEL), row),
        out_shape=jax.ShapeDtypeStruct((M, D_MODEL), F32),
        compiler_params=_params(("arbitrary",)),
        name="outmlp",
    )(x, ym, mod, mod, mod, mod, norm2, norm_f, w_out, w_up, w_down)


def kernel(x_prompt, x_sample, c_prompt, c_sample, state_shift, state_wkv, state_hgrn, norm1, norm2, norm_f, w_ada, b_ada, w_in, mu_shift, w0, w_decay_up, a0, w_aaa_up, w_gate_up, k_k, k_a, r_k, lnx_w, lnx_b, hgrn_lb, hgrn_gnorm, w_out, w_up, w_down):
    BP, TP, _ = x_prompt.shape
    BS, TS, _ = x_sample.shape
    l = 0
    row = lambda t: t.reshape(1, -1)

    mod = _ada(jnp.concatenate([c_prompt, c_sample], axis=0), w_ada[l], row(b_ada[l]))
    mod_p = mod[:BP].reshape(BP, 1, 6 * D_MODEL)
    mod_s = mod[BP:]

    w_in_b = w_in[l].astype(BF16)
    w_out_b = w_out[l].astype(BF16)
    w_up_b = w_up[l].astype(BF16)
    w_down_b = w_down[l].astype(BF16)
    zer = jnp.zeros((D_DECAY_LORA, RWKV_WIDTH), F32)
    wcomb = jnp.concatenate([jnp.concatenate([w_decay_up[l], zer], axis=1),
                             jnp.concatenate([zer, w_aaa_up[l]], axis=1)], axis=0).astype(BF16)
    wgate = w_gate_up[l].astype(BF16)
    small = [row(mu_shift[l]), row(w0[l]), row(a0[l]), row(k_k[l]), row(k_a[l]), row(r_k[l]),
             row(lnx_w[l]), row(lnx_b[l]), hgrn_lb, row(hgrn_gnorm[l])]
    n1, n2, nf = row(norm1[l]), row(norm2[l]), row(norm_f)

    def trunk(x2d, modx, rows_per_mod, Gt, states, Ls, n_batch, TM, R, fuse_in):
        if fuse_in:
            ym, shout, wkv_o, hg_o = _rec(None, Gt, states, small, wcomb, wgate, Ls, n_batch, R,
                                          inproj=(x2d.reshape(Gt, -1, D_MODEL), modx, n1, w_in_b))
        else:
            p = _inproj(x2d, modx, n1, w_in_b, TM, rows_per_mod)
            ym, shout, wkv_o, hg_o = _rec(p.reshape(Gt, -1, IN_WIDTH), Gt, states, small, wcomb, wgate,
                                          Ls, n_batch, R)
        y = _out(x2d, ym.reshape(-1, D_MODEL), modx, n2, nf, w_out_b, w_up_b, w_down_b, TM, rows_per_mod)
        return y, shout, wkv_o, hg_o

    yp, shp, wkvp, hgp = trunk(x_prompt.reshape(BP * TP, D_MODEL), mod_p, TP, BP, None, CHUNK, BP, OUT_ROWS_PROMPT,
                               PROMPT_GROUPS_PER_STEP, True)
    y_prompt = yp.reshape(BP, TP, D_MODEL)
    shift_p = shp[:, -1][None]
    wkv_p = wkvp[None]
    hgrn_p = hgp[None]

    n_seq = CHUNK // TS
    GS = BS // n_seq
    shinit_s = state_shift[l].reshape(GS, n_seq, SHIFT_WIDTH)
    ys, shs, wkvs, hgs = trunk(x_sample.reshape(BS * TS, D_MODEL), mod_s, TS, GS,
                               (shinit_s, state_wkv[l], state_hgrn[l]), TS, BS, ROWS_SAMPLE, 1, False)
    y_sample = ys.reshape(BS, TS, D_MODEL)
    shift_s = shs.reshape(BS, SHIFT_WIDTH)[None]
    wkv_s = wkvs[None]
    hgrn_s = hgs[None]

    return (y_prompt, y_sample, shift_p, wkv_p, hgrn_p, shift_s, wkv_s, hgrn_s)
```

```python
import functools

import jax
import jax.numpy as jnp
from jax import lax
from jax.experimental import pallas as pl
from jax.experimental.pallas import tpu as pltpu

F32 = jnp.float32
BF16 = jnp.bfloat16

D_MODEL = 1024
RWKV_HEADS = 8
RWKV_WIDTH = 512
HGRN_HEADS = 4
HGRN_WIDTH = 512
RWKV_HEAD_DIM = 64
D_DECAY_LORA = 64
D_AAA_LORA = 64
D_GATE_LORA = 128
SHIFT_WIDTH = 1792
IN_WIDTH = 3840
D_FF = 4096
COL_K = RWKV_WIDTH
COL_V = 2 * RWKV_WIDTH
COL_LORA = 3 * RWKV_WIDTH
COL_GATE = COL_LORA + D_DECAY_LORA + D_AAA_LORA
COL_Q = SHIFT_WIDTH
COL_F = COL_Q + HGRN_WIDTH
COL_I = COL_F + HGRN_WIDTH
COL_OG = COL_I + HGRN_WIDTH
NORM_EPS = 1e-6
LNX_EPS = 64e-5

LANES = 128
HEAD_PAIR = 2 * RWKV_HEAD_DIM
CHUNK = 64
SUB = 16
PROMPT_GROUPS_PER_STEP = 8
INPROJ_TILE = 256
FF_TILE = 1024
OUT_ROWS_PROMPT = 512
ROWS_SAMPLE = 256
V7X_VMEM_BYTES = 64 * 1024 * 1024
VMEM_LIMIT = V7X_VMEM_BYTES - 2 * 1024 * 1024


def _dot(a, b):
    return jnp.dot(a.astype(BF16), b.astype(BF16), preferred_element_type=F32)


def _dot_nt(a, b):
    return lax.dot_general(a.astype(BF16), b.astype(BF16), (((1,), (1,)), ((), ())),
                           preferred_element_type=F32)


def _dot_2pass_l(m, x):
    mb = m.astype(BF16)
    hi = x.astype(BF16)
    lo = (x - hi.astype(F32)).astype(BF16)
    return jnp.dot(mb, hi, preferred_element_type=F32) + jnp.dot(mb, lo, preferred_element_type=F32)


def _repeat_rows(m, rep):
    if rep == 1:
        return m
    n = m.shape[0]
    sel = (_shr(_iota((n * rep, n), 0), rep) == _iota((n * rep, n), 1)).astype(BF16)
    hi = m.astype(BF16)
    r1 = m - hi.astype(F32)
    mid = r1.astype(BF16)
    lo = (r1 - mid.astype(F32)).astype(BF16)
    d = lambda part: jnp.dot(sel, part, preferred_element_type=F32)
    return d(hi) + d(mid) + d(lo)


def _last_rows(m, rep):
    if rep == 1:
        return m
    n = m.shape[0] // rep
    sel = (_iota((n, n * rep), 0) * rep + (rep - 1) == _iota((n, n * rep), 1)).astype(BF16)
    hi = m.astype(BF16)
    r1 = m - hi.astype(F32)
    mid = r1.astype(BF16)
    lo = (r1 - mid.astype(F32)).astype(BF16)
    d = lambda part: jnp.dot(sel, part, preferred_element_type=F32)
    return d(hi) + d(mid) + d(lo)


def _group_sums(xs, gmat):
    n, rows = len(xs), xs[0].shape[0]
    P = _dot(jnp.concatenate(xs, axis=0), gmat)
    return [P[i * rows:(i + 1) * rows] for i in range(n)]


def _iota(shape, dim):
    return lax.broadcasted_iota(jnp.int32, shape, dim)


def _shr(x, n):
    return lax.shift_right_logical(x, jnp.int32(n.bit_length() - 1))


def _sigmoid(x):
    return 1.0 / (1.0 + jnp.exp(-x))


def _silu(x):
    return x * _sigmoid(x)


def _softplus(x):
    return jnp.maximum(x, 0.0) + jnp.log(1.0 + jnp.exp(-jnp.abs(x)))


def _rms(x, gain):
    return x * lax.rsqrt(jnp.mean(x * x, axis=-1, keepdims=True) + NORM_EPS) * gain


def _make_masks(C, Ls):
    row = _iota((C, C), 0)
    col = _iota((C, C), 1)
    seq = lambda t: _shr(t, Ls)
    blk = lambda t: _shr(t, SUB)
    same = seq(row) == seq(col)
    R = _iota((2 * C, 2 * C), 0)
    Q = _iota((2 * C, 2 * C), 1)
    tr = R & (C - 1)
    tq = Q & (C - 1)
    same2 = seq(tr) == seq(tq)
    lvl = []
    s = 1
    while s < Ls:
        lvl.append((_shr(R, 2 * s) == _shr(Q, 2 * s)) & ((R & (2 * s - 1)) >= s) & ((Q & (2 * s - 1)) < s))
        s *= 2
    lane2 = _iota((2 * C, HEAD_PAIR), 1)
    row2 = _iota((2 * C, HEAD_PAIR), 0)
    pr = _iota((HEAD_PAIR, HEAD_PAIR), 0)
    pc = _iota((HEAD_PAIR, HEAD_PAIR), 1)
    return dict(
        row=row, col=col, same=same,
        mi=same & (col <= row),
        diag_blocks=same & (col <= row) & (blk(row) == blk(col)),
        off_blocks=[same & (blk(col) == j) & (blk(row) > j) for j in range(C // SUB - 1)],
        eye2=(R == Q).astype(F32),
        lvl=lvl,
        ms_hi2=same2 & (tq < tr) & (Q >= C),
        mi22=same2 & (tq <= tr),
        lo_half=_iota((C, HEAD_PAIR), 1) < RWKV_HEAD_DIM,
        own=(lane2 < RWKV_HEAD_DIM) == (row2 < C),
        top_left=(Q < C) & (R < C), bottom_left=(Q < C) & (R >= C),
        seq_of_row2=seq(row2 & (C - 1)),
        seq_of_row=seq(_iota((C, LANES), 0)),
        head_diag=_shr(pr, RWKV_HEAD_DIM) == _shr(pc, RWKV_HEAD_DIM),
    )


def _tri_inverse(G, mk):
    lvl = mk['lvl']
    D = mk['eye2'] + jnp.where(lvl[0], G, 0.0)
    for m in lvl[1:]:
        LD = _dot(jnp.where(m, G, 0.0), D)
        yield
        D = D + _dot(D, LD)
        yield
    return D


def _lockstep(streams):
    streams = list(streams)
    out = [None] * len(streams)
    alive = list(range(len(streams)))
    while alive:
        for i in list(alive):
            try:
                next(streams[i])
            except StopIteration as stop:
                out[i] = stop.value
                alive.remove(i)
    return out


def _drain(gen):
    while True:
        try:
            next(gen)
        except StopIteration as stop:
            return stop.value


def _select_seq(full, n_seq, sid_rows):
    if n_seq == 1:
        return full
    acc = jnp.where(sid_rows == 0, full[:, 0:LANES], 0.0)
    for s in range(1, n_seq):
        acc = acc + jnp.where(sid_rows == s, full[:, s * LANES:(s + 1) * LANES], 0.0)
    return acc


def _expand_seq(x, n_seq, sid_rows):
    if n_seq == 1:
        return x
    return jnp.concatenate([jnp.where(sid_rows == s, x, 0.0) for s in range(n_seq)], axis=1)


def _wkv_pair_chunk(r, lw, cw, wl, k, v, kk, a, S, mk, Ls):
    C = r.shape[0]
    n_seq = C // Ls
    lo_half = mk['lo_half']
    stack = lambda t: jnp.where(mk['own'], jnp.concatenate([t, t], axis=0), 0.0)
    e_neg = jnp.exp(-cw)
    at = -kk * jnp.exp(cw - lw)
    bt = kk * a * e_neg
    kt = k * e_neg
    rt = r * jnp.exp(cw)
    Y = jnp.concatenate([bt, kt], axis=0)
    gar = _dot_nt(jnp.concatenate([stack(at), stack(rt)], axis=0), Y)
    ga = gar[:2 * C]
    G = jnp.where(mk['top_left'], ga, 0.0) + pltpu.roll(jnp.where(mk['bottom_left'], ga, 0.0), C, 1)
    gr = gar[2 * C:]
    sid2 = mk['seq_of_row2']
    XO = _select_seq(_dot_nt(jnp.concatenate([at, rt], axis=0), S), n_seq, sid2)
    yield
    Vz = jnp.concatenate([jnp.zeros_like(v), v], axis=0)
    xk = _dot(jnp.where(mk['ms_hi2'], ga, 0.0), Vz)
    D = yield from _tri_inverse(G, mk)
    X = XO[:C] + jnp.where(lo_half, xk[:C], xk[C:])
    DX = _dot(D, jnp.concatenate([X, X], axis=0))
    yield
    U = jnp.where(lo_half, DX[:C], DX[C:])
    Z = jnp.concatenate([U, v], axis=0)
    oz = _dot(jnp.where(mk['mi22'], gr, 0.0), Z)
    upd = _dot(Z.T, _expand_seq(Y, n_seq, sid2))
    yield
    o = XO[C:] + jnp.where(lo_half, oz[:C], oz[C:])
    bd = mk['head_diag']
    new = []
    for s in range(n_seq):
        Ss = S[s * LANES:(s + 1) * LANES, :]
        new.append(jnp.where(bd, (Ss + upd[:, s * LANES:(s + 1) * LANES]) * wl[s * Ls:s * Ls + 1, :], 0.0))
    return o, new


def _hgrn_head_chunk(q, kf, b, blast, iv, ST, mk, Ls):
    C = q.shape[0]
    assert Ls == C or Ls <= SUB
    n_seq = C // Ls
    sid = mk['seq_of_row']
    if Ls > SUB:
        rows_of = lambda off: jnp.concatenate(
            [jnp.broadcast_to(b[SUB * j + off:SUB * j + off + 1, :], (SUB, LANES)) for j in range(C // SUB)], axis=0)
        bmid = rows_of(SUB // 2 - 1)
        bend = rows_of(SUB - 1)
    else:
        bmid = 0.0
    o = _select_seq(_dot_nt(q * jnp.exp(b), ST), n_seq, sid)
    att = jnp.where(mk['diag_blocks'],
                    _dot_nt(q * jnp.exp(b - bmid), kf * jnp.exp(bmid - b)), 0.0)
    if Ls > SUB:
        ko = kf * jnp.exp(jnp.minimum(bend - b, 0.0))
        nb = C // SUB - 1
        qo = [q * jnp.exp(jnp.minimum(b - b[SUB * j + SUB - 1:SUB * j + SUB, :], 0.0)) for j in range(nb)]
        off = _dot_nt(jnp.concatenate(qo, axis=0), ko)
        for j in range(nb):
            att = att + jnp.where(mk['off_blocks'][j], off[j * C:(j + 1) * C], 0.0)
    ke = kf * jnp.exp(blast - b)
    upd = _dot(iv.T, _expand_seq(ke, n_seq, sid))
    yield
    o = o + _dot(att, iv)
    yield
    dec = jnp.exp(blast)
    new = []
    for s in range(n_seq):
        new.append(ST[s * LANES:(s + 1) * LANES, :] * dec[s * Ls:s * Ls + 1, :]
                   + upd[:, s * LANES:(s + 1) * LANES])
    return o, new


def _ada_body(c_ref, w_ref, b_ref, o_ref):
    o_ref[...] = _dot(_silu(c_ref[...]), w_ref[...]) + b_ref[...]


def _inproj_body(rep, x_ref, sh_ref, sc_ref, n1_ref, w_ref, p_ref):
    h = _rms(x_ref[...], n1_ref[...]) * (1.0 + _repeat_rows(sc_ref[...], rep)) + _repeat_rows(sh_ref[...], rep)
    p_ref[...] = jnp.dot(h.astype(BF16), w_ref[...], preferred_element_type=F32)


PREP_FIELDS = ("r", "lw", "kmod", "v", "kk", "a", "g", "bonus", "q", "kf", "iv", "og")


def _rec_body(Ls, single_seq, R, fuse_in, *refs):
    refs = list(refs)
    if fuse_in:
        xfirst_ref, xnext_ref, sh1_ref, sc1_ref, n1_ref, win_ref = refs[:6]
        del refs[:6]
        p_ref = None
    else:
        p_ref = refs.pop(0)
    if single_seq:
        shinit_ref = wkvin_ref = hgin_ref = None
    else:
        shinit_ref, wkvin_ref, hgin_ref = refs[:3]
        del refs[:3]
    (mu_ref, w0_ref, a0_ref, kk_ref, ka_ref, rk_ref, lnw_ref, lnb_ref, lb_ref, gn_ref, wcomb_ref, wgate_ref,
     y_ref, shout_ref, wkvout_ref, hgout_ref, wkv_sc, hg_sc) = refs[:18]
    if fuse_in:
        assert single_seq
        prep_sc = dict(zip(PREP_FIELDS, refs[18:18 + len(PREP_FIELDS)]))
        cum_sc = refs[18 + len(PREP_FIELDS)]
    C = CHUNK
    n_seq = C // Ls
    NP = RWKV_HEADS // 2
    HD = RWKV_HEAD_DIM
    c = pl.program_id(1)
    rows = lambda j: slice(j * C, (j + 1) * C)
    blk = lambda s: slice(s * LANES, (s + 1) * LANES)
    pairs = [slice(hp * HEAD_PAIR, (hp + 1) * HEAD_PAIR) for hp in range(NP)]
    mk = _make_masks(C, Ls)
    gsum = mk['head_diag'].astype(F32)

    def prepare(pcols, first):
        rowv = _iota((C, SHIFT_WIDTH), 0)
        xs = []
        for j in range(R):
            p_rw = pcols(j, 0, SHIFT_WIDTH)
            if not single_seq:
                p_prev = jnp.where((rowv & (Ls - 1)) == 0, _repeat_rows(shinit_ref[j], Ls),
                                   pltpu.roll(p_rw, 1, 0))
                shout_ref[j] = _last_rows(p_rw, Ls)
            else:
                keep = shout_ref.shape[1]
                p_prev = jnp.where(rowv == 0, shout_ref[j, keep - 1:keep, :], pltpu.roll(p_rw, 1, 0))
                if first:
                    p_prev = jnp.where(rowv == 0, 0.0, p_prev)
                shout_ref[j] = p_rw[C - keep:, :]
            xs.append(p_rw + (p_prev - p_rw) * mu_ref[...])
        x = jnp.concatenate(xs, axis=0)
        cat = lambda lo, hi: jnp.concatenate([pcols(j, lo, hi) for j in range(R)], axis=0)
        r = x[:, 0:COL_K]
        k = x[:, COL_K:COL_V]
        v = x[:, COL_V:COL_LORA]
        wa = x[:, COL_LORA:COL_GATE]
        gd = x[:, COL_GATE:SHIFT_WIDTH]
        lane = _iota((R * C, LANES), 1)
        da = _dot(jnp.where(lane < D_DECAY_LORA, jnp.tanh(wa), wa), wcomb_ref[...])
        g = _dot(_sigmoid(gd), wgate_ref[...])
        yield
        w_log = -_softplus(-(w0_ref[...] + da[:, :RWKV_WIDTH])) - 0.5
        lw = -jnp.exp(w_log)
        a = _sigmoid(a0_ref[...] + da[:, RWKV_WIDTH:])
        kkr = k * kk_ref[...]
        kmod = k * (1.0 + (a - 1.0) * ka_ref[...])
        rkr = r * kmod * rk_ref[...]
        lbp = lb_ref[...]
        m = jnp.maximum(lbp[0:1, :], lbp[1:2, :])
        e0 = jnp.exp(lbp[0:1, :] - m)
        e1 = jnp.exp(lbp[1:2, :] - m)
        lb = e0 / (e0 + e1)
        q = _silu(cat(COL_Q, COL_F))
        f = lb + (1.0 - lb) * _sigmoid(cat(COL_F, COL_I))
        logs = jnp.concatenate([lw, jnp.log(f)], axis=1)
        cums, tots = [], []
        for j in range(R):
            if single_seq:
                cums.append(_dot_2pass_l(mk['mi'].astype(F32), logs[rows(j)]))
            else:
                both = _dot_2pass_l(jnp.concatenate([mk['mi'], mk['same']], axis=0).astype(F32), logs[rows(j)])
                cums.append(both[:C])
                tots.append(both[C:])
        sums = _group_sums([kkr[:, sl] * kkr[:, sl] for sl in pairs] + [rkr[:, sl] for sl in pairs], gsum)
        yield
        kk = jnp.concatenate([kkr[:, sl] / jnp.maximum(jnp.sqrt(sums[hp]), 1e-12)
                              for hp, sl in enumerate(pairs)], axis=1)
        bonus = jnp.concatenate(sums[NP:], axis=1) * v
        d = dict(r=r, lw=lw, kmod=kmod, v=v, kk=kk, a=a, g=g, bonus=bonus, q=q, kf=1.0 - f,
                 iv=cat(COL_I, COL_OG), og=cat(COL_OG, IN_WIDTH))
        return d, jnp.concatenate(cums, axis=0), tots

    def next_chunk(x_ref, first):
        hs = [_rms(x_ref[j], n1_ref[...]) * (1.0 + sc1_ref[j]) + sh1_ref[j] for j in range(R)]
        hn = jnp.concatenate(hs, axis=0).astype(BF16)
        tiles = []
        for t in range(IN_WIDTH // INPROJ_TILE):
            tiles.append(jnp.dot(hn, win_ref[:, t * INPROJ_TILE:(t + 1) * INPROJ_TILE],
                                 preferred_element_type=F32))
            yield
        pcols = lambda j, lo, hi: jnp.concatenate(tiles[lo // INPROJ_TILE:hi // INPROJ_TILE], axis=1)[rows(j)]
        d, cum, _ = yield from prepare(pcols, first)
        return d, cum

    def hand_over(d, cum):
        for name in PREP_FIELDS:
            prep_sc[name][...] = d[name]
        cum_sc[...] = cum

    @pl.when(c == 0)
    def _():
        shout_ref[...] = jnp.zeros_like(shout_ref)
        wkv_sc[...] = jnp.zeros_like(wkv_sc)
        if single_seq:
            hg_sc[...] = jnp.zeros_like(hg_sc)
        else:
            for j in range(R):
                for s in range(n_seq):
                    for hp in range(NP):
                        wkv_sc[j, hp, s * LANES:s * LANES + HD, 0:HD] = wkvin_ref[j * n_seq + s, 2 * hp]
                        wkv_sc[j, hp, s * LANES + HD:(s + 1) * LANES, HD:2 * HD] = wkvin_ref[j * n_seq + s, 2 * hp + 1]
                    for h in range(HGRN_HEADS):
                        hg_sc[j, h, blk(s), :] = hgin_ref[j * n_seq + s, h].T
        if fuse_in:
            hand_over(*_drain(next_chunk(xfirst_ref, True)))

    if fuse_in:
        d = {name: prep_sc[name][...] for name in PREP_FIELDS}
        cum = cum_sc[...]
        extra = [next_chunk(xnext_ref, False)]
    else:
        d, cum, tots = _drain(prepare(lambda j, lo, hi: p_ref[j, :, lo:hi], True))
        extra = []
    if single_seq:
        tots = [cum[j * C + C - 1:j * C + C, :] for j in range(R)]
    lnw = lnw_ref[...]
    lnb = lnb_ref[...]

    def wkv_stream(j, hp):
        sl = pairs[hp]
        o, new = yield from _wkv_pair_chunk(d["r"][rows(j), sl], d["lw"][rows(j), sl], cum[rows(j), sl],
                                            jnp.exp(tots[j][:, sl]), d["kmod"][rows(j), sl], d["v"][rows(j), sl],
                                            d["kk"][rows(j), sl], d["a"][rows(j), sl], wkv_sc[j, hp], mk, Ls)
        for s in range(n_seq):
            wkv_sc[j, hp, blk(s), :] = new[s]
        return o

    def hgrn_stream(j, h):
        sl = slice(RWKV_WIDTH + h * LANES, RWKV_WIDTH + (h + 1) * LANES)
        hs = blk(h)
        o, new = yield from _hgrn_head_chunk(d["q"][rows(j), hs], d["kf"][rows(j), hs], cum[rows(j), sl],
                                             tots[j][:, sl], d["iv"][rows(j), hs], hg_sc[j, h], mk, Ls)
        for s in range(n_seq):
            hg_sc[j, h, blk(s), :] = new[s]
        on = _rms(o, gn_ref[...])
        y_ref[j, :, sl] = (on * _silu(d["og"][rows(j), hs])).astype(y_ref.dtype)

    outs = _lockstep([wkv_stream(j, hp) for j in range(R) for hp in range(NP)]
                     + [hgrn_stream(j, h) for j in range(R) for h in range(HGRN_HEADS)] + extra)

    os_ = [jnp.concatenate([outs[j * NP + hp] for j in range(R)], axis=0) for hp in range(NP)]
    means = _group_sums(os_, gsum)
    ds = [o - mu * (1.0 / RWKV_HEAD_DIM) for o, mu in zip(os_, means)]
    vars_ = _group_sums([dd * dd for dd in ds], gsum)
    for hp, sl in enumerate(pairs):
        on = ds[hp] * lax.rsqrt(vars_[hp] * (1.0 / RWKV_HEAD_DIM) + LNX_EPS) * lnw[:, sl] + lnb[:, sl]
        yv = ((on + d["bonus"][:, sl]) * d["g"][:, sl]).astype(y_ref.dtype)
        for j in range(R):
            y_ref[j, :, sl] = yv[rows(j)]

    if fuse_in:
        hand_over(*outs[-1])

    @pl.when(c == pl.num_programs(1) - 1)
    def _():
        for j in range(R):
            for s in range(n_seq):
                for hp in range(NP):
                    wkvout_ref[j * n_seq + s, 2 * hp] = wkv_sc[j, hp, s * LANES:s * LANES + HD, 0:HD]
                    wkvout_ref[j * n_seq + s, 2 * hp + 1] = wkv_sc[j, hp, s * LANES + HD:(s + 1) * LANES, HD:2 * HD]
                for h in range(HGRN_HEADS):
                    hgout_ref[j * n_seq + s, h] = hg_sc[j, h, blk(s), :].T


def _out_body(rep, x_ref, ym_ref, gt1_ref, sh2_ref, sc2_ref, gt2_ref, n2_ref, nf_ref,
              wo_ref, wu_ref, wd_ref, o_ref):
    gt1, sh2, sc2, gt2 = (_repeat_rows(m[...], rep) for m in (gt1_ref, sh2_ref, sc2_ref, gt2_ref))
    y = jnp.dot(ym_ref[...], wo_ref[...], preferred_element_type=F32)
    x1 = x_ref[...] + gt1 * y
    h = (_rms(x1, n2_ref[...]) * (1.0 + sc2) + sh2).astype(BF16)
    acc = jnp.zeros_like(x1)
    FC = FF_TILE
    for j in range(D_FF // FC):
        u = jnp.dot(h, wu_ref[:, j * FC:(j + 1) * FC], preferred_element_type=F32)
        u = jnp.square(jnp.maximum(u, 0.0)).astype(BF16)
        acc = acc + jnp.dot(u, wd_ref[j * FC:(j + 1) * FC, :], preferred_element_type=F32)
    x2 = x1 + gt2 * acc
    o_ref[...] = _rms(x2, nf_ref[...])


def _params(sem):
    return pltpu.CompilerParams(dimension_semantics=sem, vmem_limit_bytes=VMEM_LIMIT)


def _full(shape):
    return pl.BlockSpec(shape, lambda *_: (0,) * len(shape))


def _ada(c_all, w_ada, b_ada):
    n = c_all.shape[0]
    TN = 1024
    return pl.pallas_call(
        _ada_body,
        grid=(w_ada.shape[1] // TN,),
        in_specs=[pl.BlockSpec((n, D_MODEL), lambda j: (0, 0)),
                  pl.BlockSpec((D_MODEL, TN), lambda j: (0, j)),
                  pl.BlockSpec((1, TN), lambda j: (0, j))],
        out_specs=pl.BlockSpec((n, TN), lambda j: (0, j)),
        out_shape=jax.ShapeDtypeStruct((n, w_ada.shape[1]), F32),
        compiler_params=_params(("arbitrary",)),
        name="ada",
    )(c_all, w_ada, b_ada)


def _mod_specs(mod, cols, TM, rows_per_mod):
    if mod.ndim == 3:
        return [pl.BlockSpec((None, 1, D_MODEL), lambda i, c=c: (i * TM // rows_per_mod, 0, c)) for c in cols]
    return [pl.BlockSpec((TM // rows_per_mod, D_MODEL), lambda i, c=c: (i, c)) for c in cols]


def _mod_repeat(mod, rows_per_mod):
    return 1 if mod.ndim == 3 else rows_per_mod


def _inproj(x, mod, norm1, w_in, TM, rows_per_mod):
    M = x.shape[0]
    return pl.pallas_call(
        functools.partial(_inproj_body, _mod_repeat(mod, rows_per_mod)),
        grid=(M // TM,),
        in_specs=[pl.BlockSpec((TM, D_MODEL), lambda i: (i, 0))]
                 + _mod_specs(mod, (0, 1), TM, rows_per_mod)
                 + [_full((1, D_MODEL)), _full((D_MODEL, IN_WIDTH))],
        out_specs=pl.BlockSpec((TM, IN_WIDTH), lambda i: (i, 0)),
        out_shape=jax.ShapeDtypeStruct((M, IN_WIDTH), F32),
        compiler_params=_params(("arbitrary",)),
        name="inproj",
    )(x, mod, mod, norm1, w_in)


def _rec(p, n_groups, states, small, wcomb, wgate, Ls, n_batch, R, inproj=None):
    Gt, C = n_groups, CHUNK
    rows_total = (p if inproj is None else inproj[0]).shape[1]
    NC = rows_total // C
    n_seq = C // Ls
    keep = 8 if states is None else n_seq
    wkv_spec = pl.BlockSpec((R * n_seq, RWKV_HEADS, RWKV_HEAD_DIM, RWKV_HEAD_DIM), lambda g, c: (g, 0, 0, 0))
    hg_spec = pl.BlockSpec((R * n_seq, HGRN_HEADS, LANES, LANES), lambda g, c: (g, 0, 0, 0))
    sh_spec = pl.BlockSpec((R, n_seq, SHIFT_WIDTH), lambda g, c: (g, 0, 0))
    sho_spec = pl.BlockSpec((R, keep, SHIFT_WIDTH), lambda g, c: (g, 0, 0))
    st_in = [] if states is None else list(states)
    st_specs = [] if states is None else [sh_spec, wkv_spec, hg_spec]
    scratch = [pltpu.VMEM((R, RWKV_HEADS // 2, n_seq * LANES, LANES), F32),
               pltpu.VMEM((R, HGRN_HEADS, n_seq * LANES, LANES), F32)]
    if inproj is None:
        lead_in = [p]
        lead_specs = [pl.BlockSpec((R, C, IN_WIDTH), lambda g, c: (g, c, 0))]
    else:
        x, mod, norm1, w_in = inproj
        lead_in = [x, x, mod, mod, norm1, w_in]
        lead_specs = [pl.BlockSpec((R, C, D_MODEL), lambda g, c: (g, 0, 0)),
                      pl.BlockSpec((R, C, D_MODEL), lambda g, c: (g, jnp.minimum(c + 1, NC - 1), 0)),
                      pl.BlockSpec((R, 1, D_MODEL), lambda g, c: (g, 0, 0)),
                      pl.BlockSpec((R, 1, D_MODEL), lambda g, c: (g, 0, 1)),
                      _full(norm1.shape), _full(w_in.shape)]
        scratch += [pltpu.VMEM((R * C, RWKV_WIDTH), F32) for _ in PREP_FIELDS]
        scratch.append(pltpu.VMEM((R * C, 2 * RWKV_WIDTH), F32))
    return pl.pallas_call(
        functools.partial(_rec_body, Ls, states is None, R, inproj is not None),
        grid=(Gt // R, NC),
        in_specs=lead_specs + st_specs
                 + [_full(s.shape) for s in small] + [_full(wcomb.shape), _full(wgate.shape)],
        out_specs=[pl.BlockSpec((R, C, D_MODEL), lambda g, c: (g, c, 0)), sho_spec, wkv_spec, hg_spec],
        out_shape=[jax.ShapeDtypeStruct((Gt, NC * C, D_MODEL), BF16),
                   jax.ShapeDtypeStruct((Gt, keep, SHIFT_WIDTH), F32),
                   jax.ShapeDtypeStruct((n_batch, RWKV_HEADS, RWKV_HEAD_DIM, RWKV_HEAD_DIM), F32),
                   jax.ShapeDtypeStruct((n_batch, HGRN_HEADS, LANES, LANES), F32)],
        scratch_shapes=scratch,
        compiler_params=_params(("arbitrary", "arbitrary")),
        name="rec",
    )(*lead_in, *st_in, *small, wcomb, wgate)


def _out(x, ym, mod, norm2, norm_f, w_out, w_up, w_down, TM, rows_per_mod):
    M = x.shape[0]
    row = lambda i: (i, 0)
    return pl.pallas_call(
        functools.partial(_out_body, _mod_repeat(mod, rows_per_mod)),
        grid=(M // TM,),
        in_specs=[pl.BlockSpec((TM, D_MODEL), row), pl.BlockSpec((TM, D_MODEL), row)]
                 + _mod_specs(mod, (2, 3, 4, 5), TM, rows_per_mod)
                 + [_full((1, D_MODEL)), _full((1, D_MODEL)),
                    _full(w_out.shape), _full(w_up.shape), _full(w_down.shape)],
        out_specs=pl.BlockSpec((TM, D_MODEL), row),
        out_shape=jax.ShapeDtypeStruct((M, D_MODEL), F32),
        compiler_params=_params(("arbitrary",)),
        name="outmlp",
    )(x, ym, mod, mod, mod, mod, norm2, norm_f, w_out, w_up, w_down)


def kernel(x_prompt, x_sample, c_prompt, c_sample, state_shift, state_wkv, state_hgrn, norm1, norm2, norm_f, w_ada, b_ada, w_in, mu_shift, w0, w_decay_up, a0, w_aaa_up, w_gate_up, k_k, k_a, r_k, lnx_w, lnx_b, hgrn_lb, hgrn_gnorm, w_out, w_up, w_down):
    BP, TP, _ = x_prompt.shape
    BS, TS, _ = x_sample.shape
    l = 0
    row = lambda t: t.reshape(1, -1)

    mod = _ada(jnp.concatenate([c_prompt, c_sample], axis=0), w_ada[l], row(b_ada[l]))
    mod_p = mod[:BP].reshape(BP, 1, 6 * D_MODEL)
    mod_s = mod[BP:]

    w_in_b = w_in[l].astype(BF16)
    w_out_b = w_out[l].astype(BF16)
    w_up_b = w_up[l].astype(BF16)
    w_down_b = w_down[l].astype(BF16)
    zer = jnp.zeros((D_DECAY_LORA, RWKV_WIDTH), F32)
    wcomb = jnp.concatenate([jnp.concatenate([w_decay_up[l], zer], axis=1),
                             jnp.concatenate([zer, w_aaa_up[l]], axis=1)], axis=0).astype(BF16)
    wgate = w_gate_up[l].astype(BF16)
    small = [row(mu_shift[l]), row(w0[l]), row(a0[l]), row(k_k[l]), row(k_a[l]), row(r_k[l]),
             row(lnx_w[l]), row(lnx_b[l]), hgrn_lb, row(hgrn_gnorm[l])]
    n1, n2, nf = row(norm1[l]), row(norm2[l]), row(norm_f)

    def trunk(x2d, modx, rows_per_mod, Gt, states, Ls, n_batch, TM, R, fuse_in):
        if fuse_in:
            ym, shout, wkv_o, hg_o = _rec(None, Gt, states, small, wcomb, wgate, Ls, n_batch, R,
                                          inproj=(x2d.reshape(Gt, -1, D_MODEL), modx, n1, w_in_b))
        else:
            p = _inproj(x2d, modx, n1, w_in_b, TM, rows_per_mod)
            ym, shout, wkv_o, hg_o = _rec(p.reshape(Gt, -1, IN_WIDTH), Gt, states, small, wcomb, wgate,
                                          Ls, n_batch, R)
        y = _out(x2d, ym.reshape(-1, D_MODEL), modx, n2, nf, w_out_b, w_up_b, w_down_b, TM, rows_per_mod)
        return y, shout, wkv_o, hg_o

    n_seq = CHUNK // TS
    GS = BS // n_seq
    shinit_s = state_shift[l].reshape(GS, n_seq, SHIFT_WIDTH)
    ys, shs, wkvs, hgs = trunk(x_sample.reshape(BS * TS, D_MODEL), mod_s, TS, GS,
                               (shinit_s, state_wkv[l], state_hgrn[l]), TS, BS, ROWS_SAMPLE, 1, False)
    y_sample = ys.reshape(BS, TS, D_MODEL)
    shift_s = shs.reshape(BS, SHIFT_WIDTH)[None]
    wkv_s = wkvs[None]
    hgrn_s = hgs[None]

    yp, shp, wkvp, hgp = trunk(x_prompt.reshape(BP * TP, D_MODEL), mod_p, TP, BP, None, CHUNK, BP, OUT_ROWS_PROMPT,
                               PROMPT_GROUPS_PER_STEP, True)
    y_prompt = yp.reshape(BP, TP, D_MODEL)
    shift_p = shp[:, -1][None]
    wkv_p = wkvp[None]
    hgrn_p = hgp[None]

    return (y_prompt, y_sample, shift_p, wkv_p, hgrn_p, shift_s, wkv_s, hgrn_s)
```
